```python
import jax
import jax.numpy as jnp
from jax import lax
import numpy as np

D_MODEL = 1024
BATCH = 1
SEQ = 16384
DEPTH = 2

HEAD_DIM = 64
N_HEADS_A = 8
WIDTH_A = N_HEADS_A * HEAD_DIM
DILATION_PATTERNS = ((128, 1), (512, 4), (2048, 16))
ROPE_DIM = HEAD_DIM // 4
ROPE_THETA = 500000.0
N_HEADS_B = 8
WIDTH_B = N_HEADS_B * HEAD_DIM
DECAY_LORA = 64
ICLR_LORA = 64
GATE_LORA = 128
RWKV_LN_EPS = 64e-5
WIDTH_C = 512
CONV_WIDTH = 3
N_HEADS_D = 4
HEAD_DIM_D = 128
WIDTH_D = N_HEADS_D * HEAD_DIM_D
DN_CHUNK = 64
D_FF_DENSE = 2816
N_EXPERTS = 8
TOP_K = 2
D_FF_EXPERT = 3584
EVEN_IN = 3 * WIDTH_A + 3 * WIDTH_B
ODD_SPLITS = (WIDTH_C, 2 * WIDTH_C, 3 * WIDTH_C, 3 * WIDTH_C + 3 * WIDTH_D, 3 * WIDTH_C + 4 * WIDTH_D, 3 * WIDTH_C + 4 * WIDTH_D + 2 * N_HEADS_D)
ODD_IN = ODD_SPLITS[-1] + 2 * N_HEADS_D
NORM_EPS = 1e-6
NEG_INF = -1e30

kernel_name = "hybrid_dilated_rwkv7_conv_deltanet_moe"


def rmsnorm(x, w, eps=NORM_EPS):
    xf = x.astype(jnp.float32)
    y = xf * lax.rsqrt(jnp.mean(xf * xf, axis=-1, keepdims=True) + eps)
    return (y * w.astype(jnp.float32)).astype(x.dtype)


def l2norm(x, eps=1e-6):
    xf = x.astype(jnp.float32)
    return (xf * lax.rsqrt(jnp.sum(xf * xf, axis=-1, keepdims=True) + eps)).astype(x.dtype)


def shift_prev(t):
    return jnp.pad(t, ((0, 0), (1, 0), (0, 0)))[:, :-1]


def shift_next(t):
    return jnp.pad(t, ((0, 0), (0, 1), (0, 0)))[:, 1:]


def conv3_centred(t, w):
    return w[0] * shift_prev(t) + w[1] * t + w[2] * shift_next(t)


def both(t, seq_axis):
    return jnp.stack([t, jnp.flip(t, axis=seq_axis)])


def orient(t, seq_axis):
    return jnp.stack([t[0], jnp.flip(t[1], axis=seq_axis)])


def unorient(t, seq_axis):
    return t[0] + jnp.flip(t[1], axis=seq_axis)


def swiglu(h, w_gate, w_up, w_down):
    return (jax.nn.silu(h @ w_gate) * (h @ w_up)) @ w_down


def partial_rope(t, positions):
    half = ROPE_DIM // 2
    inv_freq = jnp.power(ROPE_THETA, -jnp.arange(half, dtype=jnp.float32) / half)
    ang = positions[:, None, :, None].astype(jnp.float32) * inv_freq
    cos, sin = jnp.cos(ang), jnp.sin(ang)
    tf = t[..., :ROPE_DIM].astype(jnp.float32)
    x1, x2 = tf[..., :half], tf[..., half:]
    rot = jnp.concatenate([x1 * cos - x2 * sin, x2 * cos + x1 * sin], axis=-1)
    return jnp.concatenate([rot.astype(t.dtype), t[..., ROPE_DIM:]], axis=-1)


def banded_attention(q, k, v, radius):
    Bb, H, R, L, Dh = q.shape
    blk = radius
    nb = -(-L // blk)
    pad = nb * blk - L
    cfg = lambda lo, hi: ((0, 0), (0, 0), (0, 0), (lo, hi), (0, 0))
    qb = jnp.pad(q, cfg(0, pad)).reshape(Bb, H, R, nb, blk, Dh)

    def windows(t):
        tp = jnp.pad(t, cfg(blk, blk + pad)).reshape(Bb, H, R, nb + 2, blk, Dh)
        return jnp.concatenate([tp[:, :, :, :-2], tp[:, :, :, 1:-1], tp[:, :, :, 2:]], axis=-2)

    kw, vw = windows(k), windows(v)
    s = jnp.einsum('bhrnid,bhrnjd->bhrnij', qb, kw).astype(jnp.float32)
    qi = (jnp.arange(nb)[:, None] * blk + jnp.arange(blk)[None, :])[:, :, None]
    kj = (jnp.arange(nb)[:, None] * blk - blk + jnp.arange(3 * blk)[None, :])[:, None, :]
    valid = (jnp.abs(qi - kj) <= radius) & (kj >= 0) & (kj < L)
    s = jnp.where(valid, s, NEG_INF)
    m = jnp.max(s, axis=-1)
    p = jnp.exp(s - m[..., None])
    l = jnp.sum(p, axis=-1)
    acc = jnp.einsum('bhrnij,bhrnjd->bhrnid', p, vw.astype(jnp.float32))
    m = m.reshape(Bb, H, R, nb * blk)[..., :L]
    l = l.reshape(Bb, H, R, nb * blk)[..., :L]
    acc = acc.reshape(Bb, H, R, nb * blk, Dh)[..., :L, :]
    return m, l, acc


def dilated_branch(q, k, v, window, dilation):
    Bb, H, S, Dh = q.shape
    radius = window // (2 * dilation)
    L = S // dilation
    to_sub = lambda t: t.reshape(Bb, H, L, dilation, Dh).transpose(0, 1, 3, 2, 4)
    m, l, acc = banded_attention(to_sub(q), to_sub(k), to_sub(v), radius)
    m = m.transpose(0, 1, 3, 2).reshape(Bb, H, S)
    l = l.transpose(0, 1, 3, 2).reshape(Bb, H, S)
    acc = acc.transpose(0, 1, 3, 2, 4).reshape(Bb, H, S, Dh)
    return m, l, acc


def dilated_attention(q, k, v):
    branches = [dilated_branch(q, k, v, w, d) for (w, d) in DILATION_PATTERNS]
    m_all = branches[0][0]
    for m, _, _ in branches[1:]:
        m_all = jnp.maximum(m_all, m)
    num = jnp.zeros(q.shape, jnp.float32)
    den = jnp.zeros(q.shape[:-1], jnp.float32)
    for m, l, acc in branches:
        scale = jnp.exp(m - m_all)
        num = num + scale[..., None] * acc
        den = den + scale * l
    return (num / den[..., None]).astype(q.dtype)


def rwkv7_bidir_scan(r, w, k, a, b, v):
    Z, Bb, S, H, N = r.shape
    xs = tuple(jnp.moveaxis(t.astype(jnp.float32), 2, 0) for t in (r, w, k, a, b, v))
    state0 = jnp.zeros((Z, Bb, H, N, N), jnp.float32)

    def step(state, inp):
        rt, wt, kt, at, bt, vt = inp
        sa = jnp.einsum('zbhvk,zbhk->zbhv', state, at)
        state = state * wt[..., None, :] + sa[..., None] * bt[..., None, :] + vt[..., None] * kt[..., None, :]
        return state, jnp.einsum('zbhvk,zbhk->zbhv', state, rt)

    _, ys = lax.scan(step, state0, xs)
    return jnp.moveaxis(ys, 0, 2)


def rwkv7_mixer(h, rkv, shift_mu, lora_mu, w0, w1, w2, a0, a1, a2, g1, g2, k_k, k_a, r_k, ln_w, ln_b):
    Bb, S, _ = h.shape
    rkv = rkv + shift_mu[0] * (shift_prev(rkv) - rkv) + shift_mu[1] * (shift_next(rkv) - rkv)
    r, k, v = jnp.split(rkv, 3, axis=-1)
    hx = h + lora_mu[0] * (shift_prev(h) - h) + lora_mu[1] * (shift_next(h) - h)
    w_pre = w0[:, None, None, :] + jnp.einsum('zbsr,zrc->zbsc', jnp.tanh(jnp.einsum('bsd,zdr->zbsr', hx, w1)), w2)
    decay = jnp.exp(-jnp.exp(-jax.nn.softplus(-w_pre.astype(jnp.float32)) - 0.5))
    iclr = jax.nn.sigmoid(a0[:, None, None, :] + jnp.einsum('zbsr,zrc->zbsc', jnp.einsum('bsd,zdr->zbsr', hx, a1), a2))
    gate = jax.nn.sigmoid(hx @ g1) @ g2
    heads = lambda t: t.reshape(*t.shape[:-1], N_HEADS_B, HEAD_DIM)
    kk = l2norm(heads(k * k_k))
    k_dir = heads(k[None] * (1.0 + (iclr - 1.0) * k_a))
    r_h, v_h = heads(r), heads(v)
    b_vec = kk[None] * heads(iclr)
    a_vec = jnp.broadcast_to(-kk[None], b_vec.shape)
    ys = rwkv7_bidir_scan(both(r_h, 1), orient(heads(decay), 1), orient(k_dir, 1), orient(a_vec, 1), orient(b_vec, 1), both(v_h, 1))
    yf = unorient(ys, 1)
    mu = jnp.mean(yf, axis=-1, keepdims=True)
    var = jnp.mean(jnp.square(yf - mu), axis=-1, keepdims=True)
    yn = ((yf - mu) * lax.rsqrt(var + RWKV_LN_EPS)).reshape(Bb, S, WIDTH_B) * ln_w + ln_b
    bonus = (jnp.sum(r_h * (k_dir[0] + k_dir[1]) * r_k, axis=-1, keepdims=True) * v_h).reshape(Bb, S, WIDTH_B)
    return ((yn + bonus) * gate).astype(h.dtype)


def even_mixer(h, positions, w_in, q_norm, k_norm, shift_mu, lora_mu, w0, w1, w2, a0, a1, a2, g1, g2, k_k, k_a, r_k, ln_w, ln_b, w_out):
    Bb, S, _ = h.shape
    proj = h @ w_in
    a_qkv, b_rkv = proj[..., :3 * WIDTH_A], proj[..., 3 * WIDTH_A:]
    q, k, v = (t.reshape(Bb, S, N_HEADS_A, HEAD_DIM).transpose(0, 2, 1, 3) for t in jnp.split(a_qkv, 3, axis=-1))
    q = partial_rope(rmsnorm(q, q_norm), positions) * HEAD_DIM ** -0.5
    k = partial_rope(rmsnorm(k, k_norm), positions)
    y_a = dilated_attention(q, k, v).transpose(0, 2, 1, 3).reshape(Bb, S, WIDTH_A)
    y_b = rwkv7_mixer(h, b_rkv, shift_mu, lora_mu, w0, w1, w2, a0, a1, a2, g1, g2, k_k, k_a, r_k, ln_w, ln_b)
    return jnp.concatenate([y_a, y_b], axis=-1) @ w_out


def chunk_gated_delta_rule(q, k, v, g, beta):
    out_dtype = v.dtype
    q, k, v, g, beta = (t.astype(jnp.float32) for t in (q, k, v, g, beta))
    Z, Bb, H, T, Dk = q.shape
    Dv = v.shape[-1]
    C = DN_CHUNK
    n = T // C
    q = q.reshape(Z, Bb, H, n, C, Dk)
    k = k.reshape(Z, Bb, H, n, C, Dk)
    v = v.reshape(Z, Bb, H, n, C, Dv)
    g = g.reshape(Z, Bb, H, n, C)
    beta = beta.reshape(Z, Bb, H, n, C)
    gc = jnp.cumsum(g, axis=-1)
    incl = jnp.tril(jnp.ones((C, C), bool))
    strict = jnp.tril(jnp.ones((C, C), bool), -1)
    diff = gc[..., :, None] - gc[..., None, :]
    decay = jnp.where(incl, jnp.exp(jnp.where(incl, diff, 0.0)), 0.0)
    kb = k * beta[..., None]
    lhs = jnp.where(strict, jnp.einsum('zbhnid,zbhnjd->zbhnij', kb, k) * decay, 0.0) + jnp.eye(C, dtype=jnp.float32)
    u_base = lax.linalg.triangular_solve(lhs, v * beta[..., None], left_side=True, lower=True, unit_diagonal=True)
    k_cum = lax.linalg.triangular_solve(lhs, kb * jnp.exp(gc)[..., None], left_side=True, lower=True, unit_diagonal=True)
    attn = jnp.einsum('zbhnid,zbhnjd->zbhnij', q, k) * decay
    xs = tuple(jnp.moveaxis(t, 3, 0) for t in (q, k, gc, u_base, k_cum, attn))

    def step(state, inp):
        qc, kc, gcc, ub, kcm, at = inp
        u = ub - jnp.einsum('zbhcd,zbhdv->zbhcv', kcm, state)
        o = jnp.einsum('zbhcd,zbhdv->zbhcv', qc * jnp.exp(gcc)[..., None], state) + jnp.einsum('zbhij,zbhjv->zbhiv', at, u)
        glast = gcc[..., -1]
        state = state * jnp.exp(glast)[..., None, None] + jnp.einsum('zbhcd,zbhcv->zbhdv', kc * jnp.exp(glast[..., None] - gcc)[..., None], u)
        return state, o

    state0 = jnp.zeros((Z, Bb, H, Dk, Dv), jnp.float32)
    _, o = lax.scan(step, state0, xs)
    return jnp.moveaxis(o, 0, 3).reshape(Z, Bb, H, T, Dv).astype(out_dtype)


def odd_mixer(h, w_in, conv_c, conv_dn, A_log, dt_bias, dn_norm, w_out):
    Bb, S, _ = h.shape
    c_b, c_c, c_x, dn_qkv, dn_z, dn_beta, dn_alpha = jnp.split(h @ w_in, ODD_SPLITS, axis=-1)
    y_c = c_b * conv3_centred(c_c * c_x, conv_c)
    qkv = jax.nn.silu(conv3_centred(dn_qkv, conv_dn))
    q, k, v = (t.reshape(Bb, S, N_HEADS_D, HEAD_DIM_D).transpose(0, 2, 1, 3) for t in jnp.split(qkv, 3, axis=-1))
    q = l2norm(q) * HEAD_DIM_D ** -0.5
    k = l2norm(k)
    beta = jax.nn.sigmoid(dn_beta).reshape(Bb, S, 2, N_HEADS_D).transpose(2, 0, 3, 1)
    alpha = dn_alpha.reshape(Bb, S, 2, N_HEADS_D).transpose(2, 0, 3, 1).astype(jnp.float32)
    g = -jnp.exp(A_log.astype(jnp.float32))[:, None, :, None] * jax.nn.softplus(alpha + dt_bias[:, None, :, None])
    o = chunk_gated_delta_rule(both(q, 2), both(k, 2), both(v, 2), orient(g, 2), orient(beta, 2))
    o = unorient(o, 2).transpose(0, 2, 1, 3)
    y_d = (rmsnorm(o, dn_norm) * jax.nn.silu(dn_z.reshape(Bb, S, N_HEADS_D, HEAD_DIM_D))).reshape(Bb, S, WIDTH_D)
    return jnp.concatenate([y_c, y_d], axis=-1) @ w_out


def moe_swiglu(h, router, w_gate, w_up, w_down):
    logits = (h @ router).astype(jnp.float32)
    probs = jax.nn.softmax(logits, axis=-1)
    top_p, top_i = lax.top_k(probs, TOP_K)
    top_p = top_p / jnp.sum(top_p, axis=-1, keepdims=True)
    gates = jnp.einsum('bsk,bske->bse', top_p, jax.nn.one_hot(top_i, N_EXPERTS, dtype=jnp.float32)).astype(h.dtype)
    out = jnp.zeros_like(h)
    for e in range(N_EXPERTS):
        out = out + gates[..., e:e + 1] * swiglu(h, w_gate[e], w_up[e], w_down[e])
    return out


def setup_inputs(seed: int = 0) -> dict:
    key = jax.random.key(seed)
    ks = iter(jax.random.split(key, 64))
    ne, no = (DEPTH + 1) // 2, DEPTH // 2
    D = D_MODEL

    def nrm(shape, scale):
        return scale * jax.random.normal(next(ks), shape, jnp.float32)

    def gain(shape):
        return 1.0 + 0.05 * jax.random.normal(next(ks), shape, jnp.float32)

    def unif(shape, lo, hi):
        return jax.random.uniform(next(ks), shape, jnp.float32, lo, hi)

    return {
        "x": nrm((BATCH, SEQ, D), 1.0),
        "positions": jnp.broadcast_to(jnp.arange(SEQ, dtype=jnp.int32), (BATCH, SEQ)),
        "ev_mix_norm": gain((ne, D)),
        "ev_w_in": nrm((ne, D, EVEN_IN), D ** -0.5),
        "ev_q_norm": gain((ne, HEAD_DIM)),
        "ev_k_norm": gain((ne, HEAD_DIM)),
        "ev_shift_mu": unif((ne, 2, 3 * WIDTH_B), 0.0, 0.5),
        "ev_lora_mu": unif((ne, 2, D), 0.0, 0.5),
        "ev_w0": unif((ne, 2, WIDTH_B), -4.0, 1.0),
        "ev_w1": nrm((ne, 2, D, DECAY_LORA), D ** -0.5),
        "ev_w2": nrm((ne, 2, DECAY_LORA, WIDTH_B), 0.5 * DECAY_LORA ** -0.5),
        "ev_a0": nrm((ne, 2, WIDTH_B), 0.5),
        "ev_a1": nrm((ne, 2, D, ICLR_LORA), D ** -0.5),
        "ev_a2": nrm((ne, 2, ICLR_LORA, WIDTH_B), 0.5 * ICLR_LORA ** -0.5),
        "ev_g1": nrm((ne, D, GATE_LORA), D ** -0.5),
        "ev_g2": nrm((ne, GATE_LORA, WIDTH_B), GATE_LORA ** -0.5),
        "ev_k_k": gain((ne, WIDTH_B)),
        "ev_k_a": gain((ne, WIDTH_B)),
        "ev_r_k": nrm((ne, N_HEADS_B, HEAD_DIM), 0.1),
        "ev_ln_w": gain((ne, WIDTH_B)),
        "ev_ln_b": nrm((ne, WIDTH_B), 0.02),
        "ev_w_out": nrm((ne, WIDTH_A + WIDTH_B, D), (WIDTH_A + WIDTH_B) ** -0.5),
        "ev_ffn_norm": gain((ne, D)),
        "ev_ffn_gate": nrm((ne, D, D_FF_DENSE), D ** -0.5),
        "ev_ffn_up": nrm((ne, D, D_FF_DENSE), D ** -0.5),
        "ev_ffn_down": nrm((ne, D_FF_DENSE, D), D_FF_DENSE ** -0.5),
        "od_mix_norm": gain((no, D)),
        "od_w_in": nrm((no, D, ODD_IN), D ** -0.5),
        "od_conv_c": nrm((no, CONV_WIDTH, WIDTH_C), CONV_WIDTH ** -0.5),
        "od_conv_dn": nrm((no, CONV_WIDTH, 3 * WIDTH_D), CONV_WIDTH ** -0.5),
        "od_A_log": jnp.log(unif((no, 2, N_HEADS_D), 1.0, 16.0)),
        "od_dt_bias": jnp.log(jnp.expm1(unif((no, 2, N_HEADS_D), 0.001, 0.1))),
        "od_dn_norm": gain((no, HEAD_DIM_D)),
        "od_w_out": nrm((no, WIDTH_C + WIDTH_D, D), (WIDTH_C + WIDTH_D) ** -0.5),
        "od_ffn_norm": gain((no, D)),
        "od_router": nrm((no, D, N_EXPERTS), D ** -0.5),
        "od_moe_gate": nrm((no, N_EXPERTS, D, D_FF_EXPERT), D ** -0.5),
        "od_moe_up": nrm((no, N_EXPERTS, D, D_FF_EXPERT), D ** -0.5),
        "od_moe_down": nrm((no, N_EXPERTS, D_FF_EXPERT, D), D_FF_EXPERT ** -0.5),
    }


def reference(x, positions, ev_mix_norm, ev_w_in, ev_q_norm, ev_k_norm, ev_shift_mu, ev_lora_mu, ev_w0, ev_w1, ev_w2, ev_a0, ev_a1, ev_a2, ev_g1, ev_g2, ev_k_k, ev_k_a, ev_r_k, ev_ln_w, ev_ln_b, ev_w_out, ev_ffn_norm, ev_ffn_gate, ev_ffn_up, ev_ffn_down, od_mix_norm, od_w_in, od_conv_c, od_conv_dn, od_A_log, od_dt_bias, od_dn_norm, od_w_out, od_ffn_norm, od_router, od_moe_gate, od_moe_up, od_moe_down):
    for layer in range(DEPTH):
        i = layer // 2
        if layer % 2 == 0:
            h = rmsnorm(x, ev_mix_norm[i])
            x = x + even_mixer(h, positions, ev_w_in[i], ev_q_norm[i], ev_k_norm[i], ev_shift_mu[i], ev_lora_mu[i], ev_w0[i], ev_w1[i], ev_w2[i], ev_a0[i], ev_a1[i], ev_a2[i], ev_g1[i], ev_g2[i], ev_k_k[i], ev_k_a[i], ev_r_k[i], ev_ln_w[i], ev_ln_b[i], ev_w_out[i])
            h = rmsnorm(x, ev_ffn_norm[i])
            x = x + swiglu(h, ev_ffn_gate[i], ev_ffn_up[i], ev_ffn_down[i])
        else:
            h = rmsnorm(x, od_mix_norm[i])
            x = x + odd_mixer(h, od_w_in[i], od_conv_c[i], od_conv_dn[i], od_A_log[i], od_dt_bias[i], od_dn_norm[i], od_w_out[i])
            h = rmsnorm(x, od_ffn_norm[i])
            x = x + moe_swiglu(h, od_router[i], od_moe_gate[i], od_moe_up[i], od_moe_down[i])
    return x
```

```python
import functools

import jax
import jax.numpy as jnp
from jax import lax
from jax.experimental import pallas as pl
from jax.experimental.pallas import tpu as pltpu

F32 = jnp.float32
BF16 = jnp.bfloat16

HEAD_DIM = 64
N_HEADS_A = 8
WIDTH_A = N_HEADS_A * HEAD_DIM
DILATION_PATTERNS = ((128, 1), (512, 4), (2048, 16))
ROPE_DIM = HEAD_DIM // 4
ROPE_THETA = 500000.0
N_HEADS_B = 8
WIDTH_B = N_HEADS_B * HEAD_DIM
RWKV_LN_EPS = 64e-5
WIDTH_C = 512
N_HEADS_D = 4
HEAD_DIM_D = 128
WIDTH_D = N_HEADS_D * HEAD_DIM_D
CHUNK = 64
N_EXPERTS = 8
NORM_EPS = 1e-6
NEG_INF = -1e30

V7X_VMEM_LIMIT_BYTES = 56 * 1024 * 1024


def _params(*sem):
    return pltpu.CompilerParams(dimension_semantics=sem, vmem_limit_bytes=V7X_VMEM_LIMIT_BYTES)


def _bdot(a, b):
    return jnp.dot(a.astype(BF16), b.astype(BF16), preferred_element_type=F32)


def _bdot_nt(a, b):
    return lax.dot_general(a.astype(BF16), b.astype(BF16), (((1,), (1,)), ((), ())), preferred_element_type=F32)


def _bdot_tn(a, b):
    return lax.dot_general(a.astype(BF16), b.astype(BF16), (((0,), (0,)), ((), ())), preferred_element_type=F32)


def _split3(x):
    hi = x.astype(BF16)
    r1 = x - hi.astype(F32)
    mid = r1.astype(BF16)
    lo = (r1 - mid.astype(F32)).astype(BF16)
    return hi, mid, lo


def _rms(x, w):
    return x * lax.rsqrt(jnp.mean(x * x, axis=-1, keepdims=True) + NORM_EPS) * w


def _rmsnorm_kernel(x_ref, w_ref, o_ref):
    o_ref[...] = _rms(x_ref[...], w_ref[...]).astype(o_ref.dtype)


def rmsnorm_rows(x, w, *, tm=1024):
    S, D = x.shape
    return pl.pallas_call(
        _rmsnorm_kernel,
        grid=(S // tm,),
        in_specs=[pl.BlockSpec((tm, D), lambda i: (i, 0)), pl.BlockSpec((1, D), lambda i: (0, 0))],
        out_specs=pl.BlockSpec((tm, D), lambda i: (i, 0)),
        out_shape=jax.ShapeDtypeStruct((S, D), F32),
        compiler_params=_params("parallel"),
        name="rmsnorm_rows",
    )(x, w.reshape(1, D))


def _mm_kernel(x_ref, w_ref, o_ref):
    o_ref[...] = _bdot(x_ref[...], w_ref[...]).astype(o_ref.dtype)


def _mm_res_kernel(x_ref, w_ref, r_ref, o_ref):
    o_ref[...] = (r_ref[...] + _bdot(x_ref[...], w_ref[...])).astype(o_ref.dtype)


def matmul(x, w, *, tm, tn, residual=None, out_dtype=F32, name="matmul"):
    S, K = x.shape
    N = w.shape[1]
    in_specs = [pl.BlockSpec((tm, K), lambda i, j: (i, 0)), pl.BlockSpec((K, tn), lambda i, j: (0, j))]
    args = [x, w]
    body = _mm_kernel
    if residual is not None:
        in_specs.append(pl.BlockSpec((tm, tn), lambda i, j: (i, j)))
        args.append(residual)
        body = _mm_res_kernel
    return pl.pallas_call(
        body,
        grid=(S // tm, N // tn),
        in_specs=in_specs,
        out_specs=pl.BlockSpec((tm, tn), lambda i, j: (i, j)),
        out_shape=jax.ShapeDtypeStruct((S, N), out_dtype),
        compiler_params=_params("parallel", "arbitrary"),
        name=name,
    )(*args)


def _ffn_kernel(x_ref, nw_ref, wg_ref, wu_ref, wd_ref, o_ref, h_scr):
    @pl.when(pl.program_id(1) == 0)
    def _():
        x = x_ref[...]
        h_scr[...] = _rms(x, nw_ref[...]).astype(BF16)
        o_ref[...] = x

    h = h_scr[...]
    g = jnp.dot(h, wg_ref[...], preferred_element_type=F32)
    u = jnp.dot(h, wu_ref[...], preferred_element_type=F32)
    a = (g * jax.nn.sigmoid(g) * u).astype(BF16)
    o_ref[...] += jnp.dot(a, wd_ref[...], preferred_element_type=F32)


def ffn_dense(x, nw, wg, wu, wd, *, tm=512, tf=1408):
    S, D = x.shape
    F = wg.shape[1]
    return pl.pallas_call(
        _ffn_kernel,
        grid=(S // tm, F // tf),
        in_specs=[
            pl.BlockSpec((tm, D), lambda i, f: (i, 0)),
            pl.BlockSpec((1, D), lambda i, f: (0, 0)),
            pl.BlockSpec((D, tf), lambda i, f: (0, f)),
            pl.BlockSpec((D, tf), lambda i, f: (0, f)),
            pl.BlockSpec((tf, D), lambda i, f: (f, 0)),
        ],
        out_specs=pl.BlockSpec((tm, D), lambda i, f: (i, 0)),
        out_shape=jax.ShapeDtypeStruct((S, D), F32),
        scratch_shapes=[pltpu.VMEM((tm, D), BF16)],
        compiler_params=_params("parallel", "arbitrary"),
        name="ffn_dense",
    )(x, nw.reshape(1, D), wg, wu, wd)


def _router_kernel(x_ref, nw_ref, wr_ref, o_ref):
    h = _rms(x_ref[...], nw_ref[...])
    logits = jnp.dot(h, wr_ref[...], precision=lax.Precision.HIGHEST, preferred_element_type=F32)
    lane = lax.broadcasted_iota(jnp.int32, logits.shape, 1)
    valid = lane < N_EXPERTS
    lg = jnp.where(valid, logits, NEG_INF)
    e = jnp.exp(lg - jnp.max(lg, axis=-1, keepdims=True))
    p = e / jnp.sum(e, axis=-1, keepdims=True)
    pm = jnp.where(valid, p, -1.0)
    m1 = jnp.max(pm, axis=-1, keepdims=True)
    i1 = jnp.min(jnp.where(pm == m1, lane, 128), axis=-1, keepdims=True)
    pm2 = jnp.where(lane == i1, -1.0, pm)
    m2 = jnp.max(pm2, axis=-1, keepdims=True)
    i2 = jnp.min(jnp.where(pm2 == m2, lane, 128), axis=-1, keepdims=True)
    tot = m1 + m2
    o_ref[...] = jnp.where(lane == i1, m1 / tot, 0.0) + jnp.where(lane == i2, m2 / tot, 0.0)


def router_gates(x, nw, wr_pad, *, tm=1024):
    S, D = x.shape
    return pl.pallas_call(
        _router_kernel,
        grid=(S // tm,),
        in_specs=[
            pl.BlockSpec((tm, D), lambda i: (i, 0)),
            pl.BlockSpec((1, D), lambda i: (0, 0)),
            pl.BlockSpec((D, 128), lambda i: (0, 0)),
        ],
        out_specs=pl.BlockSpec((tm, 128), lambda i: (i, 0)),
        out_shape=jax.ShapeDtypeStruct((S, 128), F32),
        compiler_params=_params("parallel"),
        name="router_gates",
    )(x, nw.reshape(1, D), wr_pad)


def _moe_dense_kernel(x_ref, nw_ref, gt_ref, wg_ref, wu_ref, wd_ref, o_ref, h_scr):
    e = pl.program_id(1)

    @pl.when((e == 0) & (pl.program_id(2) == 0))
    def _():
        x = x_ref[...]
        h_scr[...] = _rms(x, nw_ref[...]).astype(BF16)
        o_ref[...] = x

    h = h_scr[...]
    gates = gt_ref[...]
    lane = lax.broadcasted_iota(jnp.int32, gates.shape, 1)
    ge = jnp.sum(jnp.where(lane == e, gates, 0.0), axis=-1, keepdims=True)
    g = jnp.dot(h, wg_ref[...], preferred_element_type=F32)
    u = jnp.dot(h, wu_ref[...], preferred_element_type=F32)
    a = (g * jax.nn.sigmoid(g) * u * ge).astype(BF16)
    o_ref[...] += jnp.dot(a, wd_ref[...], preferred_element_type=F32)


def moe_dense(x, nw, gates, wg, wu, wd, *, tm=512, tf=1792):
    S, D = x.shape
    E, _, F = wg.shape
    return pl.pallas_call(
        _moe_dense_kernel,
        grid=(S // tm, E, F // tf),
        in_specs=[
            pl.BlockSpec((tm, D), lambda i, e, f: (i, 0)),
            pl.BlockSpec((1, D), lambda i, e, f: (0, 0)),
            pl.BlockSpec((tm, 128), lambda i, e, f: (i, 0)),
            pl.BlockSpec((None, D, tf), lambda i, e, f: (e, 0, f)),
            pl.BlockSpec((None, D, tf), lambda i, e, f: (e, 0, f)),
            pl.BlockSpec((None, tf, D), lambda i, e, f: (e, f, 0)),
        ],
        out_specs=pl.BlockSpec((tm, D), lambda i, e, f: (i, 0)),
        out_shape=jax.ShapeDtypeStruct((S, D), F32),
        scratch_shapes=[pltpu.VMEM((tm, D), BF16)],
        compiler_params=_params("parallel", "arbitrary", "arbitrary"),
        name="moe_dense",
    )(x, nw.reshape(1, D), gates, wg, wu, wd)


def _attn_branch_kernel(*refs, d, rb, bq, n_rows, first, last):
    q_ref, kp_ref, kc_ref, kn_ref, vp_ref, vc_ref, vn_ref = refs[:7]
    pos = 7
    if not first:
        m_in, l_in, a_in = refs[pos:pos + 3]
        pos += 3
    if last:
        o_ref = refs[pos]
        pos += 1
    else:
        m_out, l_out, a_out = refs[pos:pos + 3]
        pos += 3
    kbuf, vbuf = refs[pos:pos + 2]

    kbuf[0:64, :] = kp_ref[...]
    kbuf[64:64 + rb, :] = kc_ref[...]
    kbuf[64 + rb:, :] = kn_ref[...]
    vbuf[0:64, :] = vp_ref[...]
    vbuf[64:64 + rb, :] = vc_ref[...]
    vbuf[64 + rb:, :] = vn_ref[...]

    row0 = pl.program_id(1) * rb
    qi = lax.broadcasted_iota(jnp.int32, (bq, bq + 128), 0)
    kj = lax.broadcasted_iota(jnp.int32, (bq, bq + 128), 1)
    band = (kj >= qi) & (kj <= qi + 128)
    for s in range(rb // bq):
        gk = row0 + (s * bq - 64) + kj
        mask = band & (gk >= 0) & (gk < n_rows)
        rows = slice(s * bq, (s + 1) * bq)
        win = slice(s * bq, s * bq + bq + 128)
        for r in range(d):
            cols = slice(r * HEAD_DIM, (r + 1) * HEAD_DIM)
            sc = lax.dot_general(q_ref[rows, cols], kbuf[win, cols], (((1,), (1,)), ((), ())),
                                 preferred_element_type=F32)
            sc = jnp.where(mask, sc, NEG_INF)
            m_b = jnp.max(sc, axis=-1, keepdims=True)
            p = jnp.exp(sc - m_b)
            l_b = jnp.sum(p, axis=-1, keepdims=True)
            acc_b = jnp.dot(p.astype(BF16), vbuf[win, cols], preferred_element_type=F32)
            if first:
                m_n = jnp.broadcast_to(m_b, (bq, HEAD_DIM))
                l_n = jnp.broadcast_to(l_b, (bq, HEAD_DIM))
                a_n = acc_b
            else:
                m_o = m_in[rows, cols]
                m_n = jnp.maximum(m_o, m_b)
                w_o = jnp.exp(m_o - m_n)
                w_b = jnp.exp(m_b - m_n)
                l_n = l_in[rows, cols] * w_o + l_b * w_b
                a_n = a_in[rows, cols] * w_o + acc_b * w_b
            if last:
                o_ref[rows, cols] = a_n / l_n
            else:
                m_out[rows, cols] = m_n
                l_out[rows, cols] = l_n
                a_out[rows, cols] = a_n


def _attn_branch(q, k, v, state, *, d, rb, bq, first, last):
    H, S, _ = q.shape
    L = S // d
    W = d * HEAD_DIM
    qv, kv, vv = (t.reshape(H, L, W) for t in (q, k, v))
    nb = rb // 64
    last_blk = L // 64 - 1
    cur = pl.BlockSpec((None, rb, W), lambda h, i: (h, i, 0))
    prv = pl.BlockSpec((None, 64, W), lambda h, i: (h, jnp.maximum(i * nb - 1, 0), 0))
    nxt = pl.BlockSpec((None, 64, W), lambda h, i: (h, jnp.minimum((i + 1) * nb, last_blk), 0))
    in_specs = [cur, prv, cur, nxt, prv, cur, nxt]
    args = [qv, kv, kv, kv, vv, vv, vv]
    if not first:
        in_specs += [cur, cur, cur]
        args += [t.reshape(H, L, W) for t in state]
    sds = jax.ShapeDtypeStruct((H, L, W), F32)
    if last:
        out_specs, out_shape = cur, sds
    else:
        out_specs, out_shape = [cur, cur, cur], [sds, sds, sds]
    out = pl.pallas_call(
        functools.partial(_attn_branch_kernel, d=d, rb=rb, bq=bq, n_rows=L, first=first, last=last),
        grid=(H, L // rb),
        in_specs=in_specs,
        out_specs=out_specs,
        out_shape=out_shape,
        scratch_shapes=[pltpu.VMEM((rb + 128, W), BF16), pltpu.VMEM((rb + 128, W), BF16)],
        compiler_params=_params("parallel", "arbitrary"),
        name=f"dilated_attn_d{d}",
    )(*args)
    if last:
        return out.reshape(H, S, HEAD_DIM)
    return tuple(t.reshape(H, S, HEAD_DIM) for t in out)


def dilated_attention(q, k, v):
    st = _attn_branch(q, k, v, None, d=1, rb=1024, bq=256, first=True, last=False)
    st = _attn_branch(q, k, v, st, d=4, rb=512, bq=256, first=False, last=False)
    return _attn_branch(q, k, v, st, d=16, rb=256, bq=256, first=False, last=True)


def _tri_masks(z):
    row = lax.broadcasted_iota(jnp.int32, (CHUNK, CHUNK), 0)
    col = lax.broadcasted_iota(jnp.int32, (CHUNK, CHUNK), 1)
    if z == 0:
        return col <= row, col < row
    return col >= row, col > row


def _neumann_solve(n, x):
    steps = CHUNK.bit_length() - 1
    for i in range(steps):
        x = x + _bdot(n, x)
        if i + 1 < steps:
            n = _bdot(n, n)
    return x


def _rwkv_chunk_kernel(*refs):
    ins = (refs[0:6], refs[6:12])
    y_refs = refs[12:14]
    state = refs[14]

    @pl.when(pl.program_id(0) == 0)
    def _():
        state[...] = jnp.zeros_like(state)

    for z in range(2):
        r_ref, v_ref, a_ref, lw_ref, k_ref, b_ref = ins[z]
        incl, strict = _tri_masks(z)
        tri = jnp.where(incl, 1.0, 0.0).astype(BF16)
        lw = lw_ref[...]
        hi, mid, lo = _split3(lw)
        cum = (jnp.dot(tri, hi, preferred_element_type=F32) + jnp.dot(tri, mid, preferred_element_type=F32)
               + jnp.dot(tri, lo, preferred_element_type=F32))
        tot = jnp.sum(lw, axis=0, keepdims=True)
        e_pos = jnp.exp(cum)
        e_neg = jnp.exp(-cum)
        e_end = jnp.exp(tot - cum)
        k = k_ref[...]
        b = b_ref[...]
        rt = r_ref[...] * e_pos
        at = a_ref[...] * jnp.exp(cum - lw)
        kt = k * e_neg
        bt = b * e_neg
        kh = k * e_end
        bh = b * e_end
        dw = jnp.exp(tot)
        v = v_ref[...]
        for h in range(N_HEADS_B):
            sl = slice(h * HEAD_DIM, (h + 1) * HEAD_DIM)
            at_h, rt_h, v_h = at[:, sl], rt[:, sl], v[:, sl]
            x = _bdot_nt(jnp.concatenate([at_h, rt_h], axis=0), jnp.concatenate([bt[:, sl], kt[:, sl]], axis=0))
            a_ab = jnp.where(strict, x[:CHUNK, :CHUNK], 0.0)
            a_ak = jnp.where(strict, x[:CHUNK, CHUNK:], 0.0)
            a_rb = jnp.where(incl, x[CHUNK:, :CHUNK], 0.0)
            a_rk = jnp.where(incl, x[CHUNK:, CHUNK:], 0.0)
            cy = _bdot(jnp.concatenate([a_ak, a_rk], axis=0), v_h)
            pq = _neumann_solve(a_ab, jnp.concatenate([at_h, cy[:CHUNK]], axis=1))
            s_old = state[z, h]
            u = _bdot_nt(pq[:, :HEAD_DIM], s_old) + pq[:, HEAD_DIM:]
            y = _bdot_nt(rt_h, s_old) + _bdot(a_rb, u) + cy[CHUNK:]
            state[z, h] = s_old * dw[:, sl] + _bdot_tn(u, bh[:, sl]) + _bdot_tn(v_h, kh[:, sl])
            y_refs[z][:, sl] = y


def rwkv7_scan(r, v, a, lw, k, b):
    S, C = r.shape
    n = S // CHUNK
    fwd = pl.BlockSpec((CHUNK, C), lambda c: (c, 0))
    bwd = pl.BlockSpec((CHUNK, C), lambda c: (n - 1 - c, 0))
    y0, y1 = pl.pallas_call(
        _rwkv_chunk_kernel,
        grid=(n,),
        in_specs=[fwd] * 6 + [bwd] * 6,
        out_specs=[fwd, bwd],
        out_shape=[jax.ShapeDtypeStruct((S, C), F32)] * 2,
        scratch_shapes=[pltpu.VMEM((2, N_HEADS_B, HEAD_DIM, HEAD_DIM), F32)],
        compiler_params=_params("arbitrary"),
        name="rwkv7_scan",
    )(r, v, a, lw[0], k[0], b[0], r, v, a, lw[1], k[1], b[1])
    return y0, y1


def _dn_chunk_kernel(*refs):
    ins = (refs[0:6], refs[6:12])
    o_refs = refs[12:14]
    state = refs[14]

    @pl.when(pl.program_id(0) == 0)
    def _():
        state[...] = jnp.zeros_like(state)

    for z in range(2):
        q_ref, k_ref, v_ref, bcol_ref, gcol_ref, grow_ref = ins[z]
        incl, strict = _tri_masks(z)
        tri = jnp.where(incl, 1.0, 0.0).astype(BF16)
        c_hi, c_mid, c_lo = _split3(gcol_ref[...])
        gc_cols = (jnp.dot(tri, c_hi, preferred_element_type=F32) + jnp.dot(tri, c_mid, preferred_element_type=F32)
                   + jnp.dot(tri, c_lo, preferred_element_type=F32))
        nt = (((1,), (1,)), ((), ()))
        r_hi, r_mid, r_lo = _split3(grow_ref[...])
        gc_rows = (lax.dot_general(r_hi, tri, nt, preferred_element_type=F32)
                   + lax.dot_general(r_mid, tri, nt, preferred_element_type=F32)
                   + lax.dot_general(r_lo, tri, nt, preferred_element_type=F32))
        last = CHUNK - 1 if z == 0 else 0
        bcol = bcol_ref[...]
        for h in range(N_HEADS_D):
            idx = z * N_HEADS_D + h
            sl = slice(h * HEAD_DIM_D, (h + 1) * HEAD_DIM_D)
            gc = gc_cols[:, idx:idx + 1]
            diff = gc - gc_rows[idx:idx + 1, :]
            decay = jnp.where(incl, jnp.exp(jnp.where(incl, diff, 0.0)), 0.0)
            beta = bcol[:, idx:idx + 1]
            q_h, k_h, v_h = q_ref[:, sl], k_ref[:, sl], v_ref[:, sl]
            kb = k_h * beta
            kq = _bdot_nt(jnp.concatenate([kb, q_h], axis=0), k_h)
            n_mat = jnp.where(strict, -(kq[:CHUNK] * decay), 0.0)
            attn = kq[CHUNK:] * decay
            e_gc = jnp.exp(gc)
            uk = _neumann_solve(n_mat, jnp.concatenate([v_h * beta, kb * e_gc], axis=1))
            s_old = state[z, h]
            u = uk[:, :HEAD_DIM_D] - _bdot(uk[:, HEAD_DIM_D:], s_old)
            o = _bdot(q_h * e_gc, s_old) + _bdot(attn, u)
            g_last = gc[last:last + 1, :]
            state[z, h] = s_old * jnp.exp(g_last) + _bdot_tn(k_h * jnp.exp(g_last - gc), u)
            o_refs[z][:, sl] = o


def deltanet_scan(q, k, v, beta, g):
    S, C = q.shape
    n = S // CHUNK
    g_rows = g.reshape(n, CHUNK, 2 * N_HEADS_D).transpose(0, 2, 1)
    nz = 2 * N_HEADS_D

    def specs(idx):
        wide = pl.BlockSpec((CHUNK, C), lambda c: (idx(c), 0))
        col = pl.BlockSpec((CHUNK, nz), lambda c: (idx(c), 0))
        row = pl.BlockSpec((None, nz, CHUNK), lambda c: (idx(c), 0, 0))
        return [wide, wide, wide, col, col, row], wide

    in_f, out_f = specs(lambda c: c)
    in_b, out_b = specs(lambda c: n - 1 - c)
    o0, o1 = pl.pallas_call(
        _dn_chunk_kernel,
        grid=(n,),
        in_specs=in_f + in_b,
        out_specs=[out_f, out_b],
        out_shape=[jax.ShapeDtypeStruct((S, C), F32)] * 2,
        scratch_shapes=[pltpu.VMEM((2, N_HEADS_D, HEAD_DIM_D, HEAD_DIM_D), F32)],
        compiler_params=_params("arbitrary"),
        name="deltanet_scan",
    )(q, k, v, beta, g, g_rows, q, k, v, beta, g, g_rows)
    return o0, o1


def _shift_prev(t):
    return jnp.pad(t, ((1, 0), (0, 0)))[:-1]


def _shift_next(t):
    return jnp.pad(t, ((0, 1), (0, 0)))[1:]


def _conv3(t, w):
    return w[0] * _shift_prev(t) + w[1] * t + w[2] * _shift_next(t)


def _l2norm(x, eps=1e-6):
    return x * lax.rsqrt(jnp.sum(x * x, axis=-1, keepdims=True) + eps)


def _head_rms(t, w):
    return t * lax.rsqrt(jnp.mean(t * t, axis=-1, keepdims=True) + NORM_EPS) * w


def _rope(t, positions):
    half = ROPE_DIM // 2
    inv_freq = jnp.power(ROPE_THETA, -jnp.arange(half, dtype=F32) / half)
    ang = positions[:, None, None].astype(F32) * inv_freq
    cos, sin = jnp.cos(ang), jnp.sin(ang)
    x1, x2 = t[..., :half], t[..., half:ROPE_DIM]
    return jnp.concatenate([x1 * cos - x2 * sin, x2 * cos + x1 * sin, t[..., ROPE_DIM:]], axis=-1)


def _block_diag(blocks):
    rows = sum(b.shape[0] for b in blocks)
    cols = sum(b.shape[1] for b in blocks)
    out = jnp.zeros((rows, cols), blocks[0].dtype)
    r = c = 0
    for b in blocks:
        out = lax.dynamic_update_slice(out, b, (r, c))
        r += b.shape[0]
        c += b.shape[1]
    return out


def _even_layer(x, positions, mix_norm, w_in, q_norm, k_norm, shift_mu, lora_mu, w0, w1, w2, a0, a1, a2, g1, g2,
                k_k, k_a, r_k, ln_w, ln_b, w_out, ffn_norm, ffn_gate, ffn_up, ffn_down):
    S = x.shape[0]
    h = rmsnorm_rows(x, mix_norm)
    proj = matmul(h, w_in.astype(BF16), tm=512, tn=1536, name="even_in_proj")
    q, k, v = (proj[:, i * WIDTH_A:(i + 1) * WIDTH_A].reshape(S, N_HEADS_A, HEAD_DIM) for i in range(3))
    q = _rope(_head_rms(q, q_norm), positions) * HEAD_DIM ** -0.5
    k = _rope(_head_rms(k, k_norm), positions)
    q, k, v = (t.transpose(1, 0, 2).astype(BF16) for t in (q, k, v))
    y_a = dilated_attention(q, k, v).transpose(1, 0, 2).reshape(S, WIDTH_A)
    rkv = proj[:, 3 * WIDTH_A:]
    rkv = rkv + shift_mu[0] * (_shift_prev(rkv) - rkv) + shift_mu[1] * (_shift_next(rkv) - rkv)
    r, kk_in, vv = (rkv[:, i * WIDTH_B:(i + 1) * WIDTH_B] for i in range(3))
    hx = h + lora_mu[0] * (_shift_prev(h) - h) + lora_mu[1] * (_shift_next(h) - h)
    lora_in = jnp.concatenate([w1[0], w1[1], a1[0], a1[1], g1], axis=1).astype(BF16)
    l1 = matmul(hx, lora_in, tm=1024, tn=384, name="rwkv_lora_down")
    l1 = jnp.concatenate([jnp.tanh(l1[:, :128]), l1[:, 128:256], jax.nn.sigmoid(l1[:, 256:])], axis=1)
    lora_out = _block_diag([w2[0], w2[1], a2[0], a2[1], g2]).astype(BF16)
    l2 = matmul(l1, lora_out, tm=1024, tn=1280, name="rwkv_lora_up")
    w_pre = jnp.stack([l2[:, :512], l2[:, 512:1024]]) + w0[:, None, :]
    lw = -jnp.exp(-0.5) * jax.nn.sigmoid(w_pre)
    iclr = jax.nn.sigmoid(jnp.stack([l2[:, 1024:1536], l2[:, 1536:2048]]) + a0[:, None, :])
    gate = l2[:, 2048:]
    heads = lambda t: t.reshape(*t.shape[:-1], N_HEADS_B, HEAD_DIM)
    kk = _l2norm(heads(kk_in * k_k)).reshape(S, WIDTH_B)
    k_dir = kk_in[None] * (1.0 + (iclr - 1.0) * k_a)
    b_vec = kk[None] * iclr
    y0, y1 = rwkv7_scan(r, vv, -kk, lw, k_dir, b_vec)
    yf = heads(y0 + y1)
    mu = jnp.mean(yf, axis=-1, keepdims=True)
    var = jnp.mean(jnp.square(yf - mu), axis=-1, keepdims=True)
    yn = ((yf - mu) * lax.rsqrt(var + RWKV_LN_EPS)).reshape(S, WIDTH_B) * ln_w + ln_b
    bonus = (jnp.sum(heads(r) * heads(k_dir[0] + k_dir[1]) * r_k, axis=-1, keepdims=True) * heads(vv)).reshape(S, WIDTH_B)
    y_b = (yn + bonus) * gate
    x = matmul(jnp.concatenate([y_a, y_b], axis=1), w_out.astype(BF16), tm=512, tn=1024, residual=x, name="even_out_proj")
    return ffn_dense(x, ffn_norm, ffn_gate.astype(BF16), ffn_up.astype(BF16), ffn_down.astype(BF16))


def _odd_layer(x, mix_norm, w_in, conv_c, conv_dn, A_log, dt_bias, dn_norm, w_out, ffn_norm, router, moe_gate, moe_up, moe_down):
    S, D = x.shape
    h = rmsnorm_rows(x, mix_norm)
    n_in = w_in.shape[1]
    n_pad = -(-n_in // 128) * 128
    w_in_p = jnp.pad(w_in, ((0, 0), (0, n_pad - n_in))).astype(BF16)
    proj = matmul(h, w_in_p, tm=512, tn=n_pad // 2 if (n_pad // 2) % 128 == 0 else n_pad, name="odd_in_proj")
    o1 = 3 * WIDTH_C
    c_b, c_c, c_x = (proj[:, i * WIDTH_C:(i + 1) * WIDTH_C] for i in range(3))
    dn_qkv = proj[:, o1:o1 + 3 * WIDTH_D]
    dn_z = proj[:, o1 + 3 * WIDTH_D:o1 + 4 * WIDTH_D]
    dn_beta = proj[:, o1 + 4 * WIDTH_D:o1 + 4 * WIDTH_D + 2 * N_HEADS_D]
    dn_alpha = proj[:, o1 + 4 * WIDTH_D + 2 * N_HEADS_D:o1 + 4 * WIDTH_D + 4 * N_HEADS_D]
    y_c = c_b * _conv3(c_c * c_x, conv_c)
    qkv = jax.nn.silu(_conv3(dn_qkv, conv_dn))
    heads = lambda t: t.reshape(S, N_HEADS_D, HEAD_DIM_D)
    q, k, v = (qkv[:, i * WIDTH_D:(i + 1) * WIDTH_D] for i in range(3))
    q = (_l2norm(heads(q)) * HEAD_DIM_D ** -0.5).reshape(S, WIDTH_D)
    k = _l2norm(heads(k)).reshape(S, WIDTH_D)
    beta = jax.nn.sigmoid(dn_beta)
    g = -jnp.exp(A_log.reshape(-1)) * jax.nn.softplus(dn_alpha + dt_bias.reshape(-1))
    o0, o1_ = deltanet_scan(q, k, v, beta, g)
    o = heads(o0 + o1_)
    y_d = (_head_rms(o, dn_norm) * jax.nn.silu(heads(dn_z))).reshape(S, WIDTH_D)
    x = matmul(jnp.concatenate([y_c, y_d], axis=1), w_out.astype(BF16), tm=512, tn=1024, residual=x, name="odd_out_proj")
    wr_pad = jnp.pad(router, ((0, 0), (0, 128 - N_EXPERTS)))
    gates = router_gates(x, ffn_norm, wr_pad)
    return moe_dense(x, ffn_norm, gates, moe_gate.astype(BF16), moe_up.astype(BF16), moe_down.astype(BF16))


def kernel(x, positions, ev_mix_norm, ev_w_in, ev_q_norm, ev_k_norm, ev_shift_mu, ev_lora_mu, ev_w0, ev_w1, ev_w2, ev_a0, ev_a1, ev_a2, ev_g1, ev_g2, ev_k_k, ev_k_a, ev_r_k, ev_ln_w, ev_ln_b, ev_w_out, ev_ffn_norm, ev_ffn_gate, ev_ffn_up, ev_ffn_down, od_mix_norm, od_w_in, od_conv_c, od_conv_dn, od_A_log, od_dt_bias, od_dn_norm, od_w_out, od_ffn_norm, od_router, od_moe_gate, od_moe_up, od_moe_down):
    B, S, D = x.shape
    assert B == 1
    xs = x.reshape(S, D)
    pos = positions.reshape(S)
    n_layers = ev_mix_norm.shape[0] + od_mix_norm.shape[0]
    for layer in range(n_layers):
        i = layer // 2
        if layer % 2 == 0:
            xs = _even_layer(xs, pos, ev_mix_norm[i], ev_w_in[i], ev_q_norm[i], ev_k_norm[i], ev_shift_mu[i], ev_lora_mu[i],
                             ev_w0[i], ev_w1[i], ev_w2[i], ev_a0[i], ev_a1[i], ev_a2[i], ev_g1[i], ev_g2[i], ev_k_k[i],
                             ev_k_a[i], ev_r_k[i], ev_ln_w[i], ev_ln_b[i], ev_w_out[i], ev_ffn_norm[i], ev_ffn_gate[i],
                             ev_ffn_up[i], ev_ffn_down[i])
        else:
            xs = _odd_layer(xs, od_mix_norm[i], od_w_in[i], od_conv_c[i], od_conv_dn[i], od_A_log[i], od_dt_bias[i],
                            od_dn_norm[i], od_w_out[i], od_ffn_norm[i], od_router[i], od_moe_gate[i], od_moe_up[i],
                            od_moe_down[i])
    return xs.reshape(B, S, D)
```

```python
import functools

import jax
import jax.numpy as jnp
from jax import lax
from jax.experimental import pallas as pl
from jax.experimental.pallas import tpu as pltpu

F32 = jnp.float32
BF16 = jnp.bfloat16

HEAD_DIM = 64
N_HEADS_A = 8
WIDTH_A = N_HEADS_A * HEAD_DIM
DILATION_PATTERNS = ((128, 1), (512, 4), (2048, 16))
ROPE_DIM = HEAD_DIM // 4
ROPE_THETA = 500000.0
N_HEADS_B = 8
WIDTH_B = N_HEADS_B * HEAD_DIM
RWKV_LN_EPS = 64e-5
WIDTH_C = 512
N_HEADS_D = 4
HEAD_DIM_D = 128
WIDTH_D = N_HEADS_D * HEAD_DIM_D
CHUNK = 64
N_EXPERTS = 8
NORM_EPS = 1e-6
NEG_INF = -1e30

V7X_VMEM_LIMIT_BYTES = 56 * 1024 * 1024


def _params(*sem):
    return pltpu.CompilerParams(dimension_semantics=sem, vmem_limit_bytes=V7X_VMEM_LIMIT_BYTES)


def _bdot(a, b):
    return jnp.dot(a.astype(BF16), b.astype(BF16), preferred_element_type=F32)


def _bdot_nt(a, b):
    return lax.dot_general(a.astype(BF16), b.astype(BF16), (((1,), (1,)), ((), ())), preferred_element_type=F32)


def _bdot_tn(a, b):
    return lax.dot_general(a.astype(BF16), b.astype(BF16), (((0,), (0,)), ((), ())), preferred_element_type=F32)


def _split3(x):
    hi = x.astype(BF16)
    r1 = x - hi.astype(F32)
    mid = r1.astype(BF16)
    lo = (r1 - mid.astype(F32)).astype(BF16)
    return hi, mid, lo


def _rms(x, w):
    return x * lax.rsqrt(jnp.mean(x * x, axis=-1, keepdims=True) + NORM_EPS) * w


def _rmsnorm_kernel(x_ref, w_ref, o_ref):
    o_ref[...] = _rms(x_ref[...], w_ref[...]).astype(o_ref.dtype)


def rmsnorm_rows(x, w, *, tm=1024):
    S, D = x.shape
    return pl.pallas_call(
        _rmsnorm_kernel,
        grid=(S // tm,),
        in_specs=[pl.BlockSpec((tm, D), lambda i: (i, 0)), pl.BlockSpec((1, D), lambda i: (0, 0))],
        out_specs=pl.BlockSpec((tm, D), lambda i: (i, 0)),
        out_shape=jax.ShapeDtypeStruct((S, D), F32),
        compiler_params=_params("parallel"),
        name="rmsnorm_rows",
    )(x, w.reshape(1, D))


def _mm_kernel(x_ref, w_ref, o_ref):
    o_ref[...] = _bdot(x_ref[...], w_ref[...]).astype(o_ref.dtype)


def _mm_res_kernel(x_ref, w_ref, r_ref, o_ref):
    o_ref[...] = (r_ref[...] + _bdot(x_ref[...], w_ref[...])).astype(o_ref.dtype)


def matmul(x, w, *, tm, tn, residual=None, out_dtype=F32, name="matmul"):
    S, K = x.shape
    N = w.shape[1]
    in_specs = [pl.BlockSpec((tm, K), lambda i, j: (i, 0)), pl.BlockSpec((K, tn), lambda i, j: (0, j))]
    args = [x, w]
    body = _mm_kernel
    if residual is not None:
        in_specs.append(pl.BlockSpec((tm, tn), lambda i, j: (i, j)))
        args.append(residual)
        body = _mm_res_kernel
    return pl.pallas_call(
        body,
        grid=(S // tm, N // tn),
        in_specs=in_specs,
        out_specs=pl.BlockSpec((tm, tn), lambda i, j: (i, j)),
        out_shape=jax.ShapeDtypeStruct((S, N), out_dtype),
        compiler_params=_params("parallel", "arbitrary"),
        name=name,
    )(*args)


def _ffn_kernel(x_ref, nw_ref, wg_ref, wu_ref, wd_ref, o_ref, h_scr):
    @pl.when(pl.program_id(1) == 0)
    def _():
        x = x_ref[...]
        h_scr[...] = _rms(x, nw_ref[...]).astype(BF16)
        o_ref[...] = x

    h = h_scr[...]
    g = jnp.dot(h, wg_ref[...], preferred_element_type=F32)
    u = jnp.dot(h, wu_ref[...], preferred_element_type=F32)
    a = (g * jax.nn.sigmoid(g) * u).astype(BF16)
    o_ref[...] += jnp.dot(a, wd_ref[...], preferred_element_type=F32)


def ffn_dense(x, nw, wg, wu, wd, *, tm=512, tf=1408):
    S, D = x.shape
    F = wg.shape[1]
    return pl.pallas_call(
        _ffn_kernel,
        grid=(S // tm, F // tf),
        in_specs=[
            pl.BlockSpec((tm, D), lambda i, f: (i, 0)),
            pl.BlockSpec((1, D), lambda i, f: (0, 0)),
            pl.BlockSpec((D, tf), lambda i, f: (0, f)),
            pl.BlockSpec((D, tf), lambda i, f: (0, f)),
            pl.BlockSpec((tf, D), lambda i, f: (f, 0)),
        ],
        out_specs=pl.BlockSpec((tm, D), lambda i, f: (i, 0)),
        out_shape=jax.ShapeDtypeStruct((S, D), F32),
        scratch_shapes=[pltpu.VMEM((tm, D), BF16)],
        compiler_params=_params("parallel", "arbitrary"),
        name="ffn_dense",
    )(x, nw.reshape(1, D), wg, wu, wd)


def _router_kernel(x_ref, nw_ref, wr_ref, o_ref):
    h = _rms(x_ref[...], nw_ref[...])
    logits = jnp.dot(h, wr_ref[...], precision=lax.Precision.HIGHEST, preferred_element_type=F32)
    lane = lax.broadcasted_iota(jnp.int32, logits.shape, 1)
    valid = lane < N_EXPERTS
    lg = jnp.where(valid, logits, NEG_INF)
    e = jnp.exp(lg - jnp.max(lg, axis=-1, keepdims=True))
    p = e / jnp.sum(e, axis=-1, keepdims=True)
    pm = jnp.where(valid, p, -1.0)
    m1 = jnp.max(pm, axis=-1, keepdims=True)
    i1 = jnp.min(jnp.where(pm == m1, lane, 128), axis=-1, keepdims=True)
    pm2 = jnp.where(lane == i1, -1.0, pm)
    m2 = jnp.max(pm2, axis=-1, keepdims=True)
    i2 = jnp.min(jnp.where(pm2 == m2, lane, 128), axis=-1, keepdims=True)
    tot = m1 + m2
    o_ref[...] = jnp.where(lane == i1, m1 / tot, 0.0) + jnp.where(lane == i2, m2 / tot, 0.0)


def router_gates(x, nw, wr_pad, *, tm=1024):
    S, D = x.shape
    return pl.pallas_call(
        _router_kernel,
        grid=(S // tm,),
        in_specs=[
            pl.BlockSpec((tm, D), lambda i: (i, 0)),
            pl.BlockSpec((1, D), lambda i: (0, 0)),
            pl.BlockSpec((D, 128), lambda i: (0, 0)),
        ],
        out_specs=pl.BlockSpec((tm, 128), lambda i: (i, 0)),
        out_shape=jax.ShapeDtypeStruct((S, 128), F32),
        compiler_params=_params("parallel"),
        name="router_gates",
    )(x, nw.reshape(1, D), wr_pad)


def _moe_dense_kernel(x_ref, nw_ref, gt_ref, wg_ref, wu_ref, wd_ref, o_ref, h_scr):
    e = pl.program_id(1)

    @pl.when((e == 0) & (pl.program_id(2) == 0))
    def _():
        x = x_ref[...]
        h_scr[...] = _rms(x, nw_ref[...]).astype(BF16)
        o_ref[...] = x

    h = h_scr[...]
    gates = gt_ref[...]
    lane = lax.broadcasted_iota(jnp.int32, gates.shape, 1)
    ge = jnp.sum(jnp.where(lane == e, gates, 0.0), axis=-1, keepdims=True)
    g = jnp.dot(h, wg_ref[...], preferred_element_type=F32)
    u = jnp.dot(h, wu_ref[...], preferred_element_type=F32)
    a = (g * jax.nn.sigmoid(g) * u * ge).astype(BF16)
    o_ref[...] += jnp.dot(a, wd_ref[...], preferred_element_type=F32)


def moe_dense(x, nw, gates, wg, wu, wd, *, tm=512, tf=1792):
    S, D = x.shape
    E, _, F = wg.shape
    return pl.pallas_call(
        _moe_dense_kernel,
        grid=(S // tm, E, F // tf),
        in_specs=[
            pl.BlockSpec((tm, D), lambda i, e, f: (i, 0)),
            pl.BlockSpec((1, D), lambda i, e, f: (0, 0)),
            pl.BlockSpec((tm, 128), lambda i, e, f: (i, 0)),
            pl.BlockSpec((None, D, tf), lambda i, e, f: (e, 0, f)),
            pl.BlockSpec((None, D, tf), lambda i, e, f: (e, 0, f)),
            pl.BlockSpec((None, tf, D), lambda i, e, f: (e, f, 0)),
        ],
        out_specs=pl.BlockSpec((tm, D), lambda i, e, f: (i, 0)),
        out_shape=jax.ShapeDtypeStruct((S, D), F32),
        scratch_shapes=[pltpu.VMEM((tm, D), BF16)],
        compiler_params=_params("parallel", "arbitrary", "arbitrary"),
        name="moe_dense",
    )(x, nw.reshape(1, D), gates, wg, wu, wd)


def _attn_branch_kernel(*refs, d, rb, bq, n_rows, first, last):
    q_ref, kp_ref, kc_ref, kn_ref, vp_ref, vc_ref, vn_ref = refs[:7]
    pos = 7
    if not first:
        m_in, l_in, a_in = refs[pos:pos + 3]
        pos += 3
    if last:
        o_ref = refs[pos]
        pos += 1
    else:
        m_out, l_out, a_out = refs[pos:pos + 3]
        pos += 3
    kbuf, vbuf = refs[pos:pos + 2]

    kbuf[0:64, :] = kp_ref[...]
    kbuf[64:64 + rb, :] = kc_ref[...]
    kbuf[64 + rb:, :] = kn_ref[...]
    vbuf[0:64, :] = vp_ref[...]
    vbuf[64:64 + rb, :] = vc_ref[...]
    vbuf[64 + rb:, :] = vn_ref[...]

    row0 = pl.program_id(1) * rb
    qi = lax.broadcasted_iota(jnp.int32, (bq, bq + 128), 0)
    kj = lax.broadcasted_iota(jnp.int32, (bq, bq + 128), 1)
    band = (kj >= qi) & (kj <= qi + 128)
    for s in range(rb // bq):
        gk = row0 + (s * bq - 64) + kj
        mask = band & (gk >= 0) & (gk < n_rows)
        rows = slice(s * bq, (s + 1) * bq)
        win = slice(s * bq, s * bq + bq + 128)
        for r in range(d):
            cols = slice(r * HEAD_DIM, (r + 1) * HEAD_DIM)
            sc = lax.dot_general(q_ref[rows, cols], kbuf[win, cols], (((1,), (1,)), ((), ())),
                                 preferred_element_type=F32)
            sc = jnp.where(mask, sc, NEG_INF)
            m_b = jnp.max(sc, axis=-1, keepdims=True)
            p = jnp.exp(sc - m_b)
            l_b = jnp.sum(p, axis=-1, keepdims=True)
            acc_b = jnp.dot(p.astype(BF16), vbuf[win, cols], preferred_element_type=F32)
            if first:
                m_n = jnp.broadcast_to(m_b, (bq, HEAD_DIM))
                l_n = jnp.broadcast_to(l_b, (bq, HEAD_DIM))
                a_n = acc_b
            else:
                m_o = m_in[rows, cols]
                m_n = jnp.maximum(m_o, m_b)
                w_o = jnp.exp(m_o - m_n)
                w_b = jnp.exp(m_b - m_n)
                l_n = l_in[rows, cols] * w_o + l_b * w_b
                a_n = a_in[rows, cols] * w_o + acc_b * w_b
            if last:
                o_ref[rows, cols] = a_n / l_n
            else:
                m_out[rows, cols] = m_n
                l_out[rows, cols] = l_n
                a_out[rows, cols] = a_n


def _attn_branch(q, k, v, state, *, d, rb, bq, first, last):
    H, S, _ = q.shape
    L = S // d
    W = d * HEAD_DIM
    qv, kv, vv = (t.reshape(H, L, W) for t in (q, k, v))
    nb = rb // 64
    last_blk = L // 64 - 1
    cur = pl.BlockSpec((None, rb, W), lambda h, i: (h, i, 0))
    prv = pl.BlockSpec((None, 64, W), lambda h, i: (h, jnp.maximum(i * nb - 1, 0), 0))
    nxt = pl.BlockSpec((None, 64, W), lambda h, i: (h, jnp.minimum((i + 1) * nb, last_blk), 0))
    in_specs = [cur, prv, cur, nxt, prv, cur, nxt]
    args = [qv, kv, kv, kv, vv, vv, vv]
    if not first:
        in_specs += [cur, cur, cur]
        args += [t.reshape(H, L, W) for t in state]
    sds = jax.ShapeDtypeStruct((H, L, W), F32)
    if last:
        out_specs, out_shape = cur, sds
    else:
        out_specs, out_shape = [cur, cur, cur], [sds, sds, sds]
    out = pl.pallas_call(
        functools.partial(_attn_branch_kernel, d=d, rb=rb, bq=bq, n_rows=L, first=first, last=last),
        grid=(H, L // rb),
        in_specs=in_specs,
        out_specs=out_specs,
        out_shape=out_shape,
        scratch_shapes=[pltpu.VMEM((rb + 128, W), BF16), pltpu.VMEM((rb + 128, W), BF16)],
        compiler_params=_params("parallel", "arbitrary"),
        name=f"dilated_attn_d{d}",
    )(*args)
    if last:
        return out.reshape(H, S, HEAD_DIM)
    return tuple(t.reshape(H, S, HEAD_DIM) for t in out)


def dilated_attention(q, k, v):
    st = _attn_branch(q, k, v, None, d=1, rb=1024, bq=256, first=True, last=False)
    st = _attn_branch(q, k, v, st, d=4, rb=512, bq=256, first=False, last=False)
    return _attn_branch(q, k, v, st, d=16, rb=256, bq=256, first=False, last=True)


def _tri_masks(z):
    row = lax.broadcasted_iota(jnp.int32, (CHUNK, CHUNK), 0)
    col = lax.broadcasted_iota(jnp.int32, (CHUNK, CHUNK), 1)
    if z == 0:
        return col <= row, col < row
    return col >= row, col > row


def _neumann_solve(ns, xs):
    steps = CHUNK.bit_length() - 1
    for i in range(steps):
        xs = [x + _bdot(n, x) for n, x in zip(ns, xs)]
        if i + 1 < steps:
            ns = [_bdot(n, n) for n in ns]
    return xs


def _rwkv_chunk_kernel(*refs):
    ins = (refs[0:6], refs[6:12])
    y_refs = refs[12:14]
    state = refs[14]

    @pl.when(pl.program_id(0) == 0)
    def _():
        state[...] = jnp.zeros_like(state)

    chains = [(z, h) for z in range(2) for h in range(N_HEADS_B)]
    masks = [_tri_masks(z) for z in range(2)]
    prep = []
    for z in range(2):
        r_ref, v_ref, a_ref, lw_ref, k_ref, b_ref = ins[z]
        tri = jnp.where(masks[z][0], 1.0, 0.0).astype(BF16)
        lw = lw_ref[...]
        hi, mid, lo = _split3(lw)
        cum = (jnp.dot(tri, hi, preferred_element_type=F32) + jnp.dot(tri, mid, preferred_element_type=F32)
               + jnp.dot(tri, lo, preferred_element_type=F32))
        tot = jnp.sum(lw, axis=0, keepdims=True)
        e_neg = jnp.exp(-cum)
        e_end = jnp.exp(tot - cum)
        k = k_ref[...]
        b = b_ref[...]
        prep.append(dict(rt=r_ref[...] * jnp.exp(cum), at=a_ref[...] * jnp.exp(cum - lw), kt=k * e_neg, bt=b * e_neg,
                         kh=k * e_end, bh=b * e_end, dw=jnp.exp(tot), v=v_ref[...]))

    def part(name, z, h):
        return prep[z][name][:, h * HEAD_DIM:(h + 1) * HEAD_DIM]

    xs = [_bdot_nt(jnp.concatenate([part("at", z, h), part("rt", z, h)], axis=0),
                   jnp.concatenate([part("bt", z, h), part("kt", z, h)], axis=0)) for z, h in chains]
    a_ab = [jnp.where(masks[z][1], x[:CHUNK, :CHUNK], 0.0) for (z, h), x in zip(chains, xs)]
    a_rb = [jnp.where(masks[z][0], x[CHUNK:, :CHUNK], 0.0) for (z, h), x in zip(chains, xs)]
    cys = [_bdot(jnp.concatenate([jnp.where(masks[z][1], x[:CHUNK, CHUNK:], 0.0),
                                  jnp.where(masks[z][0], x[CHUNK:, CHUNK:], 0.0)], axis=0), part("v", z, h))
           for (z, h), x in zip(chains, xs)]
    pqs = _neumann_solve(a_ab, [jnp.concatenate([part("at", z, h), cy[:CHUNK]], axis=1)
                                for (z, h), cy in zip(chains, cys)])
    s_old = [state[z, h] for z, h in chains]
    us = [_bdot_nt(pq[:, :HEAD_DIM], s) + pq[:, HEAD_DIM:] for pq, s in zip(pqs, s_old)]
    ys = [_bdot_nt(part("rt", z, h), s) for (z, h), s in zip(chains, s_old)]
    ys = [y + _bdot(arb, u) + cy[CHUNK:] for y, arb, u, cy in zip(ys, a_rb, us, cys)]
    s_new = [s * part("dw", z, h) + _bdot_tn(u, part("bh", z, h)) for (z, h), s, u in zip(chains, s_old, us)]
    s_new = [s + _bdot_tn(part("v", z, h), part("kh", z, h)) for (z, h), s in zip(chains, s_new)]
    for (z, h), s, y in zip(chains, s_new, ys):
        state[z, h] = s
        y_refs[z][:, h * HEAD_DIM:(h + 1) * HEAD_DIM] = y


def rwkv7_scan(r, v, a, lw, k, b):
    S, C = r.shape
    n = S // CHUNK
    fwd = pl.BlockSpec((CHUNK, C), lambda c: (c, 0))
    bwd = pl.BlockSpec((CHUNK, C), lambda c: (n - 1 - c, 0))
    y0, y1 = pl.pallas_call(
        _rwkv_chunk_kernel,
        grid=(n,),
        in_specs=[fwd] * 6 + [bwd] * 6,
        out_specs=[fwd, bwd],
        out_shape=[jax.ShapeDtypeStruct((S, C), F32)] * 2,
        scratch_shapes=[pltpu.VMEM((2, N_HEADS_B, HEAD_DIM, HEAD_DIM), F32)],
        compiler_params=_params("arbitrary"),
        name="rwkv7_scan",
    )(r, v, a, lw[0], k[0], b[0], r, v, a, lw[1], k[1], b[1])
    return y0, y1


def _dn_chunk_kernel(*refs):
    ins = (refs[0:6], refs[6:12])
    o_refs = refs[12:14]
    state = refs[14]

    @pl.when(pl.program_id(0) == 0)
    def _():
        state[...] = jnp.zeros_like(state)

    chains = [(z, h) for z in range(2) for h in range(N_HEADS_D)]
    masks = [_tri_masks(z) for z in range(2)]
    nt = (((1,), (1,)), ((), ()))
    gcs, decays, betas, g_lasts = [], [], [], []
    for z in range(2):
        _, _, _, bcol_ref, gcol_ref, grow_ref = ins[z]
        incl = masks[z][0]
        tri = jnp.where(incl, 1.0, 0.0).astype(BF16)
        c_hi, c_mid, c_lo = _split3(gcol_ref[...])
        gc_cols = (jnp.dot(tri, c_hi, preferred_element_type=F32) + jnp.dot(tri, c_mid, preferred_element_type=F32)
                   + jnp.dot(tri, c_lo, preferred_element_type=F32))
        r_hi, r_mid, r_lo = _split3(grow_ref[...])
        gc_rows = (lax.dot_general(r_hi, tri, nt, preferred_element_type=F32)
                   + lax.dot_general(r_mid, tri, nt, preferred_element_type=F32)
                   + lax.dot_general(r_lo, tri, nt, preferred_element_type=F32))
        last = CHUNK - 1 if z == 0 else 0
        bcol = bcol_ref[...]
        for h in range(N_HEADS_D):
            idx = z * N_HEADS_D + h
            gc = gc_cols[:, idx:idx + 1]
            diff = gc - gc_rows[idx:idx + 1, :]
            gcs.append(gc)
            decays.append(jnp.where(incl, jnp.exp(jnp.where(incl, diff, 0.0)), 0.0))
            betas.append(bcol[:, idx:idx + 1])
            g_lasts.append(gc[last:last + 1, :])

    def part(i, z, h):
        return ins[z][i][:, h * HEAD_DIM_D:(h + 1) * HEAD_DIM_D]

    qs = [part(0, z, h) for z, h in chains]
    ks = [part(1, z, h) for z, h in chains]
    vs = [part(2, z, h) for z, h in chains]
    kbs = [k * beta for k, beta in zip(ks, betas)]
    e_gcs = [jnp.exp(gc) for gc in gcs]
    kqs = [_bdot_nt(jnp.concatenate([kb, q], axis=0), k) for kb, q, k in zip(kbs, qs, ks)]
    n_mats = [jnp.where(masks[z][1], -(kq[:CHUNK] * dc), 0.0) for (z, h), kq, dc in zip(chains, kqs, decays)]
    attns = [kq[CHUNK:] * dc for kq, dc in zip(kqs, decays)]
    uks = _neumann_solve(n_mats, [jnp.concatenate([v * beta, kb * e], axis=1)
                                  for v, beta, kb, e in zip(vs, betas, kbs, e_gcs)])
    s_old = [state[z, h] for z, h in chains]
    us = [uk[:, :HEAD_DIM_D] - _bdot(uk[:, HEAD_DIM_D:], s) for uk, s in zip(uks, s_old)]
    os_ = [_bdot(q * e, s) for q, e, s in zip(qs, e_gcs, s_old)]
    os_ = [o + _bdot(attn, u) for o, attn, u in zip(os_, attns, us)]
    s_new = [s * jnp.exp(gl) + _bdot_tn(k * jnp.exp(gl - gc), u)
             for s, gl, k, gc, u in zip(s_old, g_lasts, ks, gcs, us)]
    for (z, h), s, o in zip(chains, s_new, os_):
        state[z, h] = s
        o_refs[z][:, h * HEAD_DIM_D:(h + 1) * HEAD_DIM_D] = o


def deltanet_scan(q, k, v, beta, g):
    S, C = q.shape
    n = S // CHUNK
    g_rows = g.reshape(n, CHUNK, 2 * N_HEADS_D).transpose(0, 2, 1)
    nz = 2 * N_HEADS_D

    def specs(idx):
        wide = pl.BlockSpec((CHUNK, C), lambda c: (idx(c), 0))
        col = pl.BlockSpec((CHUNK, nz), lambda c: (idx(c), 0))
        row = pl.BlockSpec((None, nz, CHUNK), lambda c: (idx(c), 0, 0))
        return [wide, wide, wide, col, col, row], wide

    in_f, out_f = specs(lambda c: c)
    in_b, out_b = specs(lambda c: n - 1 - c)
    o0, o1 = pl.pallas_call(
        _dn_chunk_kernel,
        grid=(n,),
        in_specs=in_f + in_b,
        out_specs=[out_f, out_b],
        out_shape=[jax.ShapeDtypeStruct((S, C), F32)] * 2,
        scratch_shapes=[pltpu.VMEM((2, N_HEADS_D, HEAD_DIM_D, HEAD_DIM_D), F32)],
        compiler_params=_params("arbitrary"),
        name="deltanet_scan",
    )(q, k, v, beta, g, g_rows, q, k, v, beta, g, g_rows)
    return o0, o1


def _shift_prev(t):
    return jnp.pad(t, ((1, 0), (0, 0)))[:-1]


def _shift_next(t):
    return jnp.pad(t, ((0, 1), (0, 0)))[1:]


def _conv3(t, w):
    return w[0] * _shift_prev(t) + w[1] * t + w[2] * _shift_next(t)


def _l2norm(x, eps=1e-6):
    return x * lax.rsqrt(jnp.sum(x * x, axis=-1, keepdims=True) + eps)


def _head_rms(t, w):
    return t * lax.rsqrt(jnp.mean(t * t, axis=-1, keepdims=True) + NORM_EPS) * w


def _rope(t, positions):
    half = ROPE_DIM // 2
    inv_freq = jnp.power(ROPE_THETA, -jnp.arange(half, dtype=F32) / half)
    ang = positions[:, None, None].astype(F32) * inv_freq
    cos, sin = jnp.cos(ang), jnp.sin(ang)
    x1, x2 = t[..., :half], t[..., half:ROPE_DIM]
    return jnp.concatenate([x1 * cos - x2 * sin, x2 * cos + x1 * sin, t[..., ROPE_DIM:]], axis=-1)


def _block_diag(blocks):
    rows = sum(b.shape[0] for b in blocks)
    cols = sum(b.shape[1] for b in blocks)
    out = jnp.zeros((rows, cols), blocks[0].dtype)
    r = c = 0
    for b in blocks:
        out = lax.dynamic_update_slice(out, b, (r, c))
        r += b.shape[0]
        c += b.shape[1]
    return out


def _even_layer(x, positions, mix_norm, w_in, q_norm, k_norm, shift_mu, lora_mu, w0, w1, w2, a0, a1, a2, g1, g2,
                k_k, k_a, r_k, ln_w, ln_b, w_out, ffn_norm, ffn_gate, ffn_up, ffn_down):
    S = x.shape[0]
    h = rmsnorm_rows(x, mix_norm)
    proj = matmul(h, w_in.astype(BF16), tm=512, tn=1536, name="even_in_proj")
    q, k, v = (proj[:, i * WIDTH_A:(i + 1) * WIDTH_A].reshape(S, N_HEADS_A, HEAD_DIM) for i in range(3))
    q = _rope(_head_rms(q, q_norm), positions) * HEAD_DIM ** -0.5
    k = _rope(_head_rms(k, k_norm), positions)
    q, k, v = (t.transpose(1, 0, 2).astype(BF16) for t in (q, k, v))
    y_a = dilated_attention(q, k, v).transpose(1, 0, 2).reshape(S, WIDTH_A)
    rkv = proj[:, 3 * WIDTH_A:]
    rkv = rkv + shift_mu[0] * (_shift_prev(rkv) - rkv) + shift_mu[1] * (_shift_next(rkv) - rkv)
    r, kk_in, vv = (rkv[:, i * WIDTH_B:(i + 1) * WIDTH_B] for i in range(3))
    hx = h + lora_mu[0] * (_shift_prev(h) - h) + lora_mu[1] * (_shift_next(h) - h)
    lora_in = jnp.concatenate([w1[0], w1[1], a1[0], a1[1], g1], axis=1).astype(BF16)
    l1 = matmul(hx, lora_in, tm=1024, tn=384, name="rwkv_lora_down")
    l1 = jnp.concatenate([jnp.tanh(l1[:, :128]), l1[:, 128:256], jax.nn.sigmoid(l1[:, 256:])], axis=1)
    lora_out = _block_diag([w2[0], w2[1], a2[0], a2[1], g2]).astype(BF16)
    l2 = matmul(l1, lora_out, tm=1024, tn=1280, name="rwkv_lora_up")
    w_pre = jnp.stack([l2[:, :512], l2[:, 512:1024]]) + w0[:, None, :]
    lw = -jnp.exp(-0.5) * jax.nn.sigmoid(w_pre)
    iclr = jax.nn.sigmoid(jnp.stack([l2[:, 1024:1536], l2[:, 1536:2048]]) + a0[:, None, :])
    gate = l2[:, 2048:]
    heads = lambda t: t.reshape(*t.shape[:-1], N_HEADS_B, HEAD_DIM)
    kk = _l2norm(heads(kk_in * k_k)).reshape(S, WIDTH_B)
    k_dir = kk_in[None] * (1.0 + (iclr - 1.0) * k_a)
    b_vec = kk[None] * iclr
    y0, y1 = rwkv7_scan(r, vv, -kk, lw, k_dir, b_vec)
    yf = heads(y0 + y1)
    mu = jnp.mean(yf, axis=-1, keepdims=True)
    var = jnp.mean(jnp.square(yf - mu), axis=-1, keepdims=True)
    yn = ((yf - mu) * lax.rsqrt(var + RWKV_LN_EPS)).reshape(S, WIDTH_B) * ln_w + ln_b
    bonus = (jnp.sum(heads(r) * heads(k_dir[0] + k_dir[1]) * r_k, axis=-1, keepdims=True) * heads(vv)).reshape(S, WIDTH_B)
    y_b = (yn + bonus) * gate
    x = matmul(jnp.concatenate([y_a, y_b], axis=1), w_out.astype(BF16), tm=512, tn=1024, residual=x, name="even_out_proj")
    return ffn_dense(x, ffn_norm, ffn_gate.astype(BF16), ffn_up.astype(BF16), ffn_down.astype(BF16))


def _odd_layer(x, mix_norm, w_in, conv_c, conv_dn, A_log, dt_bias, dn_norm, w_out, ffn_norm, router, moe_gate, moe_up, moe_down):
    S, D = x.shape
    h = rmsnorm_rows(x, mix_norm)
    n_in = w_in.shape[1]
    n_pad = -(-n_in // 128) * 128
    w_in_p = jnp.pad(w_in, ((0, 0), (0, n_pad - n_in))).astype(BF16)
    proj = matmul(h, w_in_p, tm=512, tn=n_pad // 2 if (n_pad // 2) % 128 == 0 else n_pad, name="odd_in_proj")
    o1 = 3 * WIDTH_C
    c_b, c_c, c_x = (proj[:, i * WIDTH_C:(i + 1) * WIDTH_C] for i in range(3))
    dn_qkv = proj[:, o1:o1 + 3 * WIDTH_D]
    dn_z = proj[:, o1 + 3 * WIDTH_D:o1 + 4 * WIDTH_D]
    dn_beta = proj[:, o1 + 4 * WIDTH_D:o1 + 4 * WIDTH_D + 2 * N_HEADS_D]
    dn_alpha = proj[:, o1 + 4 * WIDTH_D + 2 * N_HEADS_D:o1 + 4 * WIDTH_D + 4 * N_HEADS_D]
    y_c = c_b * _conv3(c_c * c_x, conv_c)
    qkv = jax.nn.silu(_conv3(dn_qkv, conv_dn))
    heads = lambda t: t.reshape(S, N_HEADS_D, HEAD_DIM_D)
    q, k, v = (qkv[:, i * WIDTH_D:(i + 1) * WIDTH_D] for i in range(3))
    q = (_l2norm(heads(q)) * HEAD_DIM_D ** -0.5).reshape(S, WIDTH_D)
    k = _l2norm(heads(k)).reshape(S, WIDTH_D)
    beta = jax.nn.sigmoid(dn_beta)
    g = -jnp.exp(A_log.reshape(-1)) * jax.nn.softplus(dn_alpha + dt_bias.reshape(-1))
    o0, o1_ = deltanet_scan(q, k, v, beta, g)
    o = heads(o0 + o1_)
    y_d = (_head_rms(o, dn_norm) * jax.nn.silu(heads(dn_z))).reshape(S, WIDTH_D)
    x = matmul(jnp.concatenate([y_c, y_d], axis=1), w_out.astype(BF16), tm=512, tn=1024, residual=x, name="odd_out_proj")
    wr_pad = jnp.pad(router, ((0, 0), (0, 128 - N_EXPERTS)))
    gates = router_gates(x, ffn_norm, wr_pad)
    return moe_dense(x, ffn_norm, gates, moe_gate.astype(BF16), moe_up.astype(BF16), moe_down.astype(BF16))


def kernel(x, positions, ev_mix_norm, ev_w_in, ev_q_norm, ev_k_norm, ev_shift_mu, ev_lora_mu, ev_w0, ev_w1, ev_w2, ev_a0, ev_a1, ev_a2, ev_g1, ev_g2, ev_k_k, ev_k_a, ev_r_k, ev_ln_w, ev_ln_b, ev_w_out, ev_ffn_norm, ev_ffn_gate, ev_ffn_up, ev_ffn_down, od_mix_norm, od_w_in, od_conv_c, od_conv_dn, od_A_log, od_dt_bias, od_dn_norm, od_w_out, od_ffn_norm, od_router, od_moe_gate, od_moe_up, od_moe_down):
    B, S, D = x.shape
    assert B == 1
    xs = x.reshape(S, D)
    pos = positions.reshape(S)
    n_layers = ev_mix_norm.shape[0] + od_mix_norm.shape[0]
    for layer in range(n_layers):
        i = layer // 2
        if layer % 2 == 0:
            xs = _even_layer(xs, pos, ev_mix_norm[i], ev_w_in[i], ev_q_norm[i], ev_k_norm[i], ev_shift_mu[i], ev_lora_mu[i],
                             ev_w0[i], ev_w1[i], ev_w2[i], ev_a0[i], ev_a1[i], ev_a2[i], ev_g1[i], ev_g2[i], ev_k_k[i],
                             ev_k_a[i], ev_r_k[i], ev_ln_w[i], ev_ln_b[i], ev_w_out[i], ev_ffn_norm[i], ev_ffn_gate[i],
                             ev_ffn_up[i], ev_ffn_down[i])
        else:
            xs = _odd_layer(xs, od_mix_norm[i], od_w_in[i], od_conv_c[i], od_conv_dn[i], od_A_log[i], od_dt_bias[i],
                            od_dn_norm[i], od_w_out[i], od_ffn_norm[i], od_router[i], od_moe_gate[i], od_moe_up[i],
                            od_moe_down[i])
    return xs.reshape(B, S, D)
```

```python
import functools

import jax
import jax.numpy as jnp
from jax import lax
from jax.experimental import pallas as pl
from jax.experimental.pallas import tpu as pltpu

F32 = jnp.float32
BF16 = jnp.bfloat16

HEAD_DIM = 64
N_HEADS_A = 8
WIDTH_A = N_HEADS_A * HEAD_DIM
DILATION_PATTERNS = ((128, 1), (512, 4), (2048, 16))
ROPE_DIM = HEAD_DIM // 4
ROPE_THETA = 500000.0
N_HEADS_B = 8
WIDTH_B = N_HEADS_B * HEAD_DIM
RWKV_LN_EPS = 64e-5
WIDTH_C = 512
N_HEADS_D = 4
HEAD_DIM_D = 128
WIDTH_D = N_HEADS_D * HEAD_DIM_D
CHUNK = 64
N_EXPERTS = 8
TOP_K = 2
ROUTE_LANE_I1, ROUTE_LANE_I2, ROUTE_LANE_G1, ROUTE_LANE_G2 = 8, 9, 10, 11
NORM_EPS = 1e-6
NEG_INF = -1e30

V7X_VMEM_LIMIT_BYTES = 56 * 1024 * 1024


def _params(*sem):
    return pltpu.CompilerParams(dimension_semantics=sem, vmem_limit_bytes=V7X_VMEM_LIMIT_BYTES)


def _bdot(a, b):
    return jnp.dot(a.astype(BF16), b.astype(BF16), preferred_element_type=F32)


def _bdot_nt(a, b):
    return lax.dot_general(a.astype(BF16), b.astype(BF16), (((1,), (1,)), ((), ())), preferred_element_type=F32)


def _bdot_tn(a, b):
    return lax.dot_general(a.astype(BF16), b.astype(BF16), (((0,), (0,)), ((), ())), preferred_element_type=F32)


def _split3(x):
    hi = x.astype(BF16)
    r1 = x - hi.astype(F32)
    mid = r1.astype(BF16)
    lo = (r1 - mid.astype(F32)).astype(BF16)
    return hi, mid, lo


def _rms(x, w):
    return x * lax.rsqrt(jnp.mean(x * x, axis=-1, keepdims=True) + NORM_EPS) * w


def _rmsnorm_kernel(x_ref, w_ref, o_ref):
    o_ref[...] = _rms(x_ref[...], w_ref[...]).astype(o_ref.dtype)


def rmsnorm_rows(x, w, *, tm=1024):
    S, D = x.shape
    return pl.pallas_call(
        _rmsnorm_kernel,
        grid=(S // tm,),
        in_specs=[pl.BlockSpec((tm, D), lambda i: (i, 0)), pl.BlockSpec((1, D), lambda i: (0, 0))],
        out_specs=pl.BlockSpec((tm, D), lambda i: (i, 0)),
        out_shape=jax.ShapeDtypeStruct((S, D), F32),
        compiler_params=_params("parallel"),
        name="rmsnorm_rows",
    )(x, w.reshape(1, D))


def _mm_kernel(x_ref, w_ref, o_ref):
    o_ref[...] = _bdot(x_ref[...], w_ref[...]).astype(o_ref.dtype)


def _mm_res_kernel(x_ref, w_ref, r_ref, o_ref):
    o_ref[...] = (r_ref[...] + _bdot(x_ref[...], w_ref[...])).astype(o_ref.dtype)


def matmul(x, w, *, tm, tn, residual=None, out_dtype=F32, name="matmul"):
    S, K = x.shape
    N = w.shape[1]
    in_specs = [pl.BlockSpec((tm, K), lambda i, j: (i, 0)), pl.BlockSpec((K, tn), lambda i, j: (0, j))]
    args = [x, w]
    body = _mm_kernel
    if residual is not None:
        in_specs.append(pl.BlockSpec((tm, tn), lambda i, j: (i, j)))
        args.append(residual)
        body = _mm_res_kernel
    return pl.pallas_call(
        body,
        grid=(S // tm, N // tn),
        in_specs=in_specs,
        out_specs=pl.BlockSpec((tm, tn), lambda i, j: (i, j)),
        out_shape=jax.ShapeDtypeStruct((S, N), out_dtype),
        compiler_params=_params("parallel", "arbitrary"),
        name=name,
    )(*args)


def _ffn_kernel(x_ref, nw_ref, wg_ref, wu_ref, wd_ref, o_ref, h_scr):
    @pl.when(pl.program_id(1) == 0)
    def _():
        x = x_ref[...]
        h_scr[...] = _rms(x, nw_ref[...]).astype(BF16)
        o_ref[...] = x

    h = h_scr[...]
    g = jnp.dot(h, wg_ref[...], preferred_element_type=F32)
    u = jnp.dot(h, wu_ref[...], preferred_element_type=F32)
    a = (g * jax.nn.sigmoid(g) * u).astype(BF16)
    o_ref[...] += jnp.dot(a, wd_ref[...], preferred_element_type=F32)


def ffn_dense(x, nw, wg, wu, wd, *, tm=512, tf=1408):
    S, D = x.shape
    F = wg.shape[1]
    return pl.pallas_call(
        _ffn_kernel,
        grid=(S // tm, F // tf),
        in_specs=[
            pl.BlockSpec((tm, D), lambda i, f: (i, 0)),
            pl.BlockSpec((1, D), lambda i, f: (0, 0)),
            pl.BlockSpec((D, tf), lambda i, f: (0, f)),
            pl.BlockSpec((D, tf), lambda i, f: (0, f)),
            pl.BlockSpec((tf, D), lambda i, f: (f, 0)),
        ],
        out_specs=pl.BlockSpec((tm, D), lambda i, f: (i, 0)),
        out_shape=jax.ShapeDtypeStruct((S, D), F32),
        scratch_shapes=[pltpu.VMEM((tm, D), BF16)],
        compiler_params=_params("parallel", "arbitrary"),
        name="ffn_dense",
    )(x, nw.reshape(1, D), wg, wu, wd)


def _router_kernel(x_ref, nw_ref, wr_ref, o_ref):
    h = _rms(x_ref[...], nw_ref[...])
    logits = jnp.dot(h, wr_ref[...], precision=lax.Precision.HIGHEST, preferred_element_type=F32)
    lane = lax.broadcasted_iota(jnp.int32, logits.shape, 1)
    valid = lane < N_EXPERTS
    lg = jnp.where(valid, logits, NEG_INF)
    e = jnp.exp(lg - jnp.max(lg, axis=-1, keepdims=True))
    p = e / jnp.sum(e, axis=-1, keepdims=True)
    pm = jnp.where(valid, p, -1.0)
    m1 = jnp.max(pm, axis=-1, keepdims=True)
    i1 = jnp.min(jnp.where(pm == m1, lane, 128), axis=-1, keepdims=True)
    pm2 = jnp.where(lane == i1, -1.0, pm)
    m2 = jnp.max(pm2, axis=-1, keepdims=True)
    i2 = jnp.min(jnp.where(pm2 == m2, lane, 128), axis=-1, keepdims=True)
    tot = m1 + m2
    g1, g2 = m1 / tot, m2 / tot
    out = jnp.where(lane == i1, g1, 0.0) + jnp.where(lane == i2, g2, 0.0)
    out = jnp.where(lane == ROUTE_LANE_I1, i1.astype(F32), out)
    out = jnp.where(lane == ROUTE_LANE_I2, i2.astype(F32), out)
    out = jnp.where(lane == ROUTE_LANE_G1, g1, out)
    o_ref[...] = jnp.where(lane == ROUTE_LANE_G2, g2, out)


def router_gates(x, nw, wr_pad, *, tm=1024):
    S, D = x.shape
    return pl.pallas_call(
        _router_kernel,
        grid=(S // tm,),
        in_specs=[
            pl.BlockSpec((tm, D), lambda i: (i, 0)),
            pl.BlockSpec((1, D), lambda i: (0, 0)),
            pl.BlockSpec((D, 128), lambda i: (0, 0)),
        ],
        out_specs=pl.BlockSpec((tm, 128), lambda i: (i, 0)),
        out_shape=jax.ShapeDtypeStruct((S, 128), F32),
        compiler_params=_params("parallel"),
        name="router_gates",
    )(x, nw.reshape(1, D), wr_pad)


def _row_copy(src_hbm, src_row, dst_vmem, dst_row, sem):
    return pltpu.make_async_copy(src_hbm.at[pl.ds(src_row, 1)], dst_vmem.at[pl.ds(dst_row, 1)], sem)


def _moe_group_kernel(te_ref, tv_ref, tok_ref, x_hbm, nw_ref, wg_ref, wu_ref, wd_ref, o_ref, xbuf, h_scr, sem, *, tm):
    m = pl.program_id(0)
    f = pl.program_id(1)
    valid = tv_ref[m] > 0

    @pl.when(valid & (f == 0))
    def _():
        base = m * tm

        def start(j, carry):
            _row_copy(x_hbm, tok_ref[base + j], xbuf, j, sem).start()
            return carry

        def wait(j, carry):
            _row_copy(x_hbm, 0, xbuf, j, sem).wait()
            return carry

        lax.fori_loop(0, tm, start, 0)
        lax.fori_loop(0, tm, wait, 0)
        h_scr[...] = _rms(xbuf[...], nw_ref[...]).astype(BF16)

    @pl.when(valid)
    def _():
        h = h_scr[...]
        g = jnp.dot(h, wg_ref[...], preferred_element_type=F32)
        u = jnp.dot(h, wu_ref[...], preferred_element_type=F32)
        y = jnp.dot((g * jax.nn.sigmoid(g) * u).astype(BF16), wd_ref[...], preferred_element_type=F32)

        @pl.when(f == 0)
        def _():
            o_ref[...] = y

        @pl.when(f != 0)
        def _():
            o_ref[...] += y

    @pl.when(jnp.logical_not(valid) & (f == 0))
    def _():
        o_ref[...] = jnp.zeros_like(o_ref)


def moe_grouped_ffn(x, nw, tile_expert, tile_valid, src_tok, wg, wu, wd, *, tm, tf=1792):
    S, D = x.shape
    E, _, F = wg.shape
    n_tiles = tile_expert.shape[0]
    nf = F // tf

    def w_in(m, f, te, tv, tok):
        return (te[m], 0, jnp.where(tv[m] > 0, f, nf - 1))

    def w_out(m, f, te, tv, tok):
        return (te[m], jnp.where(tv[m] > 0, f, nf - 1), 0)

    grid_spec = pltpu.PrefetchScalarGridSpec(
        num_scalar_prefetch=3,
        grid=(n_tiles, nf),
        in_specs=[
            pl.BlockSpec(memory_space=pl.ANY),
            pl.BlockSpec((1, D), lambda m, f, te, tv, tok: (0, 0)),
            pl.BlockSpec((None, D, tf), w_in),
            pl.BlockSpec((None, D, tf), w_in),
            pl.BlockSpec((None, tf, D), w_out),
        ],
        out_specs=pl.BlockSpec((tm, D), lambda m, f, te, tv, tok: (m, 0)),
        scratch_shapes=[pltpu.VMEM((tm, D), F32), pltpu.VMEM((tm, D), BF16), pltpu.SemaphoreType.DMA(())],
    )
    return pl.pallas_call(
        functools.partial(_moe_group_kernel, tm=tm),
        grid_spec=grid_spec,
        out_shape=jax.ShapeDtypeStruct((n_tiles * tm, D), F32),
        compiler_params=_params("arbitrary", "arbitrary"),
        name="moe_grouped_ffn",
    )(tile_expert, tile_valid, src_tok, x, nw.reshape(1, D), wg, wu, wd)


def _moe_combine_kernel(pos_ref, x_ref, rt_ref, ys_hbm, o_ref, buf, sem, *, tc, n_tok):
    base = pl.program_id(0) * tc

    def start(j, carry):
        for k in range(TOP_K):
            _row_copy(ys_hbm, pos_ref[k * n_tok + base + j], buf.at[k], j, sem).start()
        return carry

    def wait(j, carry):
        for k in range(TOP_K):
            _row_copy(ys_hbm, 0, buf.at[k], j, sem).wait()
        return carry

    lax.fori_loop(0, tc, start, 0)
    lax.fori_loop(0, tc, wait, 0)
    rt = rt_ref[...]
    o_ref[...] = (x_ref[...] + rt[:, ROUTE_LANE_G1:ROUTE_LANE_G1 + 1] * buf[0]
                  + rt[:, ROUTE_LANE_G2:ROUTE_LANE_G2 + 1] * buf[1])


def moe_combine(x, route, ys, pos, *, tc=256):
    S, D = x.shape
    grid_spec = pltpu.PrefetchScalarGridSpec(
        num_scalar_prefetch=1,
        grid=(S // tc,),
        in_specs=[
            pl.BlockSpec((tc, D), lambda i, pos: (i, 0)),
            pl.BlockSpec((tc, 128), lambda i, pos: (i, 0)),
            pl.BlockSpec(memory_space=pl.ANY),
        ],
        out_specs=pl.BlockSpec((tc, D), lambda i, pos: (i, 0)),
        scratch_shapes=[pltpu.VMEM((TOP_K, tc, D), F32), pltpu.SemaphoreType.DMA(())],
    )
    return pl.pallas_call(
        functools.partial(_moe_combine_kernel, tc=tc, n_tok=S),
        grid_spec=grid_spec,
        out_shape=jax.ShapeDtypeStruct((S, D), F32),
        compiler_params=_params("arbitrary"),
        name="moe_combine",
    )(pos, x, route, ys)


def moe_top2(x, nw, router, wg, wu, wd, *, tm=512):
    S, D = x.shape
    E = wg.shape[0]
    route = router_gates(x, nw, jnp.pad(router, ((0, 0), (0, 128 - E))))
    experts = jnp.concatenate([route[:, ROUTE_LANE_I1], route[:, ROUTE_LANE_I2]]).astype(jnp.int32)
    onehot = (experts[:, None] == jnp.arange(E, dtype=jnp.int32)).astype(jnp.int32)
    csum = jnp.cumsum(onehot, axis=0)
    rank = jnp.sum(onehot * csum, axis=1) - 1
    padded = (csum[-1] + tm - 1) // tm * tm
    ends = jnp.cumsum(padded)
    pos = (jnp.sum(onehot * (ends - padded), axis=1) + rank).astype(jnp.int32)
    n_tiles = TOP_K * S // tm + E
    tokens = jnp.tile(jnp.arange(S, dtype=jnp.int32), TOP_K)
    src_tok = jnp.zeros((n_tiles * tm,), jnp.int32).at[pos].set(tokens)
    tile_start = jnp.arange(n_tiles, dtype=jnp.int32) * tm
    tile_expert = jnp.minimum(jnp.sum(tile_start[:, None] >= ends[None, :], axis=1), E - 1).astype(jnp.int32)
    tile_valid = (tile_start < ends[-1]).astype(jnp.int32)
    ys = moe_grouped_ffn(x, nw, tile_expert, tile_valid, src_tok, wg, wu, wd, tm=tm)
    return moe_combine(x, route, ys, pos)


def _attn_branch_kernel(*refs, d, rb, bq, n_rows, first, last):
    q_ref, kp_ref, kc_ref, kn_ref, vp_ref, vc_ref, vn_ref = refs[:7]
    pos = 7
    if not first:
        m_in, l_in, a_in = refs[pos:pos + 3]
        pos += 3
    if last:
        o_ref = refs[pos]
        pos += 1
    else:
        m_out, l_out, a_out = refs[pos:pos + 3]
        pos += 3
    kbuf, vbuf = refs[pos:pos + 2]

    kbuf[0:64, :] = kp_ref[...]
    kbuf[64:64 + rb, :] = kc_ref[...]
    kbuf[64 + rb:, :] = kn_ref[...]
    vbuf[0:64, :] = vp_ref[...]
    vbuf[64:64 + rb, :] = vc_ref[...]
    vbuf[64 + rb:, :] = vn_ref[...]

    row0 = pl.program_id(1) * rb
    qi = lax.broadcasted_iota(jnp.int32, (bq, bq + 128), 0)
    kj = lax.broadcasted_iota(jnp.int32, (bq, bq + 128), 1)
    band = (kj >= qi) & (kj <= qi + 128)
    for s in range(rb // bq):
        gk = row0 + (s * bq - 64) + kj
        mask = band & (gk >= 0) & (gk < n_rows)
        rows = slice(s * bq, (s + 1) * bq)
        win = slice(s * bq, s * bq + bq + 128)
        for r in range(d):
            cols = slice(r * HEAD_DIM, (r + 1) * HEAD_DIM)
            sc = lax.dot_general(q_ref[rows, cols], kbuf[win, cols], (((1,), (1,)), ((), ())),
                                 preferred_element_type=F32)
            sc = jnp.where(mask, sc, NEG_INF)
            m_b = jnp.max(sc, axis=-1, keepdims=True)
            p = jnp.exp(sc - m_b)
            l_b = jnp.sum(p, axis=-1, keepdims=True)
            acc_b = jnp.dot(p.astype(BF16), vbuf[win, cols], preferred_element_type=F32)
            if first:
                m_n = jnp.broadcast_to(m_b, (bq, HEAD_DIM))
                l_n = jnp.broadcast_to(l_b, (bq, HEAD_DIM))
                a_n = acc_b
            else:
                m_o = m_in[rows, cols]
                m_n = jnp.maximum(m_o, m_b)
                w_o = jnp.exp(m_o - m_n)
                w_b = jnp.exp(m_b - m_n)
                l_n = l_in[rows, cols] * w_o + l_b * w_b
                a_n = a_in[rows, cols] * w_o + acc_b * w_b
            if last:
                o_ref[rows, cols] = a_n / l_n
            else:
                m_out[rows, cols] = m_n
                l_out[rows, cols] = l_n
                a_out[rows, cols] = a_n


def _attn_branch(q, k, v, state, *, d, rb, bq, first, last):
    H, S, _ = q.shape
    L = S // d
    W = d * HEAD_DIM
    qv, kv, vv = (t.reshape(H, L, W) for t in (q, k, v))
    nb = rb // 64
    last_blk = L // 64 - 1
    cur = pl.BlockSpec((None, rb, W), lambda h, i: (h, i, 0))
    prv = pl.BlockSpec((None, 64, W), lambda h, i: (h, jnp.maximum(i * nb - 1, 0), 0))
    nxt = pl.BlockSpec((None, 64, W), lambda h, i: (h, jnp.minimum((i + 1) * nb, last_blk), 0))
    in_specs = [cur, prv, cur, nxt, prv, cur, nxt]
    args = [qv, kv, kv, kv, vv, vv, vv]
    if not first:
        in_specs += [cur, cur, cur]
        args += [t.reshape(H, L, W) for t in state]
    sds = jax.ShapeDtypeStruct((H, L, W), F32)
    if last:
        out_specs, out_shape = cur, sds
    else:
        out_specs, out_shape = [cur, cur, cur], [sds, sds, sds]
    out = pl.pallas_call(
        functools.partial(_attn_branch_kernel, d=d, rb=rb, bq=bq, n_rows=L, first=first, last=last),
        grid=(H, L // rb),
        in_specs=in_specs,
        out_specs=out_specs,
        out_shape=out_shape,
        scratch_shapes=[pltpu.VMEM((rb + 128, W), BF16), pltpu.VMEM((rb + 128, W), BF16)],
        compiler_params=_params("parallel", "arbitrary"),
        name=f"dilated_attn_d{d}",
    )(*args)
    if last:
        return out.reshape(H, S, HEAD_DIM)
    return tuple(t.reshape(H, S, HEAD_DIM) for t in out)


def dilated_attention(q, k, v):
    st = _attn_branch(q, k, v, None, d=1, rb=1024, bq=256, first=True, last=False)
    st = _attn_branch(q, k, v, st, d=4, rb=512, bq=256, first=False, last=False)
    return _attn_branch(q, k, v, st, d=16, rb=256, bq=256, first=False, last=True)


def _tri_masks(z):
    row = lax.broadcasted_iota(jnp.int32, (CHUNK, CHUNK), 0)
    col = lax.broadcasted_iota(jnp.int32, (CHUNK, CHUNK), 1)
    if z == 0:
        return col <= row, col < row
    return col >= row, col > row


def _neumann_solve(ns, xs):
    steps = CHUNK.bit_length() - 1
    for i in range(steps):
        xs = [x + _bdot(n, x) for n, x in zip(ns, xs)]
        if i + 1 < steps:
            ns = [_bdot(n, n) for n in ns]
    return xs


def _rwkv_chunk_kernel(*refs):
    ins = (refs[0:6], refs[6:12])
    y_refs = refs[12:14]
    state = refs[14]

    @pl.when(pl.program_id(0) == 0)
    def _():
        state[...] = jnp.zeros_like(state)

    chains = [(z, h) for z in range(2) for h in range(N_HEADS_B)]
    masks = [_tri_masks(z) for z in range(2)]
    prep = []
    for z in range(2):
        r_ref, v_ref, a_ref, lw_ref, k_ref, b_ref = ins[z]
        tri = jnp.where(masks[z][0], 1.0, 0.0).astype(BF16)
        lw = lw_ref[...]
        hi, mid, lo = _split3(lw)
        cum = (jnp.dot(tri, hi, preferred_element_type=F32) + jnp.dot(tri, mid, preferred_element_type=F32)
               + jnp.dot(tri, lo, preferred_element_type=F32))
        tot = jnp.sum(lw, axis=0, keepdims=True)
        e_neg = jnp.exp(-cum)
        e_end = jnp.exp(tot - cum)
        k = k_ref[...]
        b = b_ref[...]
        prep.append(dict(rt=r_ref[...] * jnp.exp(cum), at=a_ref[...] * jnp.exp(cum - lw), kt=k * e_neg, bt=b * e_neg,
                         kh=k * e_end, bh=b * e_end, dw=jnp.exp(tot), v=v_ref[...]))

    def part(name, z, h):
        return prep[z][name][:, h * HEAD_DIM:(h + 1) * HEAD_DIM]

    xs = [_bdot_nt(jnp.concatenate([part("at", z, h), part("rt", z, h)], axis=0),
                   jnp.concatenate([part("bt", z, h), part("kt", z, h)], axis=0)) for z, h in chains]
    a_ab = [jnp.where(masks[z][1], x[:CHUNK, :CHUNK], 0.0) for (z, h), x in zip(chains, xs)]
    a_rb = [jnp.where(masks[z][0], x[CHUNK:, :CHUNK], 0.0) for (z, h), x in zip(chains, xs)]
    cys = [_bdot(jnp.concatenate([jnp.where(masks[z][1], x[:CHUNK, CHUNK:], 0.0),
                                  jnp.where(masks[z][0], x[CHUNK:, CHUNK:], 0.0)], axis=0), part("v", z, h))
           for (z, h), x in zip(chains, xs)]
    pqs = _neumann_solve(a_ab, [jnp.concatenate([part("at", z, h), cy[:CHUNK]], axis=1)
                                for (z, h), cy in zip(chains, cys)])
    s_old = [state[z, h] for z, h in chains]
    us = [_bdot_nt(pq[:, :HEAD_DIM], s) + pq[:, HEAD_DIM:] for pq, s in zip(pqs, s_old)]
    ys = [_bdot_nt(part("rt", z, h), s) for (z, h), s in zip(chains, s_old)]
    ys = [y + _bdot(arb, u) + cy[CHUNK:] for y, arb, u, cy in zip(ys, a_rb, us, cys)]
    s_new = [s * part("dw", z, h) + _bdot_tn(u, part("bh", z, h)) for (z, h), s, u in zip(chains, s_old, us)]
    s_new = [s + _bdot_tn(part("v", z, h), part("kh", z, h)) for (z, h), s in zip(chains, s_new)]
    for (z, h), s, y in zip(chains, s_new, ys):
        state[z, h] = s
        y_refs[z][:, h * HEAD_DIM:(h + 1) * HEAD_DIM] = y


def rwkv7_scan(r, v, a, lw, k, b):
    S, C = r.shape
    n = S // CHUNK
    fwd = pl.BlockSpec((CHUNK, C), lambda c: (c, 0))
    bwd = pl.BlockSpec((CHUNK, C), lambda c: (n - 1 - c, 0))
    y0, y1 = pl.pallas_call(
        _rwkv_chunk_kernel,
        grid=(n,),
        in_specs=[fwd] * 6 + [bwd] * 6,
        out_specs=[fwd, bwd],
        out_shape=[jax.ShapeDtypeStruct((S, C), F32)] * 2,
        scratch_shapes=[pltpu.VMEM((2, N_HEADS_B, HEAD_DIM, HEAD_DIM), F32)],
        compiler_params=_params("arbitrary"),
        name="rwkv7_scan",
    )(r, v, a, lw[0], k[0], b[0], r, v, a, lw[1], k[1], b[1])
    return y0, y1


def _dn_chunk_kernel(*refs):
    ins = (refs[0:6], refs[6:12])
    o_refs = refs[12:14]
    state = refs[14]

    @pl.when(pl.program_id(0) == 0)
    def _():
        state[...] = jnp.zeros_like(state)

    chains = [(z, h) for z in range(2) for h in range(N_HEADS_D)]
    masks = [_tri_masks(z) for z in range(2)]
    nt = (((1,), (1,)), ((), ()))
    gcs, decays, betas, g_lasts = [], [], [], []
    for z in range(2):
        _, _, _, bcol_ref, gcol_ref, grow_ref = ins[z]
        incl = masks[z][0]
        tri = jnp.where(incl, 1.0, 0.0).astype(BF16)
        c_hi, c_mid, c_lo = _split3(gcol_ref[...])
        gc_cols = (jnp.dot(tri, c_hi, preferred_element_type=F32) + jnp.dot(tri, c_mid, preferred_element_type=F32)
                   + jnp.dot(tri, c_lo, preferred_element_type=F32))
        r_hi, r_mid, r_lo = _split3(grow_ref[...])
        gc_rows = (lax.dot_general(r_hi, tri, nt, preferred_element_type=F32)
                   + lax.dot_general(r_mid, tri, nt, preferred_element_type=F32)
                   + lax.dot_general(r_lo, tri, nt, preferred_element_type=F32))
        last = CHUNK - 1 if z == 0 else 0
        bcol = bcol_ref[...]
        for h in range(N_HEADS_D):
            idx = z * N_HEADS_D + h
            gc = gc_cols[:, idx:idx + 1]
            diff = gc - gc_rows[idx:idx + 1, :]
            gcs.append(gc)
            decays.append(jnp.where(incl, jnp.exp(jnp.where(incl, diff, 0.0)), 0.0))
            betas.append(bcol[:, idx:idx + 1])
            g_lasts.append(gc[last:last + 1, :])

    def part(i, z, h):
        return ins[z][i][:, h * HEAD_DIM_D:(h + 1) * HEAD_DIM_D]

    qs = [part(0, z, h) for z, h in chains]
    ks = [part(1, z, h) for z, h in chains]
    vs = [part(2, z, h) for z, h in chains]
    kbs = [k * beta for k, beta in zip(ks, betas)]
    e_gcs = [jnp.exp(gc) for gc in gcs]
    kqs = [_bdot_nt(jnp.concatenate([kb, q], axis=0), k) for kb, q, k in zip(kbs, qs, ks)]
    n_mats = [jnp.where(masks[z][1], -(kq[:CHUNK] * dc), 0.0) for (z, h), kq, dc in zip(chains, kqs, decays)]
    attns = [kq[CHUNK:] * dc for kq, dc in zip(kqs, decays)]
    uks = _neumann_solve(n_mats, [jnp.concatenate([v * beta, kb * e], axis=1)
                                  for v, beta, kb, e in zip(vs, betas, kbs, e_gcs)])
    s_old = [state[z, h] for z, h in chains]
    us = [uk[:, :HEAD_DIM_D] - _bdot(uk[:, HEAD_DIM_D:], s) for uk, s in zip(uks, s_old)]
    os_ = [_bdot(q * e, s) for q, e, s in zip(qs, e_gcs, s_old)]
    os_ = [o + _bdot(attn, u) for o, attn, u in zip(os_, attns, us)]
    s_new = [s * jnp.exp(gl) + _bdot_tn(k * jnp.exp(gl - gc), u)
             for s, gl, k, gc, u in zip(s_old, g_lasts, ks, gcs, us)]
    for (z, h), s, o in zip(chains, s_new, os_):
        state[z, h] = s
        o_refs[z][:, h * HEAD_DIM_D:(h + 1) * HEAD_DIM_D] = o


def deltanet_scan(q, k, v, beta, g):
    S, C = q.shape
    n = S // CHUNK
    g_rows = g.reshape(n, CHUNK, 2 * N_HEADS_D).transpose(0, 2, 1)
    nz = 2 * N_HEADS_D

    def specs(idx):
        wide = pl.BlockSpec((CHUNK, C), lambda c: (idx(c), 0))
        col = pl.BlockSpec((CHUNK, nz), lambda c: (idx(c), 0))
        row = pl.BlockSpec((None, nz, CHUNK), lambda c: (idx(c), 0, 0))
        return [wide, wide, wide, col, col, row], wide

    in_f, out_f = specs(lambda c: c)
    in_b, out_b = specs(lambda c: n - 1 - c)
    o0, o1 = pl.pallas_call(
        _dn_chunk_kernel,
        grid=(n,),
        in_specs=in_f + in_b,
        out_specs=[out_f, out_b],
        out_shape=[jax.ShapeDtypeStruct((S, C), F32)] * 2,
        scratch_shapes=[pltpu.VMEM((2, N_HEADS_D, HEAD_DIM_D, HEAD_DIM_D), F32)],
        compiler_params=_params("arbitrary"),
        name="deltanet_scan",
    )(q, k, v, beta, g, g_rows, q, k, v, beta, g, g_rows)
    return o0, o1


def _shift_prev(t):
    return jnp.pad(t, ((1, 0), (0, 0)))[:-1]


def _shift_next(t):
    return jnp.pad(t, ((0, 1), (0, 0)))[1:]


def _conv3(t, w):
    return w[0] * _shift_prev(t) + w[1] * t + w[2] * _shift_next(t)


def _l2norm(x, eps=1e-6):
    return x * lax.rsqrt(jnp.sum(x * x, axis=-1, keepdims=True) + eps)


def _head_rms(t, w):
    return t * lax.rsqrt(jnp.mean(t * t, axis=-1, keepdims=True) + NORM_EPS) * w


def _rope(t, positions):
    half = ROPE_DIM // 2
    inv_freq = jnp.power(ROPE_THETA, -jnp.arange(half, dtype=F32) / half)
    ang = positions[:, None, None].astype(F32) * inv_freq
    cos, sin = jnp.cos(ang), jnp.sin(ang)
    x1, x2 = t[..., :half], t[..., half:ROPE_DIM]
    return jnp.concatenate([x1 * cos - x2 * sin, x2 * cos + x1 * sin, t[..., ROPE_DIM:]], axis=-1)


def _block_diag(blocks):
    rows = sum(b.shape[0] for b in blocks)
    cols = sum(b.shape[1] for b in blocks)
    out = jnp.zeros((rows, cols), blocks[0].dtype)
    r = c = 0
    for b in blocks:
        out = lax.dynamic_update_slice(out, b, (r, c))
        r += b.shape[0]
        c += b.shape[1]
    return out


def _even_layer(x, positions, mix_norm, w_in, q_norm, k_norm, shift_mu, lora_mu, w0, w1, w2, a0, a1, a2, g1, g2,
                k_k, k_a, r_k, ln_w, ln_b, w_out, ffn_norm, ffn_gate, ffn_up, ffn_down):
    S = x.shape[0]
    h = rmsnorm_rows(x, mix_norm)
    proj = matmul(h, w_in.astype(BF16), tm=512, tn=1536, name="even_in_proj")
    q, k, v = (proj[:, i * WIDTH_A:(i + 1) * WIDTH_A].reshape(S, N_HEADS_A, HEAD_DIM) for i in range(3))
    q = _rope(_head_rms(q, q_norm), positions) * HEAD_DIM ** -0.5
    k = _rope(_head_rms(k, k_norm), positions)
    q, k, v = (t.transpose(1, 0, 2).astype(BF16) for t in (q, k, v))
    y_a = dilated_attention(q, k, v).transpose(1, 0, 2).reshape(S, WIDTH_A)
    rkv = proj[:, 3 * WIDTH_A:]
    rkv = rkv + shift_mu[0] * (_shift_prev(rkv) - rkv) + shift_mu[1] * (_shift_next(rkv) - rkv)
    r, kk_in, vv = (rkv[:, i * WIDTH_B:(i + 1) * WIDTH_B] for i in range(3))
    hx = h + lora_mu[0] * (_shift_prev(h) - h) + lora_mu[1] * (_shift_next(h) - h)
    lora_in = jnp.concatenate([w1[0], w1[1], a1[0], a1[1], g1], axis=1).astype(BF16)
    l1 = matmul(hx, lora_in, tm=1024, tn=384, name="rwkv_lora_down")
    l1 = jnp.concatenate([jnp.tanh(l1[:, :128]), l1[:, 128:256], jax.nn.sigmoid(l1[:, 256:])], axis=1)
    lora_out = _block_diag([w2[0], w2[1], a2[0], a2[1], g2]).astype(BF16)
    l2 = matmul(l1, lora_out, tm=1024, tn=1280, name="rwkv_lora_up")
    w_pre = jnp.stack([l2[:, :512], l2[:, 512:1024]]) + w0[:, None, :]
    lw = -jnp.exp(-0.5) * jax.nn.sigmoid(w_pre)
    iclr = jax.nn.sigmoid(jnp.stack([l2[:, 1024:1536], l2[:, 1536:2048]]) + a0[:, None, :])
    gate = l2[:, 2048:]
    heads = lambda t: t.reshape(*t.shape[:-1], N_HEADS_B, HEAD_DIM)
    kk = _l2norm(heads(kk_in * k_k)).reshape(S, WIDTH_B)
    k_dir = kk_in[None] * (1.0 + (iclr - 1.0) * k_a)
    b_vec = kk[None] * iclr
    y0, y1 = rwkv7_scan(r, vv, -kk, lw, k_dir, b_vec)
    yf = heads(y0 + y1)
    mu = jnp.mean(yf, axis=-1, keepdims=True)
    var = jnp.mean(jnp.square(yf - mu), axis=-1, keepdims=True)
    yn = ((yf - mu) * lax.rsqrt(var + RWKV_LN_EPS)).reshape(S, WIDTH_B) * ln_w + ln_b
    bonus = (jnp.sum(heads(r) * heads(k_dir[0] + k_dir[1]) * r_k, axis=-1, keepdims=True) * heads(vv)).reshape(S, WIDTH_B)
    y_b = (yn + bonus) * gate
    x = matmul(jnp.concatenate([y_a, y_b], axis=1), w_out.astype(BF16), tm=512, tn=1024, residual=x, name="even_out_proj")
    return ffn_dense(x, ffn_norm, ffn_gate.astype(BF16), ffn_up.astype(BF16), ffn_down.astype(BF16))


def _odd_layer(x, mix_norm, w_in, conv_c, conv_dn, A_log, dt_bias, dn_norm, w_out, ffn_norm, router, moe_gate, moe_up, moe_down):
    S, D = x.shape
    h = rmsnorm_rows(x, mix_norm)
    n_in = w_in.shape[1]
    n_pad = -(-n_in // 128) * 128
    w_in_p = jnp.pad(w_in, ((0, 0), (0, n_pad - n_in))).astype(BF16)
    proj = matmul(h, w_in_p, tm=512, tn=n_pad // 2 if (n_pad // 2) % 128 == 0 else n_pad, name="odd_in_proj")
    o1 = 3 * WIDTH_C
    c_b, c_c, c_x = (proj[:, i * WIDTH_C:(i + 1) * WIDTH_C] for i in range(3))
    dn_qkv = proj[:, o1:o1 + 3 * WIDTH_D]
    dn_z = proj[:, o1 + 3 * WIDTH_D:o1 + 4 * WIDTH_D]
    dn_beta = proj[:, o1 + 4 * WIDTH_D:o1 + 4 * WIDTH_D + 2 * N_HEADS_D]
    dn_alpha = proj[:, o1 + 4 * WIDTH_D + 2 * N_HEADS_D:o1 + 4 * WIDTH_D + 4 * N_HEADS_D]
    y_c = c_b * _conv3(c_c * c_x, conv_c)
    qkv = jax.nn.silu(_conv3(dn_qkv, conv_dn))
    heads = lambda t: t.reshape(S, N_HEADS_D, HEAD_DIM_D)
    q, k, v = (qkv[:, i * WIDTH_D:(i + 1) * WIDTH_D] for i in range(3))
    q = (_l2norm(heads(q)) * HEAD_DIM_D ** -0.5).reshape(S, WIDTH_D)
    k = _l2norm(heads(k)).reshape(S, WIDTH_D)
    beta = jax.nn.sigmoid(dn_beta)
    g = -jnp.exp(A_log.reshape(-1)) * jax.nn.softplus(dn_alpha + dt_bias.reshape(-1))
    o0, o1_ = deltanet_scan(q, k, v, beta, g)
    o = heads(o0 + o1_)
    y_d = (_head_rms(o, dn_norm) * jax.nn.silu(heads(dn_z))).reshape(S, WIDTH_D)
    x = matmul(jnp.concatenate([y_c, y_d], axis=1), w_out.astype(BF16), tm=512, tn=1024, residual=x, name="odd_out_proj")
    return moe_top2(x, ffn_norm, router, moe_gate.astype(BF16), moe_up.astype(BF16), moe_down.astype(BF16))


def kernel(x, positions, ev_mix_norm, ev_w_in, ev_q_norm, ev_k_norm, ev_shift_mu, ev_lora_mu, ev_w0, ev_w1, ev_w2, ev_a0, ev_a1, ev_a2, ev_g1, ev_g2, ev_k_k, ev_k_a, ev_r_k, ev_ln_w, ev_ln_b, ev_w_out, ev_ffn_norm, ev_ffn_gate, ev_ffn_up, ev_ffn_down, od_mix_norm, od_w_in, od_conv_c, od_conv_dn, od_A_log, od_dt_bias, od_dn_norm, od_w_out, od_ffn_norm, od_router, od_moe_gate, od_moe_up, od_moe_down):
    B, S, D = x.shape
    assert B == 1
    xs = x.reshape(S, D)
    pos = positions.reshape(S)
    n_layers = ev_mix_norm.shape[0] + od_mix_norm.shape[0]
    for layer in range(n_layers):
        i = layer // 2
        if layer % 2 == 0:
            xs = _even_layer(xs, pos, ev_mix_norm[i], ev_w_in[i], ev_q_norm[i], ev_k_norm[i], ev_shift_mu[i], ev_lora_mu[i],
                             ev_w0[i], ev_w1[i], ev_w2[i], ev_a0[i], ev_a1[i], ev_a2[i], ev_g1[i], ev_g2[i], ev_k_k[i],
                             ev_k_a[i], ev_r_k[i], ev_ln_w[i], ev_ln_b[i], ev_w_out[i], ev_ffn_norm[i], ev_ffn_gate[i],
                             ev_ffn_up[i], ev_ffn_down[i])
        else:
            xs = _odd_layer(xs, od_mix_norm[i], od_w_in[i], od_conv_c[i], od_conv_dn[i], od_A_log[i], od_dt_bias[i],
                            od_dn_norm[i], od_w_out[i], od_ffn_norm[i], od_router[i], od_moe_gate[i], od_moe_up[i],
                            od_moe_down[i])
    return xs.reshape(B, S, D)
```

```python
import functools

import jax
import jax.numpy as jnp
from jax import lax
from jax.experimental import pallas as pl
from jax.experimental.pallas import tpu as pltpu

F32 = jnp.float32
BF16 = jnp.bfloat16

HEAD_DIM = 64
N_HEADS_A = 8
WIDTH_A = N_HEADS_A * HEAD_DIM
DILATION_PATTERNS = ((128, 1), (512, 4), (2048, 16))
ROPE_DIM = HEAD_DIM // 4
ROPE_THETA = 500000.0
N_HEADS_B = 8
WIDTH_B = N_HEADS_B * HEAD_DIM
RWKV_LN_EPS = 64e-5
WIDTH_C = 512
N_HEADS_D = 4
HEAD_DIM_D = 128
WIDTH_D = N_HEADS_D * HEAD_DIM_D
CHUNK = 64
N_EXPERTS = 8
TOP_K = 2
ROUTE_LANE_I1, ROUTE_LANE_I2, ROUTE_LANE_G1, ROUTE_LANE_G2 = 8, 9, 10, 11
NORM_EPS = 1e-6
NEG_INF = -1e30

V7X_VMEM_LIMIT_BYTES = 56 * 1024 * 1024


def _params(*sem):
    return pltpu.CompilerParams(dimension_semantics=sem, vmem_limit_bytes=V7X_VMEM_LIMIT_BYTES)


def _bdot(a, b):
    return jnp.dot(a.astype(BF16), b.astype(BF16), preferred_element_type=F32)


def _bdot_nt(a, b):
    return lax.dot_general(a.astype(BF16), b.astype(BF16), (((1,), (1,)), ((), ())), preferred_element_type=F32)


def _bdot_tn(a, b):
    return lax.dot_general(a.astype(BF16), b.astype(BF16), (((0,), (0,)), ((), ())), preferred_element_type=F32)


def _split3(x):
    hi = x.astype(BF16)
    r1 = x - hi.astype(F32)
    mid = r1.astype(BF16)
    lo = (r1 - mid.astype(F32)).astype(BF16)
    return hi, mid, lo


def _rms(x, w):
    return x * lax.rsqrt(jnp.mean(x * x, axis=-1, keepdims=True) + NORM_EPS) * w


HALO = 8


def _group_sum(x, ones_bd):
    hi = x.astype(BF16)
    lo = (x - hi.astype(F32)).astype(BF16)
    return jnp.dot(hi, ones_bd, preferred_element_type=F32) + jnp.dot(lo, ones_bd, preferred_element_type=F32)


def _silu(x):
    return x * jax.nn.sigmoid(x)


def _halo_specs(tm, n_rows, width):
    per = tm // HALO
    cur = pl.BlockSpec((tm, width), lambda i: (i, 0))
    prv = pl.BlockSpec((HALO, width), lambda i: (jnp.maximum(i * per - 1, 0), 0))
    nxt = pl.BlockSpec((HALO, width), lambda i: (jnp.minimum((i + 1) * per, n_rows // HALO - 1), 0))
    return cur, prv, nxt


def _make_shifts(tm, n_rows):
    row = pl.program_id(0) * tm + lax.broadcasted_iota(jnp.int32, (tm, 1), 0)
    first, last = row == 0, row == n_rows - 1
    n_ext = tm + 2 * HALO

    def shifts(t):
        prev = jnp.where(first, 0.0, pltpu.roll(t, 1, 0)[HALO:HALO + tm])
        nxt = jnp.where(last, 0.0, pltpu.roll(t, n_ext - 1, 0)[HALO:HALO + tm])
        return prev, t[HALO:HALO + tm], nxt

    return shifts


def _even_prep_kernel(xc_ref, xp_ref, xn_ref, nw_ref, win_ref, qn_ref, kn_ref, cos_ref, sin_ref, smu_ref, lmu_ref,
                      lin_ref, lout_ref, w0_ref, a0_ref, kk_ref, ka_ref, rk_ref, ones_ref,
                      q_out, k_out, v_out, r_out, vb_out, a_out, lw0_out, lw1_out, k0_out, k1_out, b0_out, b1_out,
                      gate_out, bonus_out, *, tm, n_rows):
    shifts = _make_shifts(tm, n_rows)
    he = _rms(jnp.concatenate([xp_ref[...], xc_ref[...], xn_ref[...]], axis=0), nw_ref[...])
    proj = jnp.dot(he.astype(BF16), win_ref[...], preferred_element_type=F32)
    ones = ones_ref[...]
    lane = lax.broadcasted_iota(jnp.int32, (tm, WIDTH_A), 1) % HEAD_DIM

    def head_rms_rope(t, w):
        t = t * lax.rsqrt(_group_sum(t * t, ones) * (1.0 / HEAD_DIM) + NORM_EPS) * w
        half = ROPE_DIM // 2
        swapped = jnp.where(lane < half, pltpu.roll(t, WIDTH_A - half, 1), pltpu.roll(t, half, 1))
        return t * cos_ref[...] + swapped * sin_ref[...]

    cur = proj[HALO:HALO + tm]
    q = head_rms_rope(cur[:, :WIDTH_A], qn_ref[...]) * HEAD_DIM ** -0.5
    k = head_rms_rope(cur[:, WIDTH_A:2 * WIDTH_A], kn_ref[...])
    v = cur[:, 2 * WIDTH_A:3 * WIDTH_A]
    for h in range(N_HEADS_A):
        sl = slice(h * HEAD_DIM, (h + 1) * HEAD_DIM)
        q_out[h] = q[:, sl].astype(BF16)
        k_out[h] = k[:, sl].astype(BF16)
        v_out[h] = v[:, sl].astype(BF16)

    p_prev, p_cur, p_next = shifts(proj[:, 3 * WIDTH_A:])
    smu = smu_ref[...]
    rkv = p_cur + smu[0:1] * (p_prev - p_cur) + smu[1:2] * (p_next - p_cur)
    r, kin, vb = rkv[:, :WIDTH_B], rkv[:, WIDTH_B:2 * WIDTH_B], rkv[:, 2 * WIDTH_B:]
    h_prev, h_cur, h_next = shifts(he)
    lmu = lmu_ref[...]
    hx = h_cur + lmu[0:1] * (h_prev - h_cur) + lmu[1:2] * (h_next - h_cur)
    l1 = jnp.dot(hx.astype(BF16), lin_ref[...], preferred_element_type=F32)
    l1 = jnp.concatenate([jnp.tanh(l1[:, :128]), l1[:, 128:256], jax.nn.sigmoid(l1[:, 256:])], axis=1)
    l2 = jnp.dot(l1.astype(BF16), lout_ref[...], preferred_element_type=F32)
    w0, a0 = w0_ref[...], a0_ref[...]
    kk = kin * kk_ref[...]
    kk = kk * lax.rsqrt(_group_sum(kk * kk, ones) + 1e-6)
    kdirs = []
    for z, (lw_out, k_out_z, b_out_z) in enumerate(((lw0_out, k0_out, b0_out), (lw1_out, k1_out, b1_out))):
        w_pre = l2[:, z * WIDTH_B:(z + 1) * WIDTH_B] + w0[z:z + 1]
        lw_out[...] = -jnp.exp(-0.5) * jax.nn.sigmoid(w_pre)
        iclr = jax.nn.sigmoid(l2[:, (2 + z) * WIDTH_B:(3 + z) * WIDTH_B] + a0[z:z + 1])
        kdir = kin * (1.0 + (iclr - 1.0) * ka_ref[...])
        k_out_z[...] = kdir
        b_out_z[...] = kk * iclr
        kdirs.append(kdir)
    r_out[...] = r
    vb_out[...] = vb
    a_out[...] = -kk
    gate_out[...] = l2[:, 4 * WIDTH_B:]
    bonus_out[...] = _group_sum(r * (kdirs[0] + kdirs[1]) * rk_ref[...], ones) * vb


def even_prep(x, mix_norm, w_in, q_norm, k_norm, cos_t, sin_t, shift_mu, lora_mu, lora_in, lora_out, w0, a0, k_k, k_a,
              r_k, ones_bd, *, tm=256):
    S, D = x.shape
    cur, prv, nxt = _halo_specs(tm, S, D)
    full = lambda a: pl.BlockSpec(a.shape, lambda i: (0,) * a.ndim)
    rows = pl.BlockSpec((tm, WIDTH_B), lambda i: (i, 0))
    heads = pl.BlockSpec((N_HEADS_A, tm, HEAD_DIM), lambda i: (0, i, 0))
    consts = [mix_norm.reshape(1, D), w_in, q_norm, k_norm]
    consts2 = [shift_mu, lora_mu, lora_in, lora_out, w0, a0, k_k, k_a, r_k, ones_bd]
    return pl.pallas_call(
        functools.partial(_even_prep_kernel, tm=tm, n_rows=S),
        grid=(S // tm,),
        in_specs=[cur, prv, nxt] + [full(a) for a in consts] + [rows, rows] + [full(a) for a in consts2],
        out_specs=[heads] * 3 + [rows] * 11,
        out_shape=[jax.ShapeDtypeStruct((N_HEADS_A, S, HEAD_DIM), BF16)] * 3
        + [jax.ShapeDtypeStruct((S, WIDTH_B), F32)] * 11,
        compiler_params=_params("parallel"),
        name="even_prep",
    )(x, x, x, *consts, cos_t, sin_t, *consts2)


def _odd_prep_kernel(xc_ref, xp_ref, xn_ref, nw_ref, win_ref, cc_ref, cdn_ref, nega_ref, dtb_ref, ones_ref,
                     yc_out, q_out, k_out, v_out, zs_out, bg_out, *, tm, n_rows):
    shifts = _make_shifts(tm, n_rows)
    he = _rms(jnp.concatenate([xp_ref[...], xc_ref[...], xn_ref[...]], axis=0), nw_ref[...])
    proj = jnp.dot(he.astype(BF16), win_ref[...], preferred_element_type=F32)
    cur = proj[HALO:HALO + tm]

    def conv3(t, w):
        prev, mid, nxt = shifts(t)
        return w[0:1] * prev + w[1:2] * mid + w[2:3] * nxt

    o_dn = 3 * WIDTH_C
    yc_out[...] = cur[:, :WIDTH_C] * conv3(proj[:, WIDTH_C:2 * WIDTH_C] * proj[:, 2 * WIDTH_C:o_dn], cc_ref[...])
    qkv = _silu(conv3(proj[:, o_dn:o_dn + 3 * WIDTH_D], cdn_ref[...]))
    ones = ones_ref[...]
    l2n = lambda t: t * lax.rsqrt(_group_sum(t * t, ones) + 1e-6)
    q_out[...] = l2n(qkv[:, :WIDTH_D]) * HEAD_DIM_D ** -0.5
    k_out[...] = l2n(qkv[:, WIDTH_D:2 * WIDTH_D])
    v_out[...] = qkv[:, 2 * WIDTH_D:]
    zs_out[...] = _silu(cur[:, o_dn + 3 * WIDTH_D:o_dn + 4 * WIDTH_D])
    tail = cur[:, o_dn + 4 * WIDTH_D:]
    t = tail + dtb_ref[...]
    softplus = jnp.maximum(t, 0.0) + jnp.log(1.0 + jnp.exp(-jnp.abs(t)))
    lane = lax.broadcasted_iota(jnp.int32, tail.shape, 1)
    bg_out[...] = jnp.where(lane < 2 * N_HEADS_D, jax.nn.sigmoid(tail), nega_ref[...] * softplus)


def odd_prep(x, mix_norm, w_in_pad, conv_c, conv_dn, neg_a, dt_b, ones_bd, *, tm=256):
    S, D = x.shape
    cur, prv, nxt = _halo_specs(tm, S, D)
    full = lambda a: pl.BlockSpec(a.shape, lambda i: (0,) * a.ndim)
    rows = pl.BlockSpec((tm, WIDTH_D), lambda i: (i, 0))
    consts = [mix_norm.reshape(1, D), w_in_pad, conv_c, conv_dn, neg_a, dt_b, ones_bd]
    return pl.pallas_call(
        functools.partial(_odd_prep_kernel, tm=tm, n_rows=S),
        grid=(S // tm,),
        in_specs=[cur, prv, nxt] + [full(a) for a in consts],
        out_specs=[rows] * 5 + [pl.BlockSpec((tm, 128), lambda i: (i, 0))],
        out_shape=[jax.ShapeDtypeStruct((S, WIDTH_D), F32)] * 5 + [jax.ShapeDtypeStruct((S, 128), F32)],
        compiler_params=_params("parallel"),
        name="odd_prep",
    )(x, x, x, *consts)


def _even_post_ffn_kernel(x_ref, ya_ref, y0_ref, y1_ref, gate_ref, bonus_ref, lnw_ref, lnb_ref, ones_ref, wo_ref,
                          fnw_ref, wg_ref, wu_ref, wd_ref, o_ref, h_scr):
    @pl.when(pl.program_id(1) == 0)
    def _():
        ones = ones_ref[...]
        yf = y0_ref[...] + y1_ref[...]
        dev = yf - _group_sum(yf, ones) * (1.0 / HEAD_DIM)
        var = _group_sum(dev * dev, ones) * (1.0 / HEAD_DIM)
        yn = dev * lax.rsqrt(var + RWKV_LN_EPS) * lnw_ref[...] + lnb_ref[...]
        y_b = (yn + bonus_ref[...]) * gate_ref[...]
        y = jnp.concatenate([ya_ref[h] for h in range(N_HEADS_A)] + [y_b], axis=1)
        x = x_ref[...] + jnp.dot(y.astype(BF16), wo_ref[...], preferred_element_type=F32)
        h_scr[...] = _rms(x, fnw_ref[...]).astype(BF16)
        o_ref[...] = x

    h = h_scr[...]
    g = jnp.dot(h, wg_ref[...], preferred_element_type=F32)
    u = jnp.dot(h, wu_ref[...], preferred_element_type=F32)
    o_ref[...] += jnp.dot((_silu(g) * u).astype(BF16), wd_ref[...], preferred_element_type=F32)


def even_post_ffn(x, y_a, y0, y1, gate, bonus, ln_w, ln_b, ones_bd, w_out, ffn_norm, wg, wu, wd, *, tm=512, tf=1408):
    S, D = x.shape
    F = wg.shape[1]
    rows = pl.BlockSpec((tm, WIDTH_B), lambda i, f: (i, 0))
    full = lambda a: pl.BlockSpec(a.shape, lambda i, f: (0,) * a.ndim)
    consts = [ln_w, ln_b, ones_bd, w_out, ffn_norm.reshape(1, D)]
    return pl.pallas_call(
        _even_post_ffn_kernel,
        grid=(S // tm, F // tf),
        in_specs=[pl.BlockSpec((tm, D), lambda i, f: (i, 0)),
                  pl.BlockSpec((N_HEADS_A, tm, HEAD_DIM), lambda i, f: (0, i, 0)),
                  rows, rows, rows, rows] + [full(a) for a in consts] + [
            pl.BlockSpec((D, tf), lambda i, f: (0, f)),
            pl.BlockSpec((D, tf), lambda i, f: (0, f)),
            pl.BlockSpec((tf, D), lambda i, f: (f, 0)),
        ],
        out_specs=pl.BlockSpec((tm, D), lambda i, f: (i, 0)),
        out_shape=jax.ShapeDtypeStruct((S, D), F32),
        scratch_shapes=[pltpu.VMEM((tm, D), BF16)],
        compiler_params=_params("parallel", "arbitrary"),
        name="even_post_ffn",
    )(x, y_a, y0, y1, gate, bonus, *consts, wg, wu, wd)


def _route_record(h, wr):
    logits = jnp.dot(h, wr, precision=lax.Precision.HIGHEST, preferred_element_type=F32)
    lane = lax.broadcasted_iota(jnp.int32, logits.shape, 1)
    valid = lane < N_EXPERTS
    lg = jnp.where(valid, logits, NEG_INF)
    e = jnp.exp(lg - jnp.max(lg, axis=-1, keepdims=True))
    p = e / jnp.sum(e, axis=-1, keepdims=True)
    pm = jnp.where(valid, p, -1.0)
    m1 = jnp.max(pm, axis=-1, keepdims=True)
    i1 = jnp.min(jnp.where(pm == m1, lane, 128), axis=-1, keepdims=True)
    pm2 = jnp.where(lane == i1, -1.0, pm)
    m2 = jnp.max(pm2, axis=-1, keepdims=True)
    i2 = jnp.min(jnp.where(pm2 == m2, lane, 128), axis=-1, keepdims=True)
    tot = m1 + m2
    g1, g2 = m1 / tot, m2 / tot
    out = jnp.where(lane == i1, g1, 0.0) + jnp.where(lane == i2, g2, 0.0)
    out = jnp.where(lane == ROUTE_LANE_I1, i1.astype(F32), out)
    out = jnp.where(lane == ROUTE_LANE_I2, i2.astype(F32), out)
    out = jnp.where(lane == ROUTE_LANE_G1, g1, out)
    return jnp.where(lane == ROUTE_LANE_G2, g2, out)


def _odd_post_kernel(x_ref, yc_ref, o0_ref, o1_ref, zs_ref, dnw_ref, ones_ref, wo_ref, fnw_ref, wr_ref, x_out, route_out):
    o = o0_ref[...] + o1_ref[...]
    ms = _group_sum(o * o, ones_ref[...]) * (1.0 / HEAD_DIM_D)
    y_d = o * lax.rsqrt(ms + NORM_EPS) * dnw_ref[...] * zs_ref[...]
    y = jnp.concatenate([yc_ref[...], y_d], axis=1)
    x = x_ref[...] + jnp.dot(y.astype(BF16), wo_ref[...], preferred_element_type=F32)
    x_out[...] = x
    route_out[...] = _route_record(_rms(x, fnw_ref[...]), wr_ref[...])


def odd_post(x, y_c, o0, o1, zs, dn_norm, ones_bd, w_out, ffn_norm, wr_pad, *, tm=512):
    S, D = x.shape
    rows = pl.BlockSpec((tm, WIDTH_D), lambda i: (i, 0))
    full = lambda a: pl.BlockSpec(a.shape, lambda i: (0,) * a.ndim)
    consts = [dn_norm, ones_bd, w_out, ffn_norm.reshape(1, D), wr_pad]
    return pl.pallas_call(
        _odd_post_kernel,
        grid=(S // tm,),
        in_specs=[pl.BlockSpec((tm, D), lambda i: (i, 0)), rows, rows, rows, rows] + [full(a) for a in consts],
        out_specs=[pl.BlockSpec((tm, D), lambda i: (i, 0)), pl.BlockSpec((tm, 128), lambda i: (i, 0))],
        out_shape=[jax.ShapeDtypeStruct((S, D), F32), jax.ShapeDtypeStruct((S, 128), F32)],
        compiler_params=_params("parallel"),
        name="odd_post",
    )(x, y_c, o0, o1, zs, *consts)


def _row_copy(src_hbm, src_row, dst_vmem, dst_row, sem):
    return pltpu.make_async_copy(src_hbm.at[pl.ds(src_row, 1)], dst_vmem.at[pl.ds(dst_row, 1)], sem)


def _moe_group_kernel(te_ref, tv_ref, tok_ref, x_hbm, nw_ref, wg_ref, wu_ref, wd_ref, o_ref, xbuf, h_scr, sem, *, tm):
    m = pl.program_id(0)
    f = pl.program_id(1)
    valid = tv_ref[m] > 0
    slot = m % 2

    def gather_tile(tile, into):
        base = tile * tm

        def start(j, carry):
            _row_copy(x_hbm, tok_ref[base + j], xbuf.at[into], j, sem.at[into]).start()
            return carry

        lax.fori_loop(0, tm, start, 0, unroll=8)

    @pl.when((f == 0) & (m == 0))
    def _():
        gather_tile(0, 0)

    nxt = jnp.minimum(m + 1, pl.num_programs(0) - 1)

    @pl.when((f == 0) & (nxt > m) & (tv_ref[nxt] > 0))
    def _():
        gather_tile(nxt, 1 - slot)

    @pl.when(valid & (f == 0))
    def _():
        pltpu.make_async_copy(x_hbm.at[pl.ds(0, tm)], xbuf.at[slot], sem.at[slot]).wait()
        h_scr[...] = _rms(xbuf[slot], nw_ref[...]).astype(BF16)

    @pl.when(valid)
    def _():
        h = h_scr[...]
        g = jnp.dot(h, wg_ref[...], preferred_element_type=F32)
        u = jnp.dot(h, wu_ref[...], preferred_element_type=F32)
        y = jnp.dot((g * jax.nn.sigmoid(g) * u).astype(BF16), wd_ref[...], preferred_element_type=F32)

        @pl.when(f == 0)
        def _():
            o_ref[...] = y

        @pl.when(f != 0)
        def _():
            o_ref[...] += y

    @pl.when(jnp.logical_not(valid) & (f == 0))
    def _():
        o_ref[...] = jnp.zeros_like(o_ref)


def moe_grouped_ffn(x, nw, tile_expert, tile_valid, src_tok, wg, wu, wd, *, tm, tf=1792):
    S, D = x.shape
    E, _, F = wg.shape
    n_tiles = tile_expert.shape[0]
    nf = F // tf

    def w_in(m, f, te, tv, tok):
        return (te[m], 0, jnp.where(tv[m] > 0, f, nf - 1))

    def w_out(m, f, te, tv, tok):
        return (te[m], jnp.where(tv[m] > 0, f, nf - 1), 0)

    grid_spec = pltpu.PrefetchScalarGridSpec(
        num_scalar_prefetch=3,
        grid=(n_tiles, nf),
        in_specs=[
            pl.BlockSpec(memory_space=pl.ANY),
            pl.BlockSpec((1, D), lambda m, f, te, tv, tok: (0, 0)),
            pl.BlockSpec((None, D, tf), w_in),
            pl.BlockSpec((None, D, tf), w_in),
            pl.BlockSpec((None, tf, D), w_out),
        ],
        out_specs=pl.BlockSpec((tm, D), lambda m, f, te, tv, tok: (m, 0)),
        scratch_shapes=[pltpu.VMEM((2, tm, D), F32), pltpu.VMEM((tm, D), BF16), pltpu.SemaphoreType.DMA((2,))],
    )
    return pl.pallas_call(
        functools.partial(_moe_group_kernel, tm=tm),
        grid_spec=grid_spec,
        out_shape=jax.ShapeDtypeStruct((n_tiles * tm, D), F32),
        compiler_params=_params("arbitrary", "arbitrary"),
        name="moe_grouped_ffn",
    )(tile_expert, tile_valid, src_tok, x, nw.reshape(1, D), wg, wu, wd)


def _moe_combine_kernel(pos_ref, x_ref, rt_ref, ys_hbm, o_ref, buf, sem, *, tc, n_tok):
    i = pl.program_id(0)
    slot = i % 2

    def gather_tile(tile, into):
        base = tile * tc

        def start(j, carry):
            for k in range(TOP_K):
                _row_copy(ys_hbm, pos_ref[k * n_tok + base + j], buf.at[into, k], j, sem.at[into]).start()
            return carry

        lax.fori_loop(0, tc, start, 0, unroll=8)

    @pl.when(i == 0)
    def _():
        gather_tile(0, 0)

    @pl.when(i + 1 < pl.num_programs(0))
    def _():
        gather_tile(i + 1, 1 - slot)

    for k in range(TOP_K):
        pltpu.make_async_copy(ys_hbm.at[pl.ds(0, tc)], buf.at[slot, k], sem.at[slot]).wait()
    rt = rt_ref[...]
    o_ref[...] = (x_ref[...] + rt[:, ROUTE_LANE_G1:ROUTE_LANE_G1 + 1] * buf[slot, 0]
                  + rt[:, ROUTE_LANE_G2:ROUTE_LANE_G2 + 1] * buf[slot, 1])


def moe_combine(x, route, ys, pos, *, tc=256):
    S, D = x.shape
    grid_spec = pltpu.PrefetchScalarGridSpec(
        num_scalar_prefetch=1,
        grid=(S // tc,),
        in_specs=[
            pl.BlockSpec((tc, D), lambda i, pos: (i, 0)),
            pl.BlockSpec((tc, 128), lambda i, pos: (i, 0)),
            pl.BlockSpec(memory_space=pl.ANY),
        ],
        out_specs=pl.BlockSpec((tc, D), lambda i, pos: (i, 0)),
        scratch_shapes=[pltpu.VMEM((2, TOP_K, tc, D), F32), pltpu.SemaphoreType.DMA((2,))],
    )
    return pl.pallas_call(
        functools.partial(_moe_combine_kernel, tc=tc, n_tok=S),
        grid_spec=grid_spec,
        out_shape=jax.ShapeDtypeStruct((S, D), F32),
        compiler_params=_params("arbitrary"),
        name="moe_combine",
    )(pos, x, route, ys)


def moe_top2(x, nw, route, wg, wu, wd, *, tm=512):
    S, D = x.shape
    E = wg.shape[0]
    experts = jnp.concatenate([route[:, ROUTE_LANE_I1], route[:, ROUTE_LANE_I2]]).astype(jnp.int32)
    onehot = (experts[:, None] == jnp.arange(E, dtype=jnp.int32)).astype(jnp.int32)
    csum = jnp.cumsum(onehot, axis=0)
    rank = jnp.sum(onehot * csum, axis=1) - 1
    padded = (csum[-1] + tm - 1) // tm * tm
    ends = jnp.cumsum(padded)
    pos = (jnp.sum(onehot * (ends - padded), axis=1) + rank).astype(jnp.int32)
    n_tiles = TOP_K * S // tm + E
    tokens = jnp.tile(jnp.arange(S, dtype=jnp.int32), TOP_K)
    src_tok = jnp.zeros((n_tiles * tm,), jnp.int32).at[pos].set(tokens)
    tile_start = jnp.arange(n_tiles, dtype=jnp.int32) * tm
    tile_expert = jnp.minimum(jnp.sum(tile_start[:, None] >= ends[None, :], axis=1), E - 1).astype(jnp.int32)
    tile_valid = (tile_start < ends[-1]).astype(jnp.int32)
    ys = moe_grouped_ffn(x, nw, tile_expert, tile_valid, src_tok, wg, wu, wd, tm=tm)
    return moe_combine(x, route, ys, pos)


def _attn_branch_kernel(*refs, d, rb, bq, n_rows, first, last):
    q_ref, kp_ref, kc_ref, kn_ref, vp_ref, vc_ref, vn_ref = refs[:7]
    pos = 7
    if not first:
        m_in, l_in, a_in = refs[pos:pos + 3]
        pos += 3
    if last:
        o_ref = refs[pos]
        pos += 1
    else:
        m_out, l_out, a_out = refs[pos:pos + 3]
        pos += 3
    kbuf, vbuf = refs[pos:pos + 2]

    kbuf[0:64, :] = kp_ref[...]
    kbuf[64:64 + rb, :] = kc_ref[...]
    kbuf[64 + rb:, :] = kn_ref[...]
    vbuf[0:64, :] = vp_ref[...]
    vbuf[64:64 + rb, :] = vc_ref[...]
    vbuf[64 + rb:, :] = vn_ref[...]

    row0 = pl.program_id(1) * rb
    qi = lax.broadcasted_iota(jnp.int32, (bq, bq + 128), 0)
    kj = lax.broadcasted_iota(jnp.int32, (bq, bq + 128), 1)
    band = (kj >= qi) & (kj <= qi + 128)
    for s in range(rb // bq):
        gk = row0 + (s * bq - 64) + kj
        mask = band & (gk >= 0) & (gk < n_rows)
        rows = slice(s * bq, (s + 1) * bq)
        win = slice(s * bq, s * bq + bq + 128)
        for r in range(d):
            cols = slice(r * HEAD_DIM, (r + 1) * HEAD_DIM)
            sc = lax.dot_general(q_ref[rows, cols], kbuf[win, cols], (((1,), (1,)), ((), ())),
                                 preferred_element_type=F32)
            sc = jnp.where(mask, sc, NEG_INF)
            m_b = jnp.max(sc, axis=-1, keepdims=True)
            p = jnp.exp(sc - m_b)
            l_b = jnp.sum(p, axis=-1, keepdims=True)
            acc_b = jnp.dot(p.astype(BF16), vbuf[win, cols], preferred_element_type=F32)
            if first:
                m_n = jnp.broadcast_to(m_b, (bq, HEAD_DIM))
                l_n = jnp.broadcast_to(l_b, (bq, HEAD_DIM))
                a_n = acc_b
            else:
                m_o = m_in[rows, cols]
                m_n = jnp.maximum(m_o, m_b)
                w_o = jnp.exp(m_o - m_n)
                w_b = jnp.exp(m_b - m_n)
                l_n = l_in[rows, cols] * w_o + l_b * w_b
                a_n = a_in[rows, cols] * w_o + acc_b * w_b
            if last:
                o_ref[rows, cols] = a_n / l_n
            else:
                m_out[rows, cols] = m_n
                l_out[rows, cols] = l_n
                a_out[rows, cols] = a_n


def _attn_branch(q, k, v, state, *, d, rb, bq, first, last):
    H, S, _ = q.shape
    L = S // d
    W = d * HEAD_DIM
    qv, kv, vv = (t.reshape(H, L, W) for t in (q, k, v))
    nb = rb // 64
    last_blk = L // 64 - 1
    cur = pl.BlockSpec((None, rb, W), lambda h, i: (h, i, 0))
    prv = pl.BlockSpec((None, 64, W), lambda h, i: (h, jnp.maximum(i * nb - 1, 0), 0))
    nxt = pl.BlockSpec((None, 64, W), lambda h, i: (h, jnp.minimum((i + 1) * nb, last_blk), 0))
    in_specs = [cur, prv, cur, nxt, prv, cur, nxt]
    args = [qv, kv, kv, kv, vv, vv, vv]
    if not first:
        in_specs += [cur, cur, cur]
        args += [t.reshape(H, L, W) for t in state]
    sds = jax.ShapeDtypeStruct((H, L, W), F32)
    if last:
        out_specs, out_shape = cur, sds
    else:
        out_specs, out_shape = [cur, cur, cur], [sds, sds, sds]
    out = pl.pallas_call(
        functools.partial(_attn_branch_kernel, d=d, rb=rb, bq=bq, n_rows=L, first=first, last=last),
        grid=(H, L // rb),
        in_specs=in_specs,
        out_specs=out_specs,
        out_shape=out_shape,
        scratch_shapes=[pltpu.VMEM((rb + 128, W), BF16), pltpu.VMEM((rb + 128, W), BF16)],
        compiler_params=_params("parallel", "arbitrary"),
        name=f"dilated_attn_d{d}",
    )(*args)
    if last:
        return out.reshape(H, S, HEAD_DIM)
    return tuple(t.reshape(H, S, HEAD_DIM) for t in out)


def dilated_attention(q, k, v):
    st = _attn_branch(q, k, v, None, d=1, rb=1024, bq=256, first=True, last=False)
    st = _attn_branch(q, k, v, st, d=4, rb=512, bq=256, first=False, last=False)
    return _attn_branch(q, k, v, st, d=16, rb=256, bq=256, first=False, last=True)


def _tri_masks(z):
    row = lax.broadcasted_iota(jnp.int32, (CHUNK, CHUNK), 0)
    col = lax.broadcasted_iota(jnp.int32, (CHUNK, CHUNK), 1)
    if z == 0:
        return col <= row, col < row
    return col >= row, col > row


def _neumann_solve(ns, xs):
    steps = CHUNK.bit_length() - 1
    for i in range(steps):
        xs = [x + _bdot(n, x) for n, x in zip(ns, xs)]
        if i + 1 < steps:
            ns = [_bdot(n, n) for n in ns]
    return xs


def _rwkv_chunk_kernel(*refs):
    ins = (refs[0:6], refs[6:12])
    y_refs = refs[12:14]
    state = refs[14]

    @pl.when(pl.program_id(0) == 0)
    def _():
        state[...] = jnp.zeros_like(state)

    chains = [(z, h) for z in range(2) for h in range(N_HEADS_B)]
    masks = [_tri_masks(z) for z in range(2)]
    prep = []
    for z in range(2):
        r_ref, v_ref, a_ref, lw_ref, k_ref, b_ref = ins[z]
        tri = jnp.where(masks[z][0], 1.0, 0.0).astype(BF16)
        lw = lw_ref[...]
        hi, mid, lo = _split3(lw)
        cum = (jnp.dot(tri, hi, preferred_element_type=F32) + jnp.dot(tri, mid, preferred_element_type=F32)
               + jnp.dot(tri, lo, preferred_element_type=F32))
        tot = jnp.sum(lw, axis=0, keepdims=True)
        e_neg = jnp.exp(-cum)
        e_end = jnp.exp(tot - cum)
        k = k_ref[...]
        b = b_ref[...]
        prep.append(dict(rt=r_ref[...] * jnp.exp(cum), at=a_ref[...] * jnp.exp(cum - lw), kt=k * e_neg, bt=b * e_neg,
                         kh=k * e_end, bh=b * e_end, dw=jnp.exp(tot), v=v_ref[...]))

    def part(name, z, h):
        return prep[z][name][:, h * HEAD_DIM:(h + 1) * HEAD_DIM]

    xs = [_bdot_nt(jnp.concatenate([part("at", z, h), part("rt", z, h)], axis=0),
                   jnp.concatenate([part("bt", z, h), part("kt", z, h)], axis=0)) for z, h in chains]
    a_ab = [jnp.where(masks[z][1], x[:CHUNK, :CHUNK], 0.0) for (z, h), x in zip(chains, xs)]
    a_rb = [jnp.where(masks[z][0], x[CHUNK:, :CHUNK], 0.0) for (z, h), x in zip(chains, xs)]
    cys = [_bdot(jnp.concatenate([jnp.where(masks[z][1], x[:CHUNK, CHUNK:], 0.0),
                                  jnp.where(masks[z][0], x[CHUNK:, CHUNK:], 0.0)], axis=0), part("v", z, h))
           for (z, h), x in zip(chains, xs)]
    pqs = _neumann_solve(a_ab, [jnp.concatenate([part("at", z, h), cy[:CHUNK]], axis=1)
                                for (z, h), cy in zip(chains, cys)])
    s_old = [state[z, h] for z, h in chains]
    us = [_bdot_nt(pq[:, :HEAD_DIM], s) + pq[:, HEAD_DIM:] for pq, s in zip(pqs, s_old)]
    ys = [_bdot_nt(part("rt", z, h), s) for (z, h), s in zip(chains, s_old)]
    ys = [y + _bdot(arb, u) + cy[CHUNK:] for y, arb, u, cy in zip(ys, a_rb, us, cys)]
    s_new = [s * part("dw", z, h) + _bdot_tn(u, part("bh", z, h)) for (z, h), s, u in zip(chains, s_old, us)]
    s_new = [s + _bdot_tn(part("v", z, h), part("kh", z, h)) for (z, h), s in zip(chains, s_new)]
    for (z, h), s, y in zip(chains, s_new, ys):
        state[z, h] = s
        y_refs[z][:, h * HEAD_DIM:(h + 1) * HEAD_DIM] = y


def rwkv7_scan(r, v, a, lw, k, b):
    S, C = r.shape
    n = S // CHUNK
    fwd = pl.BlockSpec((CHUNK, C), lambda c: (c, 0))
    bwd = pl.BlockSpec((CHUNK, C), lambda c: (n - 1 - c, 0))
    y0, y1 = pl.pallas_call(
        _rwkv_chunk_kernel,
        grid=(n,),
        in_specs=[fwd] * 6 + [bwd] * 6,
        out_specs=[fwd, bwd],
        out_shape=[jax.ShapeDtypeStruct((S, C), F32)] * 2,
        scratch_shapes=[pltpu.VMEM((2, N_HEADS_B, HEAD_DIM, HEAD_DIM), F32)],
        compiler_params=_params("arbitrary"),
        name="rwkv7_scan",
    )(r, v, a, lw[0], k[0], b[0], r, v, a, lw[1], k[1], b[1])
    return y0, y1


def _dn_chunk_kernel(*refs):
    ins = (refs[0:6], refs[6:12])
    o_refs = refs[12:14]
    state = refs[14]

    @pl.when(pl.program_id(0) == 0)
    def _():
        state[...] = jnp.zeros_like(state)

    chains = [(z, h) for z in range(2) for h in range(N_HEADS_D)]
    masks = [_tri_masks(z) for z in range(2)]
    nt = (((1,), (1,)), ((), ()))
    gcs, decays, betas, g_lasts = [], [], [], []
    for z in range(2):
        _, _, _, bcol_ref, gcol_ref, grow_ref = ins[z]
        incl = masks[z][0]
        tri = jnp.where(incl, 1.0, 0.0).astype(BF16)
        c_hi, c_mid, c_lo = _split3(gcol_ref[...])
        gc_cols = (jnp.dot(tri, c_hi, preferred_element_type=F32) + jnp.dot(tri, c_mid, preferred_element_type=F32)
                   + jnp.dot(tri, c_lo, preferred_element_type=F32))
        r_hi, r_mid, r_lo = _split3(grow_ref[...])
        gc_rows = (lax.dot_general(r_hi, tri, nt, preferred_element_type=F32)
                   + lax.dot_general(r_mid, tri, nt, preferred_element_type=F32)
                   + lax.dot_general(r_lo, tri, nt, preferred_element_type=F32))
        last = CHUNK - 1 if z == 0 else 0
        bcol = bcol_ref[...]
        for h in range(N_HEADS_D):
            idx = z * N_HEADS_D + h
            gc = gc_cols[:, idx:idx + 1]
            diff = gc - gc_rows[idx:idx + 1, :]
            gcs.append(gc)
            decays.append(jnp.where(incl, jnp.exp(jnp.where(incl, diff, 0.0)), 0.0))
            betas.append(bcol[:, idx:idx + 1])
            g_lasts.append(gc[last:last + 1, :])

    def part(i, z, h):
        return ins[z][i][:, h * HEAD_DIM_D:(h + 1) * HEAD_DIM_D]

    qs = [part(0, z, h) for z, h in chains]
    ks = [part(1, z, h) for z, h in chains]
    vs = [part(2, z, h) for z, h in chains]
    kbs = [k * beta for k, beta in zip(ks, betas)]
    e_gcs = [jnp.exp(gc) for gc in gcs]
    kqs = [_bdot_nt(jnp.concatenate([kb, q], axis=0), k) for kb, q, k in zip(kbs, qs, ks)]
    n_mats = [jnp.where(masks[z][1], -(kq[:CHUNK] * dc), 0.0) for (z, h), kq, dc in zip(chains, kqs, decays)]
    attns = [kq[CHUNK:] * dc for kq, dc in zip(kqs, decays)]
    uks = _neumann_solve(n_mats, [jnp.concatenate([v * beta, kb * e], axis=1)
                                  for v, beta, kb, e in zip(vs, betas, kbs, e_gcs)])
    s_old = [state[z, h] for z, h in chains]
    us = [uk[:, :HEAD_DIM_D] - _bdot(uk[:, HEAD_DIM_D:], s) for uk, s in zip(uks, s_old)]
    os_ = [_bdot(q * e, s) for q, e, s in zip(qs, e_gcs, s_old)]
    os_ = [o + _bdot(attn, u) for o, attn, u in zip(os_, attns, us)]
    s_new = [s * jnp.exp(gl) + _bdot_tn(k * jnp.exp(gl - gc), u)
             for s, gl, k, gc, u in zip(s_old, g_lasts, ks, gcs, us)]
    for (z, h), s, o in zip(chains, s_new, os_):
        state[z, h] = s
        o_refs[z][:, h * HEAD_DIM_D:(h + 1) * HEAD_DIM_D] = o


def deltanet_scan(q, k, v, beta, g):
    S, C = q.shape
    n = S // CHUNK
    g_rows = g.reshape(n, CHUNK, 2 * N_HEADS_D).transpose(0, 2, 1)
    nz = 2 * N_HEADS_D

    def specs(idx):
        wide = pl.BlockSpec((CHUNK, C), lambda c: (idx(c), 0))
        col = pl.BlockSpec((CHUNK, nz), lambda c: (idx(c), 0))
        row = pl.BlockSpec((None, nz, CHUNK), lambda c: (idx(c), 0, 0))
        return [wide, wide, wide, col, col, row], wide

    in_f, out_f = specs(lambda c: c)
    in_b, out_b = specs(lambda c: n - 1 - c)
    o0, o1 = pl.pallas_call(
        _dn_chunk_kernel,
        grid=(n,),
        in_specs=in_f + in_b,
        out_specs=[out_f, out_b],
        out_shape=[jax.ShapeDtypeStruct((S, C), F32)] * 2,
        scratch_shapes=[pltpu.VMEM((2, N_HEADS_D, HEAD_DIM_D, HEAD_DIM_D), F32)],
        compiler_params=_params("arbitrary"),
        name="deltanet_scan",
    )(q, k, v, beta, g, g_rows, q, k, v, beta, g, g_rows)
    return o0, o1


def _rope_tables(positions):
    S = positions.shape[0]
    half = ROPE_DIM // 2
    inv_freq = jnp.power(ROPE_THETA, -jnp.arange(half, dtype=F32) / half)
    ang = positions[:, None].astype(F32) * inv_freq
    cos, sin = jnp.cos(ang), jnp.sin(ang)
    rest = HEAD_DIM - ROPE_DIM
    cos_t = jnp.concatenate([cos, cos, jnp.ones((S, rest), F32)], axis=1)
    sin_t = jnp.concatenate([-sin, sin, jnp.zeros((S, rest), F32)], axis=1)
    return jnp.tile(cos_t, (1, N_HEADS_A)), jnp.tile(sin_t, (1, N_HEADS_A))


def _ones_block_diag(width, group):
    idx = jnp.arange(width) // group
    return (idx[:, None] == idx[None, :]).astype(BF16)


def _block_diag(blocks):
    rows = sum(b.shape[0] for b in blocks)
    cols = sum(b.shape[1] for b in blocks)
    out = jnp.zeros((rows, cols), blocks[0].dtype)
    r = c = 0
    for b in blocks:
        out = lax.dynamic_update_slice(out, b, (r, c))
        r += b.shape[0]
        c += b.shape[1]
    return out


def _even_layer(x, positions, mix_norm, w_in, q_norm, k_norm, shift_mu, lora_mu, w0, w1, w2, a0, a1, a2, g1, g2,
                k_k, k_a, r_k, ln_w, ln_b, w_out, ffn_norm, ffn_gate, ffn_up, ffn_down):
    row = lambda t: t.reshape(1, -1)
    cos_t, sin_t = _rope_tables(positions)
    ones_bd = _ones_block_diag(WIDTH_B, HEAD_DIM)
    lora_in = jnp.concatenate([w1[0], w1[1], a1[0], a1[1], g1], axis=1).astype(BF16)
    lora_out = _block_diag([w2[0], w2[1], a2[0], a2[1], g2]).astype(BF16)
    (q, k, v, r, vb, a_vec, lw0, lw1, k0, k1, b0, b1, gate, bonus) = even_prep(
        x, mix_norm, w_in.astype(BF16), row(jnp.tile(q_norm, N_HEADS_A)), row(jnp.tile(k_norm, N_HEADS_A)), cos_t, sin_t,
        shift_mu, lora_mu, lora_in, lora_out, w0, a0, row(k_k), row(k_a), row(r_k), ones_bd)
    y_a = dilated_attention(q, k, v)
    y0, y1 = rwkv7_scan(r, vb, a_vec, (lw0, lw1), (k0, k1), (b0, b1))
    return even_post_ffn(x, y_a, y0, y1, gate, bonus, row(ln_w), row(ln_b), ones_bd, w_out.astype(BF16), ffn_norm,
                         ffn_gate.astype(BF16), ffn_up.astype(BF16), ffn_down.astype(BF16))


def _odd_layer(x, mix_norm, w_in, conv_c, conv_dn, A_log, dt_bias, dn_norm, w_out, ffn_norm, router, moe_gate, moe_up, moe_down):
    n_in = w_in.shape[1]
    n_pad = -(-n_in // 128) * 128
    w_in_p = jnp.pad(w_in, ((0, 0), (0, n_pad - n_in))).astype(BF16)
    nz = 2 * N_HEADS_D
    neg_a = jnp.zeros((1, 128), F32).at[0, nz:2 * nz].set(-jnp.exp(A_log.reshape(-1)))
    dt_b = jnp.zeros((1, 128), F32).at[0, nz:2 * nz].set(dt_bias.reshape(-1))
    ones_bd = _ones_block_diag(WIDTH_D, HEAD_DIM_D)
    y_c, q, k, v, zs, bg = odd_prep(x, mix_norm, w_in_p, conv_c, conv_dn, neg_a, dt_b, ones_bd)
    o0, o1 = deltanet_scan(q, k, v, bg[:, :nz], bg[:, nz:2 * nz])
    wr_pad = jnp.pad(router, ((0, 0), (0, 128 - N_EXPERTS)))
    x, route = odd_post(x, y_c, o0, o1, zs, jnp.tile(dn_norm, N_HEADS_D).reshape(1, -1), ones_bd, w_out.astype(BF16),
                        ffn_norm, wr_pad)
    return moe_top2(x, ffn_norm, route, moe_gate.astype(BF16), moe_up.astype(BF16), moe_down.astype(BF16))


def kernel(x, positions, ev_mix_norm, ev_w_in, ev_q_norm, ev_k_norm, ev_shift_mu, ev_lora_mu, ev_w0, ev_w1, ev_w2, ev_a0, ev_a1, ev_a2, ev_g1, ev_g2, ev_k_k, ev_k_a, ev_r_k, ev_ln_w, ev_ln_b, ev_w_out, ev_ffn_norm, ev_ffn_gate, ev_ffn_up, ev_ffn_down, od_mix_norm, od_w_in, od_conv_c, od_conv_dn, od_A_log, od_dt_bias, od_dn_norm, od_w_out, od_ffn_norm, od_router, od_moe_gate, od_moe_up, od_moe_down):
    B, S, D = x.shape
    assert B == 1
    xs = x.reshape(S, D)
    pos = positions.reshape(S)
    n_layers = ev_mix_norm.shape[0] + od_mix_norm.shape[0]
    for layer in range(n_layers):
        i = layer // 2
        if layer % 2 == 0:
            xs = _even_layer(xs, pos, ev_mix_norm[i], ev_w_in[i], ev_q_norm[i], ev_k_norm[i], ev_shift_mu[i], ev_lora_mu[i],
                             ev_w0[i], ev_w1[i], ev_w2[i], ev_a0[i], ev_a1[i], ev_a2[i], ev_g1[i], ev_g2[i], ev_k_k[i],
                             ev_k_a[i], ev_r_k[i], ev_ln_w[i], ev_ln_b[i], ev_w_out[i], ev_ffn_norm[i], ev_ffn_gate[i],
                             ev_ffn_up[i], ev_ffn_down[i])
        else:
            xs = _odd_layer(xs, od_mix_norm[i], od_w_in[i], od_conv_c[i], od_conv_dn[i], od_A_log[i], od_dt_bias[i],
                            od_dn_norm[i], od_w_out[i], od_ffn_norm[i], od_router[i], od_moe_gate[i], od_moe_up[i],
                            od_moe_down[i])
    return xs.reshape(B, S, D)
```

```python
import functools

import jax
import jax.numpy as jnp
from jax import lax
from jax.experimental import pallas as pl
from jax.experimental.pallas import tpu as pltpu

F32 = jnp.float32
BF16 = jnp.bfloat16

HEAD_DIM = 64
N_HEADS_A = 8
WIDTH_A = N_HEADS_A * HEAD_DIM
DILATION_PATTERNS = ((128, 1), (512, 4), (2048, 16))
ROPE_DIM = HEAD_DIM // 4
ROPE_THETA = 500000.0
N_HEADS_B = 8
WIDTH_B = N_HEADS_B * HEAD_DIM
RWKV_LN_EPS = 64e-5
WIDTH_C = 512
N_HEADS_D = 4
HEAD_DIM_D = 128
WIDTH_D = N_HEADS_D * HEAD_DIM_D
CHUNK = 64
N_EXPERTS = 8
TOP_K = 2
ROUTE_LANE_I1, ROUTE_LANE_I2, ROUTE_LANE_G1, ROUTE_LANE_G2 = 8, 9, 10, 11
NORM_EPS = 1e-6
NEG_INF = -1e30

V7X_VMEM_LIMIT_BYTES = 56 * 1024 * 1024


def _params(*sem):
    return pltpu.CompilerParams(dimension_semantics=sem, vmem_limit_bytes=V7X_VMEM_LIMIT_BYTES)


def _bdot(a, b):
    return jnp.dot(a.astype(BF16), b.astype(BF16), preferred_element_type=F32)


def _bdot_nt(a, b):
    return lax.dot_general(a.astype(BF16), b.astype(BF16), (((1,), (1,)), ((), ())), preferred_element_type=F32)


def _bdot_tn(a, b):
    return lax.dot_general(a.astype(BF16), b.astype(BF16), (((0,), (0,)), ((), ())), preferred_element_type=F32)


def _split3(x):
    hi = x.astype(BF16)
    r1 = x - hi.astype(F32)
    mid = r1.astype(BF16)
    lo = (r1 - mid.astype(F32)).astype(BF16)
    return hi, mid, lo


def _rms(x, w):
    return x * lax.rsqrt(jnp.mean(x * x, axis=-1, keepdims=True) + NORM_EPS) * w


HALO = 8


def _group_sum(x, ones_bd):
    hi = x.astype(BF16)
    lo = (x - hi.astype(F32)).astype(BF16)
    return jnp.dot(hi, ones_bd, preferred_element_type=F32) + jnp.dot(lo, ones_bd, preferred_element_type=F32)


def _silu(x):
    return x * jax.nn.sigmoid(x)


def _halo_specs(tm, n_rows, width):
    per = tm // HALO
    cur = pl.BlockSpec((tm, width), lambda i: (i, 0))
    prv = pl.BlockSpec((HALO, width), lambda i: (jnp.maximum(i * per - 1, 0), 0))
    nxt = pl.BlockSpec((HALO, width), lambda i: (jnp.minimum((i + 1) * per, n_rows // HALO - 1), 0))
    return cur, prv, nxt


def _make_shifts(tm, n_rows):
    row = pl.program_id(0) * tm + lax.broadcasted_iota(jnp.int32, (tm, 1), 0)
    first, last = row == 0, row == n_rows - 1
    n_ext = tm + 2 * HALO

    def shifts(t):
        prev = jnp.where(first, 0.0, pltpu.roll(t, 1, 0)[HALO:HALO + tm])
        nxt = jnp.where(last, 0.0, pltpu.roll(t, n_ext - 1, 0)[HALO:HALO + tm])
        return prev, t[HALO:HALO + tm], nxt

    return shifts


def _even_prep_kernel(xc_ref, xp_ref, xn_ref, nw_ref, win_ref, qn_ref, kn_ref, cos_ref, sin_ref, smu_ref, lmu_ref,
                      lin_ref, lout_ref, w0_ref, a0_ref, kk_ref, ka_ref, rk_ref, ones_ref,
                      q_out, k_out, v_out, r_out, vb_out, a_out, lw0_out, lw1_out, k0_out, k1_out, b0_out, b1_out,
                      gate_out, bonus_out, *, tm, n_rows):
    shifts = _make_shifts(tm, n_rows)
    he = _rms(jnp.concatenate([xp_ref[...], xc_ref[...], xn_ref[...]], axis=0), nw_ref[...])
    proj = jnp.dot(he.astype(BF16), win_ref[...], preferred_element_type=F32)
    ones = ones_ref[...]
    lane = lax.broadcasted_iota(jnp.int32, (tm, WIDTH_A), 1) % HEAD_DIM

    def head_rms_rope(t, w):
        t = t * lax.rsqrt(_group_sum(t * t, ones) * (1.0 / HEAD_DIM) + NORM_EPS) * w
        half = ROPE_DIM // 2
        swapped = jnp.where(lane < half, pltpu.roll(t, WIDTH_A - half, 1), pltpu.roll(t, half, 1))
        return t * cos_ref[...] + swapped * sin_ref[...]

    cur = proj[HALO:HALO + tm]
    q = head_rms_rope(cur[:, :WIDTH_A], qn_ref[...]) * HEAD_DIM ** -0.5
    k = head_rms_rope(cur[:, WIDTH_A:2 * WIDTH_A], kn_ref[...])
    v = cur[:, 2 * WIDTH_A:3 * WIDTH_A]
    q_out[...] = q
    k_out[...] = k
    v_out[...] = v

    p_prev, p_cur, p_next = shifts(proj[:, 3 * WIDTH_A:])
    smu = smu_ref[...]
    rkv = p_cur + smu[0:1] * (p_prev - p_cur) + smu[1:2] * (p_next - p_cur)
    r, kin, vb = rkv[:, :WIDTH_B], rkv[:, WIDTH_B:2 * WIDTH_B], rkv[:, 2 * WIDTH_B:]
    h_prev, h_cur, h_next = shifts(he)
    lmu = lmu_ref[...]
    hx = h_cur + lmu[0:1] * (h_prev - h_cur) + lmu[1:2] * (h_next - h_cur)
    l1 = jnp.dot(hx.astype(BF16), lin_ref[...], preferred_element_type=F32)
    l1 = jnp.concatenate([jnp.tanh(l1[:, :128]), l1[:, 128:256], jax.nn.sigmoid(l1[:, 256:])], axis=1)
    l2 = jnp.dot(l1.astype(BF16), lout_ref[...], preferred_element_type=F32)
    w0, a0 = w0_ref[...], a0_ref[...]
    kk = kin * kk_ref[...]
    kk = kk * lax.rsqrt(_group_sum(kk * kk, ones) + 1e-6)
    kdirs = []
    for z, (lw_out, k_out_z, b_out_z) in enumerate(((lw0_out, k0_out, b0_out), (lw1_out, k1_out, b1_out))):
        w_pre = l2[:, z * WIDTH_B:(z + 1) * WIDTH_B] + w0[z:z + 1]
        lw_out[...] = -jnp.exp(-0.5) * jax.nn.sigmoid(w_pre)
        iclr = jax.nn.sigmoid(l2[:, (2 + z) * WIDTH_B:(3 + z) * WIDTH_B] + a0[z:z + 1])
        kdir = kin * (1.0 + (iclr - 1.0) * ka_ref[...])
        k_out_z[...] = kdir
        b_out_z[...] = kk * iclr
        kdirs.append(kdir)
    r_out[...] = r
    vb_out[...] = vb
    a_out[...] = -kk
    gate_out[...] = l2[:, 4 * WIDTH_B:]
    bonus_out[...] = _group_sum(r * (kdirs[0] + kdirs[1]) * rk_ref[...], ones) * vb


def even_prep(x, mix_norm, w_in, q_norm, k_norm, cos_t, sin_t, shift_mu, lora_mu, lora_in, lora_out, w0, a0, k_k, k_a,
              r_k, ones_bd, *, tm=256):
    S, D = x.shape
    cur, prv, nxt = _halo_specs(tm, S, D)
    full = lambda a: pl.BlockSpec(a.shape, lambda i: (0,) * a.ndim)
    rows = pl.BlockSpec((tm, WIDTH_B), lambda i: (i, 0))
    consts = [mix_norm.reshape(1, D), w_in, q_norm, k_norm]
    consts2 = [shift_mu, lora_mu, lora_in, lora_out, w0, a0, k_k, k_a, r_k, ones_bd]
    return pl.pallas_call(
        functools.partial(_even_prep_kernel, tm=tm, n_rows=S),
        grid=(S // tm,),
        in_specs=[cur, prv, nxt] + [full(a) for a in consts] + [rows, rows] + [full(a) for a in consts2],
        out_specs=[rows] * 14,
        out_shape=[jax.ShapeDtypeStruct((S, WIDTH_B), F32)] * 14,
        compiler_params=_params("parallel"),
        name="even_prep",
    )(x, x, x, *consts, cos_t, sin_t, *consts2)


def _odd_prep_kernel(xc_ref, xp_ref, xn_ref, nw_ref, win_ref, cc_ref, cdn_ref, nega_ref, dtb_ref, ones_ref,
                     yc_out, q_out, k_out, v_out, zs_out, bg_out, *, tm, n_rows):
    shifts = _make_shifts(tm, n_rows)
    he = _rms(jnp.concatenate([xp_ref[...], xc_ref[...], xn_ref[...]], axis=0), nw_ref[...])
    proj = jnp.dot(he.astype(BF16), win_ref[...], preferred_element_type=F32)
    cur = proj[HALO:HALO + tm]

    def conv3(t, w):
        prev, mid, nxt = shifts(t)
        return w[0:1] * prev + w[1:2] * mid + w[2:3] * nxt

    o_dn = 3 * WIDTH_C
    yc_out[...] = cur[:, :WIDTH_C] * conv3(proj[:, WIDTH_C:2 * WIDTH_C] * proj[:, 2 * WIDTH_C:o_dn], cc_ref[...])
    qkv = _silu(conv3(proj[:, o_dn:o_dn + 3 * WIDTH_D], cdn_ref[...]))
    ones = ones_ref[...]
    l2n = lambda t: t * lax.rsqrt(_group_sum(t * t, ones) + 1e-6)
    q_out[...] = l2n(qkv[:, :WIDTH_D]) * HEAD_DIM_D ** -0.5
    k_out[...] = l2n(qkv[:, WIDTH_D:2 * WIDTH_D])
    v_out[...] = qkv[:, 2 * WIDTH_D:]
    zs_out[...] = _silu(cur[:, o_dn + 3 * WIDTH_D:o_dn + 4 * WIDTH_D])
    tail = cur[:, o_dn + 4 * WIDTH_D:]
    t = tail + dtb_ref[...]
    softplus = jnp.maximum(t, 0.0) + jnp.log(1.0 + jnp.exp(-jnp.abs(t)))
    lane = lax.broadcasted_iota(jnp.int32, tail.shape, 1)
    bg_out[...] = jnp.where(lane < 2 * N_HEADS_D, jax.nn.sigmoid(tail), nega_ref[...] * softplus)


def odd_prep(x, mix_norm, w_in_pad, conv_c, conv_dn, neg_a, dt_b, ones_bd, *, tm=256):
    S, D = x.shape
    cur, prv, nxt = _halo_specs(tm, S, D)
    full = lambda a: pl.BlockSpec(a.shape, lambda i: (0,) * a.ndim)
    rows = pl.BlockSpec((tm, WIDTH_D), lambda i: (i, 0))
    consts = [mix_norm.reshape(1, D), w_in_pad, conv_c, conv_dn, neg_a, dt_b, ones_bd]
    return pl.pallas_call(
        functools.partial(_odd_prep_kernel, tm=tm, n_rows=S),
        grid=(S // tm,),
        in_specs=[cur, prv, nxt] + [full(a) for a in consts],
        out_specs=[rows] * 5 + [pl.BlockSpec((tm, 128), lambda i: (i, 0))],
        out_shape=[jax.ShapeDtypeStruct((S, WIDTH_D), F32)] * 5 + [jax.ShapeDtypeStruct((S, 128), F32)],
        compiler_params=_params("parallel"),
        name="odd_prep",
    )(x, x, x, *consts)


def _even_post_ffn_kernel(x_ref, ya_ref, y0_ref, y1_ref, gate_ref, bonus_ref, lnw_ref, lnb_ref, ones_ref, wo_ref,
                          fnw_ref, wg_ref, wu_ref, wd_ref, o_ref, h_scr):
    @pl.when(pl.program_id(1) == 0)
    def _():
        ones = ones_ref[...]
        yf = y0_ref[...] + y1_ref[...]
        dev = yf - _group_sum(yf, ones) * (1.0 / HEAD_DIM)
        var = _group_sum(dev * dev, ones) * (1.0 / HEAD_DIM)
        yn = dev * lax.rsqrt(var + RWKV_LN_EPS) * lnw_ref[...] + lnb_ref[...]
        y_b = (yn + bonus_ref[...]) * gate_ref[...]
        y = jnp.concatenate([ya_ref[...], y_b], axis=1)
        x = x_ref[...] + jnp.dot(y.astype(BF16), wo_ref[...], preferred_element_type=F32)
        h_scr[...] = _rms(x, fnw_ref[...]).astype(BF16)
        o_ref[...] = x

    h = h_scr[...]
    g = jnp.dot(h, wg_ref[...], preferred_element_type=F32)
    u = jnp.dot(h, wu_ref[...], preferred_element_type=F32)
    o_ref[...] += jnp.dot((_silu(g) * u).astype(BF16), wd_ref[...], preferred_element_type=F32)


def even_post_ffn(x, y_a, y0, y1, gate, bonus, ln_w, ln_b, ones_bd, w_out, ffn_norm, wg, wu, wd, *, tm=512, tf=1408):
    S, D = x.shape
    F = wg.shape[1]
    rows = pl.BlockSpec((tm, WIDTH_B), lambda i, f: (i, 0))
    full = lambda a: pl.BlockSpec(a.shape, lambda i, f: (0,) * a.ndim)
    consts = [ln_w, ln_b, ones_bd, w_out, ffn_norm.reshape(1, D)]
    return pl.pallas_call(
        _even_post_ffn_kernel,
        grid=(S // tm, F // tf),
        in_specs=[pl.BlockSpec((tm, D), lambda i, f: (i, 0)),
                  rows, rows, rows, rows, rows] + [full(a) for a in consts] + [
            pl.BlockSpec((D, tf), lambda i, f: (0, f)),
            pl.BlockSpec((D, tf), lambda i, f: (0, f)),
            pl.BlockSpec((tf, D), lambda i, f: (f, 0)),
        ],
        out_specs=pl.BlockSpec((tm, D), lambda i, f: (i, 0)),
        out_shape=jax.ShapeDtypeStruct((S, D), F32),
        scratch_shapes=[pltpu.VMEM((tm, D), BF16)],
        compiler_params=_params("parallel", "arbitrary"),
        name="even_post_ffn",
    )(x, y_a, y0, y1, gate, bonus, *consts, wg, wu, wd)


def _route_record(h, wr):
    logits = jnp.dot(h, wr, precision=lax.Precision.HIGHEST, preferred_element_type=F32)
    lane = lax.broadcasted_iota(jnp.int32, logits.shape, 1)
    valid = lane < N_EXPERTS
    lg = jnp.where(valid, logits, NEG_INF)
    e = jnp.exp(lg - jnp.max(lg, axis=-1, keepdims=True))
    p = e / jnp.sum(e, axis=-1, keepdims=True)
    pm = jnp.where(valid, p, -1.0)
    m1 = jnp.max(pm, axis=-1, keepdims=True)
    i1 = jnp.min(jnp.where(pm == m1, lane, 128), axis=-1, keepdims=True)
    pm2 = jnp.where(lane == i1, -1.0, pm)
    m2 = jnp.max(pm2, axis=-1, keepdims=True)
    i2 = jnp.min(jnp.where(pm2 == m2, lane, 128), axis=-1, keepdims=True)
    tot = m1 + m2
    g1, g2 = m1 / tot, m2 / tot
    out = jnp.where(lane == i1, g1, 0.0) + jnp.where(lane == i2, g2, 0.0)
    out = jnp.where(lane == ROUTE_LANE_I1, i1.astype(F32), out)
    out = jnp.where(lane == ROUTE_LANE_I2, i2.astype(F32), out)
    out = jnp.where(lane == ROUTE_LANE_G1, g1, out)
    return jnp.where(lane == ROUTE_LANE_G2, g2, out)


def _odd_post_kernel(x_ref, yc_ref, o0_ref, o1_ref, zs_ref, dnw_ref, ones_ref, wo_ref, fnw_ref, wr_ref, x_out, route_out):
    o = o0_ref[...] + o1_ref[...]
    ms = _group_sum(o * o, ones_ref[...]) * (1.0 / HEAD_DIM_D)
    y_d = o * lax.rsqrt(ms + NORM_EPS) * dnw_ref[...] * zs_ref[...]
    y = jnp.concatenate([yc_ref[...], y_d], axis=1)
    x = x_ref[...] + jnp.dot(y.astype(BF16), wo_ref[...], preferred_element_type=F32)
    x_out[...] = x
    route_out[...] = _route_record(_rms(x, fnw_ref[...]), wr_ref[...])


def odd_post(x, y_c, o0, o1, zs, dn_norm, ones_bd, w_out, ffn_norm, wr_pad, *, tm=512):
    S, D = x.shape
    rows = pl.BlockSpec((tm, WIDTH_D), lambda i: (i, 0))
    full = lambda a: pl.BlockSpec(a.shape, lambda i: (0,) * a.ndim)
    consts = [dn_norm, ones_bd, w_out, ffn_norm.reshape(1, D), wr_pad]
    return pl.pallas_call(
        _odd_post_kernel,
        grid=(S // tm,),
        in_specs=[pl.BlockSpec((tm, D), lambda i: (i, 0)), rows, rows, rows, rows] + [full(a) for a in consts],
        out_specs=[pl.BlockSpec((tm, D), lambda i: (i, 0)), pl.BlockSpec((tm, 128), lambda i: (i, 0))],
        out_shape=[jax.ShapeDtypeStruct((S, D), F32), jax.ShapeDtypeStruct((S, 128), F32)],
        compiler_params=_params("parallel"),
        name="odd_post",
    )(x, y_c, o0, o1, zs, *consts)


def _row_copy(src_hbm, src_row, dst_vmem, dst_row, sem):
    return pltpu.make_async_copy(src_hbm.at[pl.ds(src_row, 1)], dst_vmem.at[pl.ds(dst_row, 1)], sem)


def _moe_group_kernel(te_ref, tv_ref, tok_ref, x_hbm, nw_ref, wg_ref, wu_ref, wd_ref, o_ref, xbuf, h_scr, sem, *, tm):
    m = pl.program_id(0)
    f = pl.program_id(1)
    valid = tv_ref[m] > 0
    slot = m % 2

    def gather_tile(tile, into):
        base = tile * tm

        def start(j, carry):
            _row_copy(x_hbm, tok_ref[base + j], xbuf.at[into], j, sem.at[into]).start()
            return carry

        lax.fori_loop(0, tm, start, 0, unroll=8)

    @pl.when((f == 0) & (m == 0))
    def _():
        gather_tile(0, 0)

    nxt = jnp.minimum(m + 1, pl.num_programs(0) - 1)

    @pl.when((f == 0) & (nxt > m) & (tv_ref[nxt] > 0))
    def _():
        gather_tile(nxt, 1 - slot)

    @pl.when(valid & (f == 0))
    def _():
        pltpu.make_async_copy(x_hbm.at[pl.ds(0, tm)], xbuf.at[slot], sem.at[slot]).wait()
        h_scr[...] = _rms(xbuf[slot], nw_ref[...]).astype(BF16)

    @pl.when(valid)
    def _():
        h = h_scr[...]
        g = jnp.dot(h, wg_ref[...], preferred_element_type=F32)
        u = jnp.dot(h, wu_ref[...], preferred_element_type=F32)
        y = jnp.dot((g * jax.nn.sigmoid(g) * u).astype(BF16), wd_ref[...], preferred_element_type=F32)

        @pl.when(f == 0)
        def _():
            o_ref[...] = y

        @pl.when(f != 0)
        def _():
            o_ref[...] += y

    @pl.when(jnp.logical_not(valid) & (f == 0))
    def _():
        o_ref[...] = jnp.zeros_like(o_ref)


def moe_grouped_ffn(x, nw, tile_expert, tile_valid, src_tok, wg, wu, wd, *, tm, tf=1792):
    S, D = x.shape
    E, _, F = wg.shape
    n_tiles = tile_expert.shape[0]
    nf = F // tf

    def w_in(m, f, te, tv, tok):
        return (te[m], 0, jnp.where(tv[m] > 0, f, nf - 1))

    def w_out(m, f, te, tv, tok):
        return (te[m], jnp.where(tv[m] > 0, f, nf - 1), 0)

    grid_spec = pltpu.PrefetchScalarGridSpec(
        num_scalar_prefetch=3,
        grid=(n_tiles, nf),
        in_specs=[
            pl.BlockSpec(memory_space=pl.ANY),
            pl.BlockSpec((1, D), lambda m, f, te, tv, tok: (0, 0)),
            pl.BlockSpec((None, D, tf), w_in),
            pl.BlockSpec((None, D, tf), w_in),
            pl.BlockSpec((None, tf, D), w_out),
        ],
        out_specs=pl.BlockSpec((tm, D), lambda m, f, te, tv, tok: (m, 0)),
        scratch_shapes=[pltpu.VMEM((2, tm, D), F32), pltpu.VMEM((tm, D), BF16), pltpu.SemaphoreType.DMA((2,))],
    )
    return pl.pallas_call(
        functools.partial(_moe_group_kernel, tm=tm),
        grid_spec=grid_spec,
        out_shape=jax.ShapeDtypeStruct((n_tiles * tm, D), F32),
        compiler_params=_params("arbitrary", "arbitrary"),
        name="moe_grouped_ffn",
    )(tile_expert, tile_valid, src_tok, x, nw.reshape(1, D), wg, wu, wd)


def _moe_combine_kernel(pos_ref, x_ref, rt_ref, ys_hbm, o_ref, buf, sem, *, tc, n_tok):
    i = pl.program_id(0)
    slot = i % 2

    def gather_tile(tile, into):
        base = tile * tc

        def start(j, carry):
            for k in range(TOP_K):
                _row_copy(ys_hbm, pos_ref[k * n_tok + base + j], buf.at[into, k], j, sem.at[into]).start()
            return carry

        lax.fori_loop(0, tc, start, 0, unroll=8)

    @pl.when(i == 0)
    def _():
        gather_tile(0, 0)

    @pl.when(i + 1 < pl.num_programs(0))
    def _():
        gather_tile(i + 1, 1 - slot)

    for k in range(TOP_K):
        pltpu.make_async_copy(ys_hbm.at[pl.ds(0, tc)], buf.at[slot, k], sem.at[slot]).wait()
    rt = rt_ref[...]
    o_ref[...] = (x_ref[...] + rt[:, ROUTE_LANE_G1:ROUTE_LANE_G1 + 1] * buf[slot, 0]
                  + rt[:, ROUTE_LANE_G2:ROUTE_LANE_G2 + 1] * buf[slot, 1])


def moe_combine(x, route, ys, pos, *, tc=256):
    S, D = x.shape
    grid_spec = pltpu.PrefetchScalarGridSpec(
        num_scalar_prefetch=1,
        grid=(S // tc,),
        in_specs=[
            pl.BlockSpec((tc, D), lambda i, pos: (i, 0)),
            pl.BlockSpec((tc, 128), lambda i, pos: (i, 0)),
            pl.BlockSpec(memory_space=pl.ANY),
        ],
        out_specs=pl.BlockSpec((tc, D), lambda i, pos: (i, 0)),
        scratch_shapes=[pltpu.VMEM((2, TOP_K, tc, D), F32), pltpu.SemaphoreType.DMA((2,))],
    )
    return pl.pallas_call(
        functools.partial(_moe_combine_kernel, tc=tc, n_tok=S),
        grid_spec=grid_spec,
        out_shape=jax.ShapeDtypeStruct((S, D), F32),
        compiler_params=_params("arbitrary"),
        name="moe_combine",
    )(pos, x, route, ys)


def moe_top2(x, nw, route, wg, wu, wd, *, tm=512):
    S, D = x.shape
    E = wg.shape[0]
    experts = jnp.concatenate([route[:, ROUTE_LANE_I1], route[:, ROUTE_LANE_I2]]).astype(jnp.int32)
    onehot = (experts[:, None] == jnp.arange(E, dtype=jnp.int32)).astype(jnp.int32)
    csum = jnp.cumsum(onehot, axis=0)
    rank = jnp.sum(onehot * csum, axis=1) - 1
    padded = (csum[-1] + tm - 1) // tm * tm
    ends = jnp.cumsum(padded)
    pos = (jnp.sum(onehot * (ends - padded), axis=1) + rank).astype(jnp.int32)
    n_tiles = TOP_K * S // tm + E
    tokens = jnp.tile(jnp.arange(S, dtype=jnp.int32), TOP_K)
    src_tok = jnp.zeros((n_tiles * tm,), jnp.int32).at[pos].set(tokens)
    tile_start = jnp.arange(n_tiles, dtype=jnp.int32) * tm
    tile_expert = jnp.minimum(jnp.sum(tile_start[:, None] >= ends[None, :], axis=1), E - 1).astype(jnp.int32)
    tile_valid = (tile_start < ends[-1]).astype(jnp.int32)
    ys = moe_grouped_ffn(x, nw, tile_expert, tile_valid, src_tok, wg, wu, wd, tm=tm)
    return moe_combine(x, route, ys, pos)


ATTN_RADIUS = 64
ATTN_BQ = 128
ATTN_TILE = ATTN_BQ * max(d for _, d in DILATION_PATTERNS)
ATTN_HALO = ATTN_RADIUS * max(d for _, d in DILATION_PATTERNS)
assert all(w // (2 * d) == ATTN_RADIUS for w, d in DILATION_PATTERNS)


def _attn_kernel(q_ref, kp_ref, kc_ref, kn_ref, vp_ref, vc_ref, vn_ref, o_ref, kbuf, vbuf, m_s, l_s, a_s, *, n_tok):
    tile, bq, halo, rad = ATTN_TILE, ATTN_BQ, ATTN_HALO, ATTN_RADIUS
    kbuf[0:halo, :] = kp_ref[...]
    kbuf[halo:halo + tile, :] = kc_ref[...]
    kbuf[halo + tile:, :] = kn_ref[...]
    vbuf[0:halo, :] = vp_ref[...]
    vbuf[halo:halo + tile, :] = vc_ref[...]
    vbuf[halo + tile:, :] = vn_ref[...]

    t0 = pl.program_id(1) * tile
    qi = lax.broadcasted_iota(jnp.int32, (bq, bq + 2 * rad), 0)
    kj = lax.broadcasted_iota(jnp.int32, (bq, bq + 2 * rad), 1)
    band = (kj >= qi) & (kj <= qi + 2 * rad)
    lo_half = lax.broadcasted_iota(jnp.int32, (bq, 2 * HEAD_DIM), 1) < HEAD_DIM

    for n_branch, (_, d) in enumerate(DILATION_PATTERNS):
        span = bq * d

        def block(blk, carry, d=d, span=span, first=n_branch == 0):
            base = pl.multiple_of(blk * span, span)
            for r in range(d):
                stride = None if d == 1 else d
                q_rows = pl.ds(base + r, bq, stride=stride)
                k_rows = pl.ds(halo + base + (r - rad * d), bq + 2 * rad, stride=stride)
                q = q_ref[q_rows, :]
                kw = kbuf[k_rows, :].astype(BF16)
                vw = vbuf[k_rows, :].astype(BF16)
                tok = t0 + base + (r - rad * d) + d * kj
                mask = band & (tok >= 0) & (tok < n_tok)
                halves = []
                for own in (lo_half, jnp.logical_not(lo_half)):
                    sc = _bdot_nt(jnp.where(own, q, 0.0), kw)
                    sc = jnp.where(mask, sc, NEG_INF)
                    m_h = jnp.max(sc, axis=-1, keepdims=True)
                    p = jnp.exp(sc - m_h)
                    halves.append((m_h, jnp.sum(p, axis=-1, keepdims=True),
                                   jnp.dot(p.astype(BF16), vw, preferred_element_type=F32)))
                m_b, l_b, a_b = (jnp.where(lo_half, x0, x1) for x0, x1 in zip(*halves))
                if first:
                    m_n, l_n, a_n = m_b, l_b, a_b
                else:
                    m_o = m_s[q_rows, :]
                    m_n = jnp.maximum(m_o, m_b)
                    w_o = jnp.exp(m_o - m_n)
                    w_b = jnp.exp(m_b - m_n)
                    l_n = l_s[q_rows, :] * w_o + l_b * w_b
                    a_n = a_s[q_rows, :] * w_o + a_b * w_b
                m_s[q_rows, :] = m_n
                l_s[q_rows, :] = l_n
                a_s[q_rows, :] = a_n
            return carry

        lax.fori_loop(0, tile // span, block, 0)

    o_ref[...] = a_s[...] / l_s[...]


def dilated_attention(q, k, v):
    S, W = q.shape
    pair = 2 * HEAD_DIM
    per = ATTN_TILE // ATTN_HALO
    cur = pl.BlockSpec((ATTN_TILE, pair), lambda p, i: (i, p))
    prv = pl.BlockSpec((ATTN_HALO, pair), lambda p, i: (jnp.maximum(i * per - 1, 0), p))
    nxt = pl.BlockSpec((ATTN_HALO, pair), lambda p, i: (jnp.minimum((i + 1) * per, S // ATTN_HALO - 1), p))
    ext = ATTN_TILE + 2 * ATTN_HALO
    return pl.pallas_call(
        functools.partial(_attn_kernel, n_tok=S),
        grid=(W // pair, S // ATTN_TILE),
        in_specs=[cur, prv, cur, nxt, prv, cur, nxt],
        out_specs=cur,
        out_shape=jax.ShapeDtypeStruct((S, W), F32),
        scratch_shapes=[pltpu.VMEM((ext, pair), F32), pltpu.VMEM((ext, pair), F32)]
        + [pltpu.VMEM((ATTN_TILE, pair), F32)] * 3,
        compiler_params=_params("parallel", "arbitrary"),
        name="dilated_attention",
    )(q, k, k, k, v, v, v)


def _tri_masks(z):
    row = lax.broadcasted_iota(jnp.int32, (CHUNK, CHUNK), 0)
    col = lax.broadcasted_iota(jnp.int32, (CHUNK, CHUNK), 1)
    if z == 0:
        return col <= row, col < row
    return col >= row, col > row


SUB_CHUNKS = 2
STEP_ROWS = SUB_CHUNKS * CHUNK


def _sub_rows(z, j):
    i = j if z == 0 else SUB_CHUNKS - 1 - j
    return slice(i * CHUNK, (i + 1) * CHUNK)


def _neumann_solve(ns, xs):
    steps = CHUNK.bit_length() - 1
    for i in range(steps):
        xs = [x + _bdot(n, x) for n, x in zip(ns, xs)]
        if i + 1 < steps:
            ns = [_bdot(n, n) for n in ns]
    return xs


def _rwkv_chunk_kernel(*refs):
    ins = (refs[0:6], refs[6:12])
    y_refs = refs[12:14]
    state = refs[14]

    @pl.when(pl.program_id(0) == 0)
    def _():
        state[...] = jnp.zeros_like(state)

    chains = [(j, z, h) for j in range(SUB_CHUNKS) for z in range(2) for h in range(N_HEADS_B)]
    masks = [_tri_masks(z) for z in range(2)]
    prep = {}
    for j in range(SUB_CHUNKS):
        for z in range(2):
            rows = _sub_rows(z, j)
            r_ref, v_ref, a_ref, lw_ref, k_ref, b_ref = ins[z]
            tri = jnp.where(masks[z][0], 1.0, 0.0).astype(BF16)
            lw = lw_ref[rows, :]
            hi, mid, lo = _split3(lw)
            cum = (jnp.dot(tri, hi, preferred_element_type=F32) + jnp.dot(tri, mid, preferred_element_type=F32)
                   + jnp.dot(tri, lo, preferred_element_type=F32))
            tot = jnp.sum(lw, axis=0, keepdims=True)
            e_neg = jnp.exp(-cum)
            e_end = jnp.exp(tot - cum)
            k = k_ref[rows, :]
            b = b_ref[rows, :]
            prep[j, z] = dict(rt=r_ref[rows, :] * jnp.exp(cum), at=a_ref[rows, :] * jnp.exp(cum - lw), kt=k * e_neg,
                              bt=b * e_neg, kh=k * e_end, bh=b * e_end, dw=jnp.exp(tot), v=v_ref[rows, :])

    def part(name, c):
        j, z, h = c
        return prep[j, z][name][:, h * HEAD_DIM:(h + 1) * HEAD_DIM]

    xs = [_bdot_nt(jnp.concatenate([part("at", c), part("rt", c)], axis=0),
                   jnp.concatenate([part("bt", c), part("kt", c)], axis=0)) for c in chains]
    a_ab = [jnp.where(masks[c[1]][1], x[:CHUNK, :CHUNK], 0.0) for c, x in zip(chains, xs)]
    a_rb = [jnp.where(masks[c[1]][0], x[CHUNK:, :CHUNK], 0.0) for c, x in zip(chains, xs)]
    cys = [_bdot(jnp.concatenate([jnp.where(masks[c[1]][1], x[:CHUNK, CHUNK:], 0.0),
                                  jnp.where(masks[c[1]][0], x[CHUNK:, CHUNK:], 0.0)], axis=0), part("v", c))
           for c, x in zip(chains, xs)]
    pqs = _neumann_solve(a_ab, [jnp.concatenate([part("at", c), cy[:CHUNK]], axis=1) for c, cy in zip(chains, cys)])
    kvs = [_bdot_tn(part("v", c), part("kh", c)) for c in chains]
    per = 2 * N_HEADS_B
    s_cur = [state[z, h] for _, z, h in chains[:per]]
    for j in range(SUB_CHUNKS):
        sel = slice(j * per, (j + 1) * per)
        cs = chains[sel]
        us = [_bdot_nt(pq[:, :HEAD_DIM], s) + pq[:, HEAD_DIM:] for pq, s in zip(pqs[sel], s_cur)]
        ys = [_bdot_nt(part("rt", c), s) for c, s in zip(cs, s_cur)]
        ys = [y + _bdot(arb, u) + cy[CHUNK:] for y, arb, u, cy in zip(ys, a_rb[sel], us, cys[sel])]
        s_cur = [s * part("dw", c) + _bdot_tn(u, part("bh", c)) + kv for c, s, u, kv in zip(cs, s_cur, us, kvs[sel])]
        for (_, z, h), y in zip(cs, ys):
            y_refs[z][_sub_rows(z, j), h * HEAD_DIM:(h + 1) * HEAD_DIM] = y
    for (_, z, h), s in zip(chains[:per], s_cur):
        state[z, h] = s


def rwkv7_scan(r, v, a, lw, k, b):
    S, C = r.shape
    n = S // STEP_ROWS
    fwd = pl.BlockSpec((STEP_ROWS, C), lambda c: (c, 0))
    bwd = pl.BlockSpec((STEP_ROWS, C), lambda c: (n - 1 - c, 0))
    y0, y1 = pl.pallas_call(
        _rwkv_chunk_kernel,
        grid=(n,),
        in_specs=[fwd] * 6 + [bwd] * 6,
        out_specs=[fwd, bwd],
        out_shape=[jax.ShapeDtypeStruct((S, C), F32)] * 2,
        scratch_shapes=[pltpu.VMEM((2, N_HEADS_B, HEAD_DIM, HEAD_DIM), F32)],
        compiler_params=_params("arbitrary"),
        name="rwkv7_scan",
    )(r, v, a, lw[0], k[0], b[0], r, v, a, lw[1], k[1], b[1])
    return y0, y1


def _dn_chunk_kernel(*refs):
    ins = (refs[0:6], refs[6:12])
    o_refs = refs[12:14]
    state = refs[14]

    @pl.when(pl.program_id(0) == 0)
    def _():
        state[...] = jnp.zeros_like(state)

    chains = [(j, z, h) for j in range(SUB_CHUNKS) for z in range(2) for h in range(N_HEADS_D)]
    masks = [_tri_masks(z) for z in range(2)]
    nt = (((1,), (1,)), ((), ()))
    gcs, decays, betas, g_lasts = [], [], [], []
    for j in range(SUB_CHUNKS):
        for z in range(2):
            rows = _sub_rows(z, j)
            _, _, _, bcol_ref, gcol_ref, grow_ref = ins[z]
            incl = masks[z][0]
            tri = jnp.where(incl, 1.0, 0.0).astype(BF16)
            c_hi, c_mid, c_lo = _split3(gcol_ref[rows, :])
            gc_cols = (jnp.dot(tri, c_hi, preferred_element_type=F32) + jnp.dot(tri, c_mid, preferred_element_type=F32)
                       + jnp.dot(tri, c_lo, preferred_element_type=F32))
            r_hi, r_mid, r_lo = _split3(grow_ref[rows.start // CHUNK])
            gc_rows = (lax.dot_general(r_hi, tri, nt, preferred_element_type=F32)
                       + lax.dot_general(r_mid, tri, nt, preferred_element_type=F32)
                       + lax.dot_general(r_lo, tri, nt, preferred_element_type=F32))
            last = CHUNK - 1 if z == 0 else 0
            bcol = bcol_ref[rows, :]
            for h in range(N_HEADS_D):
                idx = z * N_HEADS_D + h
                gc = gc_cols[:, idx:idx + 1]
                diff = gc - gc_rows[idx:idx + 1, :]
                gcs.append(gc)
                decays.append(jnp.where(incl, jnp.exp(jnp.where(incl, diff, 0.0)), 0.0))
                betas.append(bcol[:, idx:idx + 1])
                g_lasts.append(gc[last:last + 1, :])

    def part(i, c):
        j, z, h = c
        return ins[z][i][_sub_rows(z, j), h * HEAD_DIM_D:(h + 1) * HEAD_DIM_D]

    qs = [part(0, c) for c in chains]
    ks = [part(1, c) for c in chains]
    vs = [part(2, c) for c in chains]
    kbs = [k * beta for k, beta in zip(ks, betas)]
    e_gcs = [jnp.exp(gc) for gc in gcs]
    kqs = [_bdot_nt(jnp.concatenate([kb, q], axis=0), k) for kb, q, k in zip(kbs, qs, ks)]
    n_mats = [jnp.where(masks[c[1]][1], -(kq[:CHUNK] * dc), 0.0) for c, kq, dc in zip(chains, kqs, decays)]
    attns = [kq[CHUNK:] * dc for kq, dc in zip(kqs, decays)]
    uks = _neumann_solve(n_mats, [jnp.concatenate([v * beta, kb * e], axis=1)
                                  for v, beta, kb, e in zip(vs, betas, kbs, e_gcs)])
    per = 2 * N_HEADS_D
    s_cur = [state[z, h] for _, z, h in chains[:per]]
    for j in range(SUB_CHUNKS):
        sel = slice(j * per, (j + 1) * per)
        us = [uk[:, :HEAD_DIM_D] - _bdot(uk[:, HEAD_DIM_D:], s) for uk, s in zip(uks[sel], s_cur)]
        os_ = [_bdot(q * e, s) for q, e, s in zip(qs[sel], e_gcs[sel], s_cur)]
        os_ = [o + _bdot(attn, u) for o, attn, u in zip(os_, attns[sel], us)]
        s_cur = [s * jnp.exp(gl) + _bdot_tn(k * jnp.exp(gl - gc), u)
                 for s, gl, k, gc, u in zip(s_cur, g_lasts[sel], ks[sel], gcs[sel], us)]
        for (_, z, h), o in zip(chains[sel], os_):
            o_refs[z][_sub_rows(z, j), h * HEAD_DIM_D:(h + 1) * HEAD_DIM_D] = o
    for (_, z, h), s in zip(chains[:per], s_cur):
        state[z, h] = s


def deltanet_scan(q, k, v, beta, g):
    S, C = q.shape
    n = S // STEP_ROWS
    nz = 2 * N_HEADS_D
    g_rows = g.reshape(S // CHUNK, CHUNK, nz).transpose(0, 2, 1)

    def specs(idx):
        wide = pl.BlockSpec((STEP_ROWS, C), lambda c: (idx(c), 0))
        col = pl.BlockSpec((STEP_ROWS, nz), lambda c: (idx(c), 0))
        row = pl.BlockSpec((SUB_CHUNKS, nz, CHUNK), lambda c: (idx(c), 0, 0))
        return [wide, wide, wide, col, col, row], wide

    in_f, out_f = specs(lambda c: c)
    in_b, out_b = specs(lambda c: n - 1 - c)
    o0, o1 = pl.pallas_call(
        _dn_chunk_kernel,
        grid=(n,),
        in_specs=in_f + in_b,
        out_specs=[out_f, out_b],
        out_shape=[jax.ShapeDtypeStruct((S, C), F32)] * 2,
        scratch_shapes=[pltpu.VMEM((2, N_HEADS_D, HEAD_DIM_D, HEAD_DIM_D), F32)],
        compiler_params=_params("arbitrary"),
        name="deltanet_scan",
    )(q, k, v, beta, g, g_rows, q, k, v, beta, g, g_rows)
    return o0, o1


def _rope_tables(positions):
    S = positions.shape[0]
    half = ROPE_DIM // 2
    inv_freq = jnp.power(ROPE_THETA, -jnp.arange(half, dtype=F32) / half)
    ang = positions[:, None].astype(F32) * inv_freq
    cos, sin = jnp.cos(ang), jnp.sin(ang)
    rest = HEAD_DIM - ROPE_DIM
    cos_t = jnp.concatenate([cos, cos, jnp.ones((S, rest), F32)], axis=1)
    sin_t = jnp.concatenate([-sin, sin, jnp.zeros((S, rest), F32)], axis=1)
    return jnp.tile(cos_t, (1, N_HEADS_A)), jnp.tile(sin_t, (1, N_HEADS_A))


def _ones_block_diag(width, group):
    idx = jnp.arange(width) // group
    return (idx[:, None] == idx[None, :]).astype(BF16)


def _block_diag(blocks):
    rows = sum(b.shape[0] for b in blocks)
    cols = sum(b.shape[1] for b in blocks)
    out = jnp.zeros((rows, cols), blocks[0].dtype)
    r = c = 0
    for b in blocks:
        out = lax.dynamic_update_slice(out, b, (r, c))
        r += b.shape[0]
        c += b.shape[1]
    return out


def _even_layer(x, positions, mix_norm, w_in, q_norm, k_norm, shift_mu, lora_mu, w0, w1, w2, a0, a1, a2, g1, g2,
                k_k, k_a, r_k, ln_w, ln_b, w_out, ffn_norm, ffn_gate, ffn_up, ffn_down):
    row = lambda t: t.reshape(1, -1)
    cos_t, sin_t = _rope_tables(positions)
    ones_bd = _ones_block_diag(WIDTH_B, HEAD_DIM)
    lora_in = jnp.concatenate([w1[0], w1[1], a1[0], a1[1], g1], axis=1).astype(BF16)
    lora_out = _block_diag([w2[0], w2[1], a2[0], a2[1], g2]).astype(BF16)
    (q, k, v, r, vb, a_vec, lw0, lw1, k0, k1, b0, b1, gate, bonus) = even_prep(
        x, mix_norm, w_in.astype(BF16), row(jnp.tile(q_norm, N_HEADS_A)), row(jnp.tile(k_norm, N_HEADS_A)), cos_t, sin_t,
        shift_mu, lora_mu, lora_in, lora_out, w0, a0, row(k_k), row(k_a), row(r_k), ones_bd)
    y_a = dilated_attention(q, k, v)
    y0, y1 = rwkv7_scan(r, vb, a_vec, (lw0, lw1), (k0, k1), (b0, b1))
    return even_post_ffn(x, y_a, y0, y1, gate, bonus, row(ln_w), row(ln_b), ones_bd, w_out.astype(BF16), ffn_norm,
                         ffn_gate.astype(BF16), ffn_up.astype(BF16), ffn_down.astype(BF16))


def _odd_layer(x, mix_norm, w_in, conv_c, conv_dn, A_log, dt_bias, dn_norm, w_out, ffn_norm, router, moe_gate, moe_up, moe_down):
    n_in = w_in.shape[1]
    n_pad = -(-n_in // 128) * 128
    w_in_p = jnp.pad(w_in, ((0, 0), (0, n_pad - n_in))).astype(BF16)
    nz = 2 * N_HEADS_D
    neg_a = jnp.zeros((1, 128), F32).at[0, nz:2 * nz].set(-jnp.exp(A_log.reshape(-1)))
    dt_b = jnp.zeros((1, 128), F32).at[0, nz:2 * nz].set(dt_bias.reshape(-1))
    ones_bd = _ones_block_diag(WIDTH_D, HEAD_DIM_D)
    y_c, q, k, v, zs, bg = odd_prep(x, mix_norm, w_in_p, conv_c, conv_dn, neg_a, dt_b, ones_bd)
    o0, o1 = deltanet_scan(q, k, v, bg[:, :nz], bg[:, nz:2 * nz])
    wr_pad = jnp.pad(router, ((0, 0), (0, 128 - N_EXPERTS)))
    x, route = odd_post(x, y_c, o0, o1, zs, jnp.tile(dn_norm, N_HEADS_D).reshape(1, -1), ones_bd, w_out.astype(BF16),
                        ffn_norm, wr_pad)
    return moe_top2(x, ffn_norm, route, moe_gate.astype(BF16), moe_up.astype(BF16), moe_down.astype(BF16))


def kernel(x, positions, ev_mix_norm, ev_w_in, ev_q_norm, ev_k_norm, ev_shift_mu, ev_lora_mu, ev_w0, ev_w1, ev_w2, ev_a0, ev_a1, ev_a2, ev_g1, ev_g2, ev_k_k, ev_k_a, ev_r_k, ev_ln_w, ev_ln_b, ev_w_out, ev_ffn_norm, ev_ffn_gate, ev_ffn_up, ev_ffn_down, od_mix_norm, od_w_in, od_conv_c, od_conv_dn, od_A_log, od_dt_bias, od_dn_norm, od_w_out, od_ffn_norm, od_router, od_moe_gate, od_moe_up, od_moe_down):
    B, S, D = x.shape
    assert B == 1
    xs = x.reshape(S, D)
    pos = positions.reshape(S)
    n_layers = ev_mix_norm.shape[0] + od_mix_norm.shape[0]
    for layer in range(n_layers):
        i = layer // 2
        if layer % 2 == 0:
            xs = _even_layer(xs, pos, ev_mix_norm[i], ev_w_in[i], ev_q_norm[i], ev_k_norm[i], ev_shift_mu[i], ev_lora_mu[i],
                             ev_w0[i], ev_w1[i], ev_w2[i], ev_a0[i], ev_a1[i], ev_a2[i], ev_g1[i], ev_g2[i], ev_k_k[i],
                             ev_k_a[i], ev_r_k[i], ev_ln_w[i], ev_ln_b[i], ev_w_out[i], ev_ffn_norm[i], ev_ffn_gate[i],
                             ev_ffn_up[i], ev_ffn_down[i])
        else:
            xs = _odd_layer(xs, od_mix_norm[i], od_w_in[i], od_conv_c[i], od_conv_dn[i], od_A_log[i], od_dt_bias[i],
                            od_dn_norm[i], od_w_out[i], od_ffn_norm[i], od_router[i], od_moe_gate[i], od_moe_up[i],
                            od_moe_down[i])
    return xs.reshape(B, S, D)
```

```python
import functools

import jax
import jax.numpy as jnp
from jax import lax
from jax.experimental import pallas as pl
from jax.experimental.pallas import tpu as pltpu

F32 = jnp.float32
BF16 = jnp.bfloat16

HEAD_DIM = 64
N_HEADS_A = 8
WIDTH_A = N_HEADS_A * HEAD_DIM
DILATION_PATTERNS = ((128, 1), (512, 4), (2048, 16))
ROPE_DIM = HEAD_DIM // 4
ROPE_THETA = 500000.0
N_HEADS_B = 8
WIDTH_B = N_HEADS_B * HEAD_DIM
RWKV_LN_EPS = 64e-5
WIDTH_C = 512
N_HEADS_D = 4
HEAD_DIM_D = 128
WIDTH_D = N_HEADS_D * HEAD_DIM_D
CHUNK = 64
N_EXPERTS = 8
TOP_K = 2
ROUTE_LANE_I1, ROUTE_LANE_I2, ROUTE_LANE_G1, ROUTE_LANE_G2 = 8, 9, 10, 11
NORM_EPS = 1e-6
NEG_INF = -1e30

V7X_VMEM_LIMIT_BYTES = 56 * 1024 * 1024


def _params(*sem):
    return pltpu.CompilerParams(dimension_semantics=sem, vmem_limit_bytes=V7X_VMEM_LIMIT_BYTES)


def _bdot(a, b):
    return jnp.dot(a.astype(BF16), b.astype(BF16), preferred_element_type=F32)


def _bdot_nt(a, b):
    return lax.dot_general(a.astype(BF16), b.astype(BF16), (((1,), (1,)), ((), ())), preferred_element_type=F32)


def _bdot_tn(a, b):
    return lax.dot_general(a.astype(BF16), b.astype(BF16), (((0,), (0,)), ((), ())), preferred_element_type=F32)


def _split3(x):
    hi = x.astype(BF16)
    r1 = x - hi.astype(F32)
    mid = r1.astype(BF16)
    lo = (r1 - mid.astype(F32)).astype(BF16)
    return hi, mid, lo


def _rms(x, w):
    return x * lax.rsqrt(jnp.mean(x * x, axis=-1, keepdims=True) + NORM_EPS) * w


HALO = 8


def _group_sum(x, ones_bd):
    hi = x.astype(BF16)
    lo = (x - hi.astype(F32)).astype(BF16)
    return jnp.dot(hi, ones_bd, preferred_element_type=F32) + jnp.dot(lo, ones_bd, preferred_element_type=F32)


def _silu(x):
    return x * jax.nn.sigmoid(x)


def _halo_specs(tm, n_rows, width):
    per = tm // HALO
    cur = pl.BlockSpec((tm, width), lambda i: (i, 0))
    prv = pl.BlockSpec((HALO, width), lambda i: (jnp.maximum(i * per - 1, 0), 0))
    nxt = pl.BlockSpec((HALO, width), lambda i: (jnp.minimum((i + 1) * per, n_rows // HALO - 1), 0))
    return cur, prv, nxt


def _make_shifts(tm, n_rows):
    row = pl.program_id(0) * tm + lax.broadcasted_iota(jnp.int32, (tm, 1), 0)
    first, last = row == 0, row == n_rows - 1
    n_ext = tm + 2 * HALO

    def shifts(t):
        prev = jnp.where(first, 0.0, pltpu.roll(t, 1, 0)[HALO:HALO + tm])
        nxt = jnp.where(last, 0.0, pltpu.roll(t, n_ext - 1, 0)[HALO:HALO + tm])
        return prev, t[HALO:HALO + tm], nxt

    return shifts


def _even_prep_kernel(xc_ref, xp_ref, xn_ref, nw_ref, win_ref, qn_ref, kn_ref, cos_ref, sin_ref, smu_ref, lmu_ref,
                      lin_ref, lout_ref, w0_ref, a0_ref, kk_ref, ka_ref, rk_ref, ones_ref,
                      q_out, k_out, v_out, r_out, vb_out, a_out, lw0_out, lw1_out, k0_out, k1_out, b0_out, b1_out,
                      gate_out, bonus_out, *, tm, n_rows):
    shifts = _make_shifts(tm, n_rows)
    he = _rms(jnp.concatenate([xp_ref[...], xc_ref[...], xn_ref[...]], axis=0), nw_ref[...])
    proj = jnp.dot(he.astype(BF16), win_ref[...], preferred_element_type=F32)
    ones = ones_ref[...]
    lane = lax.broadcasted_iota(jnp.int32, (tm, WIDTH_A), 1) % HEAD_DIM

    def head_rms_rope(t, w):
        t = t * lax.rsqrt(_group_sum(t * t, ones) * (1.0 / HEAD_DIM) + NORM_EPS) * w
        half = ROPE_DIM // 2
        swapped = jnp.where(lane < half, pltpu.roll(t, WIDTH_A - half, 1), pltpu.roll(t, half, 1))
        return t * cos_ref[...] + swapped * sin_ref[...]

    cur = proj[HALO:HALO + tm]
    q = head_rms_rope(cur[:, :WIDTH_A], qn_ref[...]) * HEAD_DIM ** -0.5
    k = head_rms_rope(cur[:, WIDTH_A:2 * WIDTH_A], kn_ref[...])
    v = cur[:, 2 * WIDTH_A:3 * WIDTH_A]
    q_out[...] = q
    k_out[...] = k
    v_out[...] = v

    p_prev, p_cur, p_next = shifts(proj[:, 3 * WIDTH_A:])
    smu = smu_ref[...]
    rkv = p_cur + smu[0:1] * (p_prev - p_cur) + smu[1:2] * (p_next - p_cur)
    r, kin, vb = rkv[:, :WIDTH_B], rkv[:, WIDTH_B:2 * WIDTH_B], rkv[:, 2 * WIDTH_B:]
    h_prev, h_cur, h_next = shifts(he)
    lmu = lmu_ref[...]
    hx = h_cur + lmu[0:1] * (h_prev - h_cur) + lmu[1:2] * (h_next - h_cur)
    l1 = jnp.dot(hx.astype(BF16), lin_ref[...], preferred_element_type=F32)
    l1 = jnp.concatenate([jnp.tanh(l1[:, :128]), l1[:, 128:256], jax.nn.sigmoid(l1[:, 256:])], axis=1)
    l2 = jnp.dot(l1.astype(BF16), lout_ref[...], preferred_element_type=F32)
    w0, a0 = w0_ref[...], a0_ref[...]
    kk = kin * kk_ref[...]
    kk = kk * lax.rsqrt(_group_sum(kk * kk, ones) + 1e-6)
    kdirs = []
    for z, (lw_out, k_out_z, b_out_z) in enumerate(((lw0_out, k0_out, b0_out), (lw1_out, k1_out, b1_out))):
        w_pre = l2[:, z * WIDTH_B:(z + 1) * WIDTH_B] + w0[z:z + 1]
        lw_out[...] = -jnp.exp(-0.5) * jax.nn.sigmoid(w_pre)
        iclr = jax.nn.sigmoid(l2[:, (2 + z) * WIDTH_B:(3 + z) * WIDTH_B] + a0[z:z + 1])
        kdir = kin * (1.0 + (iclr - 1.0) * ka_ref[...])
        k_out_z[...] = kdir
        b_out_z[...] = kk * iclr
        kdirs.append(kdir)
    r_out[...] = r
    vb_out[...] = vb
    a_out[...] = -kk
    gate_out[...] = l2[:, 4 * WIDTH_B:]
    bonus_out[...] = _group_sum(r * (kdirs[0] + kdirs[1]) * rk_ref[...], ones) * vb


def even_prep(x, mix_norm, w_in, q_norm, k_norm, cos_t, sin_t, shift_mu, lora_mu, lora_in, lora_out, w0, a0, k_k, k_a,
              r_k, ones_bd, *, tm=256):
    S, D = x.shape
    cur, prv, nxt = _halo_specs(tm, S, D)
    full = lambda a: pl.BlockSpec(a.shape, lambda i: (0,) * a.ndim)
    rows = pl.BlockSpec((tm, WIDTH_B), lambda i: (i, 0))
    consts = [mix_norm.reshape(1, D), w_in, q_norm, k_norm]
    consts2 = [shift_mu, lora_mu, lora_in, lora_out, w0, a0, k_k, k_a, r_k, ones_bd]
    return pl.pallas_call(
        functools.partial(_even_prep_kernel, tm=tm, n_rows=S),
        grid=(S // tm,),
        in_specs=[cur, prv, nxt] + [full(a) for a in consts] + [rows, rows] + [full(a) for a in consts2],
        out_specs=[rows] * 14,
        out_shape=[jax.ShapeDtypeStruct((S, WIDTH_B), F32)] * 14,
        compiler_params=_params("parallel"),
        name="even_prep",
    )(x, x, x, *consts, cos_t, sin_t, *consts2)


def _odd_prep_kernel(xc_ref, xp_ref, xn_ref, nw_ref, win_ref, cc_ref, cdn_ref, nega_ref, dtb_ref, ones_ref,
                     yc_out, q_out, k_out, v_out, zs_out, bg_out, *, tm, n_rows):
    shifts = _make_shifts(tm, n_rows)
    he = _rms(jnp.concatenate([xp_ref[...], xc_ref[...], xn_ref[...]], axis=0), nw_ref[...])
    proj = jnp.dot(he.astype(BF16), win_ref[...], preferred_element_type=F32)
    cur = proj[HALO:HALO + tm]

    def conv3(t, w):
        prev, mid, nxt = shifts(t)
        return w[0:1] * prev + w[1:2] * mid + w[2:3] * nxt

    o_dn = 3 * WIDTH_C
    yc_out[...] = cur[:, :WIDTH_C] * conv3(proj[:, WIDTH_C:2 * WIDTH_C] * proj[:, 2 * WIDTH_C:o_dn], cc_ref[...])
    qkv = _silu(conv3(proj[:, o_dn:o_dn + 3 * WIDTH_D], cdn_ref[...]))
    ones = ones_ref[...]
    l2n = lambda t: t * lax.rsqrt(_group_sum(t * t, ones) + 1e-6)
    q_out[...] = l2n(qkv[:, :WIDTH_D]) * HEAD_DIM_D ** -0.5
    k_out[...] = l2n(qkv[:, WIDTH_D:2 * WIDTH_D])
    v_out[...] = qkv[:, 2 * WIDTH_D:]
    zs_out[...] = _silu(cur[:, o_dn + 3 * WIDTH_D:o_dn + 4 * WIDTH_D])
    tail = cur[:, o_dn + 4 * WIDTH_D:]
    t = tail + dtb_ref[...]
    softplus = jnp.maximum(t, 0.0) + jnp.log(1.0 + jnp.exp(-jnp.abs(t)))
    lane = lax.broadcasted_iota(jnp.int32, tail.shape, 1)
    bg_out[...] = jnp.where(lane < 2 * N_HEADS_D, jax.nn.sigmoid(tail), nega_ref[...] * softplus)


def odd_prep(x, mix_norm, w_in_pad, conv_c, conv_dn, neg_a, dt_b, ones_bd, *, tm=256):
    S, D = x.shape
    cur, prv, nxt = _halo_specs(tm, S, D)
    full = lambda a: pl.BlockSpec(a.shape, lambda i: (0,) * a.ndim)
    rows = pl.BlockSpec((tm, WIDTH_D), lambda i: (i, 0))
    consts = [mix_norm.reshape(1, D), w_in_pad, conv_c, conv_dn, neg_a, dt_b, ones_bd]
    return pl.pallas_call(
        functools.partial(_odd_prep_kernel, tm=tm, n_rows=S),
        grid=(S // tm,),
        in_specs=[cur, prv, nxt] + [full(a) for a in consts],
        out_specs=[rows] * 5 + [pl.BlockSpec((tm, 128), lambda i: (i, 0))],
        out_shape=[jax.ShapeDtypeStruct((S, WIDTH_D), F32)] * 5 + [jax.ShapeDtypeStruct((S, 128), F32)],
        compiler_params=_params("parallel"),
        name="odd_prep",
    )(x, x, x, *consts)


def _even_post_ffn_kernel(x_ref, ya_ref, y0_ref, y1_ref, gate_ref, bonus_ref, lnw_ref, lnb_ref, ones_ref, wo_ref,
                          fnw_ref, wg_ref, wu_ref, wd_ref, o_ref, h_scr):
    @pl.when(pl.program_id(1) == 0)
    def _():
        ones = ones_ref[...]
        yf = y0_ref[...] + y1_ref[...]
        dev = yf - _group_sum(yf, ones) * (1.0 / HEAD_DIM)
        var = _group_sum(dev * dev, ones) * (1.0 / HEAD_DIM)
        yn = dev * lax.rsqrt(var + RWKV_LN_EPS) * lnw_ref[...] + lnb_ref[...]
        y_b = (yn + bonus_ref[...]) * gate_ref[...]
        y = jnp.concatenate([ya_ref[...], y_b], axis=1)
        x = x_ref[...] + jnp.dot(y.astype(BF16), wo_ref[...], preferred_element_type=F32)
        h_scr[...] = _rms(x, fnw_ref[...]).astype(BF16)
        o_ref[...] = x

    h = h_scr[...]
    g = jnp.dot(h, wg_ref[...], preferred_element_type=F32)
    u = jnp.dot(h, wu_ref[...], preferred_element_type=F32)
    o_ref[...] += jnp.dot((_silu(g) * u).astype(BF16), wd_ref[...], preferred_element_type=F32)


def even_post_ffn(x, y_a, y0, y1, gate, bonus, ln_w, ln_b, ones_bd, w_out, ffn_norm, wg, wu, wd, *, tm=512, tf=1408):
    S, D = x.shape
    F = wg.shape[1]
    rows = pl.BlockSpec((tm, WIDTH_B), lambda i, f: (i, 0))
    full = lambda a: pl.BlockSpec(a.shape, lambda i, f: (0,) * a.ndim)
    consts = [ln_w, ln_b, ones_bd, w_out, ffn_norm.reshape(1, D)]
    return pl.pallas_call(
        _even_post_ffn_kernel,
        grid=(S // tm, F // tf),
        in_specs=[pl.BlockSpec((tm, D), lambda i, f: (i, 0)),
                  rows, rows, rows, rows, rows] + [full(a) for a in consts] + [
            pl.BlockSpec((D, tf), lambda i, f: (0, f)),
            pl.BlockSpec((D, tf), lambda i, f: (0, f)),
            pl.BlockSpec((tf, D), lambda i, f: (f, 0)),
        ],
        out_specs=pl.BlockSpec((tm, D), lambda i, f: (i, 0)),
        out_shape=jax.ShapeDtypeStruct((S, D), F32),
        scratch_shapes=[pltpu.VMEM((tm, D), BF16)],
        compiler_params=_params("parallel", "arbitrary"),
        name="even_post_ffn",
    )(x, y_a, y0, y1, gate, bonus, *consts, wg, wu, wd)


def _route_record(h, wr):
    h_hi, w_hi = h.astype(BF16), wr.astype(BF16)
    h_lo, w_lo = (h - h_hi.astype(F32)).astype(BF16), (wr - w_hi.astype(F32)).astype(BF16)
    logits = (jnp.dot(h_hi, w_hi, preferred_element_type=F32) + jnp.dot(h_lo, w_hi, preferred_element_type=F32)
              + jnp.dot(h_hi, w_lo, preferred_element_type=F32))
    lane = lax.broadcasted_iota(jnp.int32, logits.shape, 1)
    valid = lane < N_EXPERTS
    lg = jnp.where(valid, logits, NEG_INF)
    e = jnp.exp(lg - jnp.max(lg, axis=-1, keepdims=True))
    p = e / jnp.sum(e, axis=-1, keepdims=True)
    pm = jnp.where(valid, p, -1.0)
    m1 = jnp.max(pm, axis=-1, keepdims=True)
    i1 = jnp.min(jnp.where(pm == m1, lane, 128), axis=-1, keepdims=True)
    pm2 = jnp.where(lane == i1, -1.0, pm)
    m2 = jnp.max(pm2, axis=-1, keepdims=True)
    i2 = jnp.min(jnp.where(pm2 == m2, lane, 128), axis=-1, keepdims=True)
    tot = m1 + m2
    g1, g2 = m1 / tot, m2 / tot
    out = jnp.where(lane == i1, g1, 0.0) + jnp.where(lane == i2, g2, 0.0)
    out = jnp.where(lane == ROUTE_LANE_I1, i1.astype(F32), out)
    out = jnp.where(lane == ROUTE_LANE_I2, i2.astype(F32), out)
    out = jnp.where(lane == ROUTE_LANE_G1, g1, out)
    return jnp.where(lane == ROUTE_LANE_G2, g2, out)


def _odd_post_kernel(x_ref, yc_ref, o0_ref, o1_ref, zs_ref, dnw_ref, ones_ref, wo_ref, fnw_ref, wr_ref, x_out, route_out):
    o = o0_ref[...] + o1_ref[...]
    ms = _group_sum(o * o, ones_ref[...]) * (1.0 / HEAD_DIM_D)
    y_d = o * lax.rsqrt(ms + NORM_EPS) * dnw_ref[...] * zs_ref[...]
    y = jnp.concatenate([yc_ref[...], y_d], axis=1)
    x = x_ref[...] + jnp.dot(y.astype(BF16), wo_ref[...], preferred_element_type=F32)
    x_out[...] = x
    route_out[...] = _route_record(_rms(x, fnw_ref[...]), wr_ref[...])


def odd_post(x, y_c, o0, o1, zs, dn_norm, ones_bd, w_out, ffn_norm, wr_pad, *, tm=512):
    S, D = x.shape
    rows = pl.BlockSpec((tm, WIDTH_D), lambda i: (i, 0))
    full = lambda a: pl.BlockSpec(a.shape, lambda i: (0,) * a.ndim)
    consts = [dn_norm, ones_bd, w_out, ffn_norm.reshape(1, D), wr_pad]
    return pl.pallas_call(
        _odd_post_kernel,
        grid=(S // tm,),
        in_specs=[pl.BlockSpec((tm, D), lambda i: (i, 0)), rows, rows, rows, rows] + [full(a) for a in consts],
        out_specs=[pl.BlockSpec((tm, D), lambda i: (i, 0)), pl.BlockSpec((tm, 128), lambda i: (i, 0))],
        out_shape=[jax.ShapeDtypeStruct((S, D), F32), jax.ShapeDtypeStruct((S, 128), F32)],
        compiler_params=_params("parallel"),
        name="odd_post",
    )(x, y_c, o0, o1, zs, *consts)


def _row_copy(src_hbm, src_row, dst_vmem, dst_row, sem):
    return pltpu.make_async_copy(src_hbm.at[pl.ds(src_row, 1)], dst_vmem.at[pl.ds(dst_row, 1)], sem)


def _moe_group_kernel(te_ref, tv_ref, tok_ref, x_hbm, nw_ref, wg_ref, wu_ref, wd_ref, o_ref, xbuf, h_scr, sem, *, tm, n_f):
    m = pl.program_id(0)
    f = pl.program_id(1)
    valid = tv_ref[m] > 0
    slot = m % 2
    part = tm // n_f

    @pl.when((f == 0) & (m == 0))
    def _():
        def start(j, carry):
            _row_copy(x_hbm, tok_ref[j], xbuf.at[0], j, sem.at[0]).start()
            return carry

        lax.fori_loop(0, tm, start, 0, unroll=8)

    fed = valid | ((m > 0) & (tv_ref[jnp.maximum(m - 1, 0)] > 0))

    @pl.when(fed & (f == 0))
    def _():
        pltpu.make_async_copy(x_hbm.at[pl.ds(0, tm)], xbuf.at[slot], sem.at[slot]).wait()

    @pl.when(valid & (f == 0))
    def _():
        h_scr[...] = _rms(xbuf[slot], nw_ref[...]).astype(BF16)

    @pl.when(valid)
    def _():
        nbase = (m + 1) * tm + f * part
        for j in range(part):
            _row_copy(x_hbm, tok_ref[nbase + j], xbuf.at[1 - slot], f * part + j, sem.at[1 - slot]).start()
        h = h_scr[...]
        g = jnp.dot(h, wg_ref[...], preferred_element_type=F32)
        u = jnp.dot(h, wu_ref[...], preferred_element_type=F32)
        y = jnp.dot((g * jax.nn.sigmoid(g) * u).astype(BF16), wd_ref[...], preferred_element_type=F32)

        @pl.when(f == 0)
        def _():
            o_ref[...] = y

        @pl.when(f != 0)
        def _():
            o_ref[...] += y

    @pl.when(jnp.logical_not(valid) & (f == 0))
    def _():
        o_ref[...] = jnp.zeros_like(o_ref)


def moe_grouped_ffn(x, nw, tile_expert, tile_valid, src_tok, wg, wu, wd, *, tm, tf=1792):
    S, D = x.shape
    E, _, F = wg.shape
    n_tiles = tile_expert.shape[0]
    nf = F // tf

    def w_in(m, f, te, tv, tok):
        return (te[m], 0, jnp.where(tv[m] > 0, f, nf - 1))

    def w_out(m, f, te, tv, tok):
        return (te[m], jnp.where(tv[m] > 0, f, nf - 1), 0)

    grid_spec = pltpu.PrefetchScalarGridSpec(
        num_scalar_prefetch=3,
        grid=(n_tiles, nf),
        in_specs=[
            pl.BlockSpec(memory_space=pl.ANY),
            pl.BlockSpec((1, D), lambda m, f, te, tv, tok: (0, 0)),
            pl.BlockSpec((None, D, tf), w_in),
            pl.BlockSpec((None, D, tf), w_in),
            pl.BlockSpec((None, tf, D), w_out),
        ],
        out_specs=pl.BlockSpec((tm, D), lambda m, f, te, tv, tok: (m, 0)),
        scratch_shapes=[pltpu.VMEM((2, tm, D), F32), pltpu.VMEM((tm, D), BF16), pltpu.SemaphoreType.DMA((2,))],
    )
    return pl.pallas_call(
        functools.partial(_moe_group_kernel, tm=tm, n_f=nf),
        grid_spec=grid_spec,
        out_shape=jax.ShapeDtypeStruct((n_tiles * tm, D), F32),
        compiler_params=_params("arbitrary", "arbitrary"),
        name="moe_grouped_ffn",
    )(tile_expert, tile_valid, src_tok, x, nw.reshape(1, D), wg, wu, wd)


def _moe_combine_kernel(pos_ref, x_ref, rt_ref, ys_hbm, o_ref, buf, sem, *, tc, n_tok):
    i = pl.program_id(0)
    slot = i % 2

    def gather_tile(tile, into):
        base = tile * tc

        def start(j, carry):
            for k in range(TOP_K):
                _row_copy(ys_hbm, pos_ref[k * n_tok + base + j], buf.at[into, k], j, sem.at[into]).start()
            return carry

        lax.fori_loop(0, tc, start, 0, unroll=8)

    @pl.when(i == 0)
    def _():
        gather_tile(0, 0)

    @pl.when(i + 1 < pl.num_programs(0))
    def _():
        gather_tile(i + 1, 1 - slot)

    for k in range(TOP_K):
        pltpu.make_async_copy(ys_hbm.at[pl.ds(0, tc)], buf.at[slot, k], sem.at[slot]).wait()
    rt = rt_ref[...]
    o_ref[...] = (x_ref[...] + rt[:, ROUTE_LANE_G1:ROUTE_LANE_G1 + 1] * buf[slot, 0]
                  + rt[:, ROUTE_LANE_G2:ROUTE_LANE_G2 + 1] * buf[slot, 1])


def moe_combine(x, route, ys, pos, *, tc=256):
    S, D = x.shape
    grid_spec = pltpu.PrefetchScalarGridSpec(
        num_scalar_prefetch=1,
        grid=(S // tc,),
        in_specs=[
            pl.BlockSpec((tc, D), lambda i, pos: (i, 0)),
            pl.BlockSpec((tc, 128), lambda i, pos: (i, 0)),
            pl.BlockSpec(memory_space=pl.ANY),
        ],
        out_specs=pl.BlockSpec((tc, D), lambda i, pos: (i, 0)),
        scratch_shapes=[pltpu.VMEM((2, TOP_K, tc, D), F32), pltpu.SemaphoreType.DMA((2,))],
    )
    return pl.pallas_call(
        functools.partial(_moe_combine_kernel, tc=tc, n_tok=S),
        grid_spec=grid_spec,
        out_shape=jax.ShapeDtypeStruct((S, D), F32),
        compiler_params=_params("arbitrary"),
        name="moe_combine",
    )(pos, x, route, ys)


def moe_top2(x, nw, route, wg, wu, wd, *, tm=512):
    S, D = x.shape
    E = wg.shape[0]
    experts = jnp.concatenate([route[:, ROUTE_LANE_I1], route[:, ROUTE_LANE_I2]]).astype(jnp.int32)
    onehot = (experts[:, None] == jnp.arange(E, dtype=jnp.int32)).astype(jnp.int32)
    csum = jnp.cumsum(onehot, axis=0)
    rank = jnp.sum(onehot * csum, axis=1) - 1
    padded = (csum[-1] + tm - 1) // tm * tm
    ends = jnp.cumsum(padded)
    pos = (jnp.sum(onehot * (ends - padded), axis=1) + rank).astype(jnp.int32)
    n_tiles = TOP_K * S // tm + E + 1
    tokens = jnp.tile(jnp.arange(S, dtype=jnp.int32), TOP_K)
    src_tok = jnp.zeros((n_tiles * tm,), jnp.int32).at[pos].set(tokens)
    tile_start = jnp.arange(n_tiles, dtype=jnp.int32) * tm
    tile_expert = jnp.minimum(jnp.sum(tile_start[:, None] >= ends[None, :], axis=1), E - 1).astype(jnp.int32)
    tile_valid = (tile_start < ends[-1]).astype(jnp.int32)
    ys = moe_grouped_ffn(x, nw, tile_expert, tile_valid, src_tok, wg, wu, wd, tm=tm)
    return moe_combine(x, route, ys, pos)


ATTN_RADIUS = 64
ATTN_BQ = 128
ATTN_TILE = ATTN_BQ * max(d for _, d in DILATION_PATTERNS)
ATTN_HALO = ATTN_RADIUS * max(d for _, d in DILATION_PATTERNS)
ATTN_BLOCKS_IN_FLIGHT = 4
assert all(w // (2 * d) == ATTN_RADIUS for w, d in DILATION_PATTERNS)


def _attn_kernel(q_ref, kp_ref, kc_ref, kn_ref, vp_ref, vc_ref, vn_ref, o_ref, kbuf, vbuf, m_s, l_s, a_s, *, n_tok):
    tile, bq, halo, rad = ATTN_TILE, ATTN_BQ, ATTN_HALO, ATTN_RADIUS
    kbuf[0:halo, :] = kp_ref[...]
    kbuf[halo:halo + tile, :] = kc_ref[...]
    kbuf[halo + tile:, :] = kn_ref[...]
    vbuf[0:halo, :] = vp_ref[...]
    vbuf[halo:halo + tile, :] = vc_ref[...]
    vbuf[halo + tile:, :] = vn_ref[...]

    t0 = pl.program_id(1) * tile
    qi = lax.broadcasted_iota(jnp.int32, (bq, bq + 2 * rad), 0)
    kj = lax.broadcasted_iota(jnp.int32, (bq, bq + 2 * rad), 1)
    band = (kj >= qi) & (kj <= qi + 2 * rad)
    lo_half = lax.broadcasted_iota(jnp.int32, (bq, 2 * HEAD_DIM), 1) < HEAD_DIM

    for n_branch, (_, d) in enumerate(DILATION_PATTERNS):
        span = bq * d
        reps = max(1, ATTN_BLOCKS_IN_FLIGHT // d)

        def block(blk, carry, d=d, span=span, reps=reps, first=n_branch == 0):
            stride = None if d == 1 else d
            problems = [(rep, r) for rep in range(reps) for r in range(d)]

            def scores(rep, r):
                base = pl.multiple_of(blk * (reps * span), span) + rep * span
                q_rows = pl.ds(base + r, bq, stride=stride)
                k_rows = pl.ds(halo + base + (r - rad * d), bq + 2 * rad, stride=stride)
                q = q_ref[q_rows, :]
                kw = kbuf[k_rows, :].astype(BF16)
                tok = t0 + base + (r - rad * d) + d * kj
                mask = band & (tok >= 0) & (tok < n_tok)
                sc = [jnp.where(mask, _bdot_nt(jnp.where(own, q, 0.0), kw), NEG_INF)
                      for own in (lo_half, jnp.logical_not(lo_half))]
                return q_rows, k_rows, sc

            pending = scores(*problems[0])
            for i in range(len(problems)):
                q_rows, k_rows, sc_pair = pending
                if i + 1 < len(problems):
                    pending = scores(*problems[i + 1])
                vw = vbuf[k_rows, :].astype(BF16)
                halves = []
                for sc in sc_pair:
                    m_h = jnp.max(sc, axis=-1, keepdims=True)
                    p = jnp.exp(sc - m_h)
                    halves.append((m_h, jnp.sum(p, axis=-1, keepdims=True),
                                   jnp.dot(p.astype(BF16), vw, preferred_element_type=F32)))
                m_b, l_b, a_b = (jnp.where(lo_half, x0, x1) for x0, x1 in zip(*halves))
                if first:
                    m_n, l_n, a_n = m_b, l_b, a_b
                else:
                    m_o = m_s[q_rows, :]
                    m_n = jnp.maximum(m_o, m_b)
                    w_o = jnp.exp(m_o - m_n)
                    w_b = jnp.exp(m_b - m_n)
                    l_n = l_s[q_rows, :] * w_o + l_b * w_b
                    a_n = a_s[q_rows, :] * w_o + a_b * w_b
                m_s[q_rows, :] = m_n
                l_s[q_rows, :] = l_n
                a_s[q_rows, :] = a_n
            return carry

        lax.fori_loop(0, tile // (reps * span), block, 0)

    o_ref[...] = a_s[...] / l_s[...]


def dilated_attention(q, k, v):
    S, W = q.shape
    pair = 2 * HEAD_DIM
    per = ATTN_TILE // ATTN_HALO
    cur = pl.BlockSpec((ATTN_TILE, pair), lambda p, i: (i, p))
    prv = pl.BlockSpec((ATTN_HALO, pair), lambda p, i: (jnp.maximum(i * per - 1, 0), p))
    nxt = pl.BlockSpec((ATTN_HALO, pair), lambda p, i: (jnp.minimum((i + 1) * per, S // ATTN_HALO - 1), p))
    ext = ATTN_TILE + 2 * ATTN_HALO
    return pl.pallas_call(
        functools.partial(_attn_kernel, n_tok=S),
        grid=(W // pair, S // ATTN_TILE),
        in_specs=[cur, prv, cur, nxt, prv, cur, nxt],
        out_specs=cur,
        out_shape=jax.ShapeDtypeStruct((S, W), F32),
        scratch_shapes=[pltpu.VMEM((ext, pair), F32), pltpu.VMEM((ext, pair), F32)]
        + [pltpu.VMEM((ATTN_TILE, pair), F32)] * 3,
        compiler_params=_params("parallel", "arbitrary"),
        name="dilated_attention",
    )(q, k, k, k, v, v, v)


def _tri_masks(z):
    row = lax.broadcasted_iota(jnp.int32, (CHUNK, CHUNK), 0)
    col = lax.broadcasted_iota(jnp.int32, (CHUNK, CHUNK), 1)
    if z == 0:
        return col <= row, col < row
    return col >= row, col > row


SUB_CHUNKS = 2
STEP_ROWS = SUB_CHUNKS * CHUNK


def _sub_rows(z, j):
    i = j if z == 0 else SUB_CHUNKS - 1 - j
    return slice(i * CHUNK, (i + 1) * CHUNK)


def _neumann_solve(ns, xs):
    steps = CHUNK.bit_length() - 1
    for i in range(steps):
        xs = [x + _bdot(n, x) for n, x in zip(ns, xs)]
        if i + 1 < steps:
            ns = [_bdot(n, n) for n in ns]
    return xs


def _rwkv_chunk_kernel(*refs):
    ins = (refs[0:6], refs[6:12])
    y_refs = refs[12:14]
    state = refs[14]

    @pl.when(pl.program_id(0) == 0)
    def _():
        state[...] = jnp.zeros_like(state)

    chains = [(j, z, h) for j in range(SUB_CHUNKS) for z in range(2) for h in range(N_HEADS_B)]
    masks = [_tri_masks(z) for z in range(2)]
    prep = {}
    for j in range(SUB_CHUNKS):
        for z in range(2):
            rows = _sub_rows(z, j)
            r_ref, v_ref, a_ref, lw_ref, k_ref, b_ref = ins[z]
            tri = jnp.where(masks[z][0], 1.0, 0.0).astype(BF16)
            lw = lw_ref[rows, :]
            hi, mid, lo = _split3(lw)
            cum = (jnp.dot(tri, hi, preferred_element_type=F32) + jnp.dot(tri, mid, preferred_element_type=F32)
                   + jnp.dot(tri, lo, preferred_element_type=F32))
            tot = jnp.sum(lw, axis=0, keepdims=True)
            e_neg = jnp.exp(-cum)
            e_end = jnp.exp(tot - cum)
            k = k_ref[rows, :]
            b = b_ref[rows, :]
            prep[j, z] = dict(rt=r_ref[rows, :] * jnp.exp(cum), at=a_ref[rows, :] * jnp.exp(cum - lw), kt=k * e_neg,
                              bt=b * e_neg, kh=k * e_end, bh=b * e_end, dw=jnp.exp(tot), v=v_ref[rows, :])

    def part(name, c):
        j, z, h = c
        return prep[j, z][name][:, h * HEAD_DIM:(h + 1) * HEAD_DIM]

    xs = [_bdot_nt(jnp.concatenate([part("at", c), part("rt", c)], axis=0),
                   jnp.concatenate([part("bt", c), part("kt", c)], axis=0)) for c in chains]
    a_ab = [jnp.where(masks[c[1]][1], x[:CHUNK, :CHUNK], 0.0) for c, x in zip(chains, xs)]
    a_rb = [jnp.where(masks[c[1]][0], x[CHUNK:, :CHUNK], 0.0) for c, x in zip(chains, xs)]
    cys = [_bdot(jnp.concatenate([jnp.where(masks[c[1]][1], x[:CHUNK, CHUNK:], 0.0),
                                  jnp.where(masks[c[1]][0], x[CHUNK:, CHUNK:], 0.0)], axis=0), part("v", c))
           for c, x in zip(chains, xs)]
    pqs = _neumann_solve(a_ab, [jnp.concatenate([part("at", c), cy[:CHUNK]], axis=1) for c, cy in zip(chains, cys)])
    kvs = [_bdot_tn(part("v", c), part("kh", c)) for c in chains]
    per = 2 * N_HEADS_B
    s_cur = [state[z, h] for _, z, h in chains[:per]]
    for j in range(SUB_CHUNKS):
        sel = slice(j * per, (j + 1) * per)
        cs = chains[sel]
        us = [_bdot_nt(pq[:, :HEAD_DIM], s) + pq[:, HEAD_DIM:] for pq, s in zip(pqs[sel], s_cur)]
        ys = [_bdot_nt(part("rt", c), s) for c, s in zip(cs, s_cur)]
        ys = [y + _bdot(arb, u) + cy[CHUNK:] for y, arb, u, cy in zip(ys, a_rb[sel], us, cys[sel])]
        s_cur = [s * part("dw", c) + _bdot_tn(u, part("bh", c)) + kv for c, s, u, kv in zip(cs, s_cur, us, kvs[sel])]
        for (_, z, h), y in zip(cs, ys):
            y_refs[z][_sub_rows(z, j), h * HEAD_DIM:(h + 1) * HEAD_DIM] = y
    for (_, z, h), s in zip(chains[:per], s_cur):
        state[z, h] = s


def rwkv7_scan(r, v, a, lw, k, b):
    S, C = r.shape
    n = S // STEP_ROWS
    fwd = pl.BlockSpec((STEP_ROWS, C), lambda c: (c, 0))
    bwd = pl.BlockSpec((STEP_ROWS, C), lambda c: (n - 1 - c, 0))
    y0, y1 = pl.pallas_call(
        _rwkv_chunk_kernel,
        grid=(n,),
        in_specs=[fwd] * 6 + [bwd] * 6,
        out_specs=[fwd, bwd],
        out_shape=[jax.ShapeDtypeStruct((S, C), F32)] * 2,
        scratch_shapes=[pltpu.VMEM((2, N_HEADS_B, HEAD_DIM, HEAD_DIM), F32)],
        compiler_params=_params("arbitrary"),
        name="rwkv7_scan",
    )(r, v, a, lw[0], k[0], b[0], r, v, a, lw[1], k[1], b[1])
    return y0, y1


def _dn_chunk_kernel(*refs):
    ins = (refs[0:6], refs[6:12])
    o_refs = refs[12:14]
    state = refs[14]

    @pl.when(pl.program_id(0) == 0)
    def _():
        state[...] = jnp.zeros_like(state)

    chains = [(j, z, h) for j in range(SUB_CHUNKS) for z in range(2) for h in range(N_HEADS_D)]
    masks = [_tri_masks(z) for z in range(2)]
    nt = (((1,), (1,)), ((), ()))
    gcs, decays, betas, g_lasts = [], [], [], []
    for j in range(SUB_CHUNKS):
        for z in range(2):
            rows = _sub_rows(z, j)
            _, _, _, bcol_ref, gcol_ref, grow_ref = ins[z]
            incl = masks[z][0]
            tri = jnp.where(incl, 1.0, 0.0).astype(BF16)
            c_hi, c_mid, c_lo = _split3(gcol_ref[rows, :])
            gc_cols = (jnp.dot(tri, c_hi, preferred_element_type=F32) + jnp.dot(tri, c_mid, preferred_element_type=F32)
                       + jnp.dot(tri, c_lo, preferred_element_type=F32))
            r_hi, r_mid, r_lo = _split3(grow_ref[rows.start // CHUNK])
            gc_rows = (lax.dot_general(r_hi, tri, nt, preferred_element_type=F32)
                       + lax.dot_general(r_mid, tri, nt, preferred_element_type=F32)
                       + lax.dot_general(r_lo, tri, nt, preferred_element_type=F32))
            last = CHUNK - 1 if z == 0 else 0
            bcol = bcol_ref[rows, :]
            for h in range(N_HEADS_D):
                idx = z * N_HEADS_D + h
                gc = gc_cols[:, idx:idx + 1]
                diff = gc - gc_rows[idx:idx + 1, :]
                gcs.append(gc)
                decays.append(jnp.where(incl, jnp.exp(jnp.where(incl, diff, 0.0)), 0.0))
                betas.append(bcol[:, idx:idx + 1])
                g_lasts.append(gc[last:last + 1, :])

    def part(i, c):
        j, z, h = c
        return ins[z][i][_sub_rows(z, j), h * HEAD_DIM_D:(h + 1) * HEAD_DIM_D]

    qs = [part(0, c) for c in chains]
    ks = [part(1, c) for c in chains]
    vs = [part(2, c) for c in chains]
    kbs = [k * beta for k, beta in zip(ks, betas)]
    e_gcs = [jnp.exp(gc) for gc in gcs]
    kqs = [_bdot_nt(jnp.concatenate([kb, q], axis=0), k) for kb, q, k in zip(kbs, qs, ks)]
    n_mats = [jnp.where(masks[c[1]][1], -(kq[:CHUNK] * dc), 0.0) for c, kq, dc in zip(chains, kqs, decays)]
    attns = [kq[CHUNK:] * dc for kq, dc in zip(kqs, decays)]
    uks = _neumann_solve(n_mats, [jnp.concatenate([v * beta, kb * e], axis=1)
                                  for v, beta, kb, e in zip(vs, betas, kbs, e_gcs)])
    per = 2 * N_HEADS_D
    s_cur = [state[z, h] for _, z, h in chains[:per]]
    for j in range(SUB_CHUNKS):
        sel = slice(j * per, (j + 1) * per)
        us = [uk[:, :HEAD_DIM_D] - _bdot(uk[:, HEAD_DIM_D:], s) for uk, s in zip(uks[sel], s_cur)]
        os_ = [_bdot(q * e, s) for q, e, s in zip(qs[sel], e_gcs[sel], s_cur)]
        os_ = [o + _bdot(attn, u) for o, attn, u in zip(os_, attns[sel], us)]
        s_cur = [s * jnp.exp(gl) + _bdot_tn(k * jnp.exp(gl - gc), u)
                 for s, gl, k, gc, u in zip(s_cur, g_lasts[sel], ks[sel], gcs[sel], us)]
        for (_, z, h), o in zip(chains[sel], os_):
            o_refs[z][_sub_rows(z, j), h * HEAD_DIM_D:(h + 1) * HEAD_DIM_D] = o
    for (_, z, h), s in zip(chains[:per], s_cur):
        state[z, h] = s


def deltanet_scan(q, k, v, beta, g):
    S, C = q.shape
    n = S // STEP_ROWS
    nz = 2 * N_HEADS_D
    g_rows = g.reshape(S // CHUNK, CHUNK, nz).transpose(0, 2, 1)

    def specs(idx):
        wide = pl.BlockSpec((STEP_ROWS, C), lambda c: (idx(c), 0))
        col = pl.BlockSpec((STEP_ROWS, nz), lambda c: (idx(c), 0))
        row = pl.BlockSpec((SUB_CHUNKS, nz, CHUNK), lambda c: (idx(c), 0, 0))
        return [wide, wide, wide, col, col, row], wide

    in_f, out_f = specs(lambda c: c)
    in_b, out_b = specs(lambda c: n - 1 - c)
    o0, o1 = pl.pallas_call(
        _dn_chunk_kernel,
        grid=(n,),
        in_specs=in_f + in_b,
        out_specs=[out_f, out_b],
        out_shape=[jax.ShapeDtypeStruct((S, C), F32)] * 2,
        scratch_shapes=[pltpu.VMEM((2, N_HEADS_D, HEAD_DIM_D, HEAD_DIM_D), F32)],
        compiler_params=_params("arbitrary"),
        name="deltanet_scan",
    )(q, k, v, beta, g, g_rows, q, k, v, beta, g, g_rows)
    return o0, o1


def _rope_tables(positions):
    S = positions.shape[0]
    half = ROPE_DIM // 2
    inv_freq = jnp.power(ROPE_THETA, -jnp.arange(half, dtype=F32) / half)
    ang = positions[:, None].astype(F32) * inv_freq
    cos, sin = jnp.cos(ang), jnp.sin(ang)
    rest = HEAD_DIM - ROPE_DIM
    cos_t = jnp.concatenate([cos, cos, jnp.ones((S, rest), F32)], axis=1)
    sin_t = jnp.concatenate([-sin, sin, jnp.zeros((S, rest), F32)], axis=1)
    return jnp.tile(cos_t, (1, N_HEADS_A)), jnp.tile(sin_t, (1, N_HEADS_A))


def _ones_block_diag(width, group):
    idx = jnp.arange(width) // group
    return (idx[:, None] == idx[None, :]).astype(BF16)


def _block_diag(blocks):
    rows = sum(b.shape[0] for b in blocks)
    cols = sum(b.shape[1] for b in blocks)
    out = jnp.zeros((rows, cols), blocks[0].dtype)
    r = c = 0
    for b in blocks:
        out = lax.dynamic_update_slice(out, b, (r, c))
        r += b.shape[0]
        c += b.shape[1]
    return out


def _even_layer(x, positions, mix_norm, w_in, q_norm, k_norm, shift_mu, lora_mu, w0, w1, w2, a0, a1, a2, g1, g2,
                k_k, k_a, r_k, ln_w, ln_b, w_out, ffn_norm, ffn_gate, ffn_up, ffn_down):
    row = lambda t: t.reshape(1, -1)
    cos_t, sin_t = _rope_tables(positions)
    ones_bd = _ones_block_diag(WIDTH_B, HEAD_DIM)
    lora_in = jnp.concatenate([w1[0], w1[1], a1[0], a1[1], g1], axis=1).astype(BF16)
    lora_out = _block_diag([w2[0], w2[1], a2[0], a2[1], g2]).astype(BF16)
    (q, k, v, r, vb, a_vec, lw0, lw1, k0, k1, b0, b1, gate, bonus) = even_prep(
        x, mix_norm, w_in.astype(BF16), row(jnp.tile(q_norm, N_HEADS_A)), row(jnp.tile(k_norm, N_HEADS_A)), cos_t, sin_t,
        shift_mu, lora_mu, lora_in, lora_out, w0, a0, row(k_k), row(k_a), row(r_k), ones_bd)
    y_a = dilated_attention(q, k, v)
    y0, y1 = rwkv7_scan(r, vb, a_vec, (lw0, lw1), (k0, k1), (b0, b1))
    return even_post_ffn(x, y_a, y0, y1, gate, bonus, row(ln_w), row(ln_b), ones_bd, w_out.astype(BF16), ffn_norm,
                         ffn_gate.astype(BF16), ffn_up.astype(BF16), ffn_down.astype(BF16))


def _odd_layer(x, mix_norm, w_in, conv_c, conv_dn, A_log, dt_bias, dn_norm, w_out, ffn_norm, router, moe_gate, moe_up, moe_down):
    n_in = w_in.shape[1]
    n_pad = -(-n_in // 128) * 128
    w_in_p = jnp.pad(w_in, ((0, 0), (0, n_pad - n_in))).astype(BF16)
    nz = 2 * N_HEADS_D
    neg_a = jnp.zeros((1, 128), F32).at[0, nz:2 * nz].set(-jnp.exp(A_log.reshape(-1)))
    dt_b = jnp.zeros((1, 128), F32).at[0, nz:2 * nz].set(dt_bias.reshape(-1))
    ones_bd = _ones_block_diag(WIDTH_D, HEAD_DIM_D)
    y_c, q, k, v, zs, bg = odd_prep(x, mix_norm, w_in_p, conv_c, conv_dn, neg_a, dt_b, ones_bd)
    o0, o1 = deltanet_scan(q, k, v, bg[:, :nz], bg[:, nz:2 * nz])
    wr_pad = jnp.pad(router, ((0, 0), (0, 128 - N_EXPERTS)))
    x, route = odd_post(x, y_c, o0, o1, zs, jnp.tile(dn_norm, N_HEADS_D).reshape(1, -1), ones_bd, w_out.astype(BF16),
                        ffn_norm, wr_pad)
    return moe_top2(x, ffn_norm, route, moe_gate.astype(BF16), moe_up.astype(BF16), moe_down.astype(BF16))


def kernel(x, positions, ev_mix_norm, ev_w_in, ev_q_norm, ev_k_norm, ev_shift_mu, ev_lora_mu, ev_w0, ev_w1, ev_w2, ev_a0, ev_a1, ev_a2, ev_g1, ev_g2, ev_k_k, ev_k_a, ev_r_k, ev_ln_w, ev_ln_b, ev_w_out, ev_ffn_norm, ev_ffn_gate, ev_ffn_up, ev_ffn_down, od_mix_norm, od_w_in, od_conv_c, od_conv_dn, od_A_log, od_dt_bias, od_dn_norm, od_w_out, od_ffn_norm, od_router, od_moe_gate, od_moe_up, od_moe_down):
    B, S, D = x.shape
    assert B == 1
    xs = x.reshape(S, D)
    pos = positions.reshape(S)
    n_layers = ev_mix_norm.shape[0] + od_mix_norm.shape[0]
    for layer in range(n_layers):
        i = layer // 2
        if layer % 2 == 0:
            xs = _even_layer(xs, pos, ev_mix_norm[i], ev_w_in[i], ev_q_norm[i], ev_k_norm[i], ev_shift_mu[i], ev_lora_mu[i],
                             ev_w0[i], ev_w1[i], ev_w2[i], ev_a0[i], ev_a1[i], ev_a2[i], ev_g1[i], ev_g2[i], ev_k_k[i],
                             ev_k_a[i], ev_r_k[i], ev_ln_w[i], ev_ln_b[i], ev_w_out[i], ev_ffn_norm[i], ev_ffn_gate[i],
                             ev_ffn_up[i], ev_ffn_down[i])
        else:
            xs = _odd_layer(xs, od_mix_norm[i], od_w_in[i], od_conv_c[i], od_conv_dn[i], od_A_log[i], od_dt_bias[i],
                            od_dn_norm[i], od_w_out[i], od_ffn_norm[i], od_router[i], od_moe_gate[i], od_moe_up[i],
                            od_moe_down[i])
    return xs.reshape(B, S, D)
```

```python
import functools

import jax
import jax.numpy as jnp
from jax import lax
from jax.experimental import pallas as pl
from jax.experimental.pallas import tpu as pltpu

F32 = jnp.float32
BF16 = jnp.bfloat16

HEAD_DIM = 64
N_HEADS_A = 8
WIDTH_A = N_HEADS_A * HEAD_DIM
DILATION_PATTERNS = ((128, 1), (512, 4), (2048, 16))
ROPE_DIM = HEAD_DIM // 4
ROPE_THETA = 500000.0
N_HEADS_B = 8
WIDTH_B = N_HEADS_B * HEAD_DIM
RWKV_LN_EPS = 64e-5
WIDTH_C = 512
N_HEADS_D = 4
HEAD_DIM_D = 128
WIDTH_D = N_HEADS_D * HEAD_DIM_D
CHUNK = 64
N_EXPERTS = 8
TOP_K = 2
ROUTE_LANE_I1, ROUTE_LANE_I2, ROUTE_LANE_G1, ROUTE_LANE_G2 = 8, 9, 10, 11
MOE_ROW_SPLITS = 4
NORM_EPS = 1e-6
NEG_INF = -1e30

V7X_VMEM_LIMIT_BYTES = 56 * 1024 * 1024


def _params(*sem):
    return pltpu.CompilerParams(dimension_semantics=sem, vmem_limit_bytes=V7X_VMEM_LIMIT_BYTES)


def _bdot(a, b):
    return jnp.dot(a.astype(BF16), b.astype(BF16), preferred_element_type=F32)


def _bdot_nt(a, b):
    return lax.dot_general(a.astype(BF16), b.astype(BF16), (((1,), (1,)), ((), ())), preferred_element_type=F32)


def _bdot_tn(a, b):
    return lax.dot_general(a.astype(BF16), b.astype(BF16), (((0,), (0,)), ((), ())), preferred_element_type=F32)


def _split3(x):
    hi = x.astype(BF16)
    r1 = x - hi.astype(F32)
    mid = r1.astype(BF16)
    lo = (r1 - mid.astype(F32)).astype(BF16)
    return hi, mid, lo


def _rms(x, w):
    return x * lax.rsqrt(jnp.mean(x * x, axis=-1, keepdims=True) + NORM_EPS) * w


HALO = 8


def _group_sum(x, ones_bd):
    hi = x.astype(BF16)
    lo = (x - hi.astype(F32)).astype(BF16)
    return jnp.dot(hi, ones_bd, preferred_element_type=F32) + jnp.dot(lo, ones_bd, preferred_element_type=F32)


def _silu(x):
    return x * jax.nn.sigmoid(x)


def _halo_specs(tm, n_rows, width):
    per = tm // HALO
    cur = pl.BlockSpec((tm, width), lambda i: (i, 0))
    prv = pl.BlockSpec((HALO, width), lambda i: (jnp.maximum(i * per - 1, 0), 0))
    nxt = pl.BlockSpec((HALO, width), lambda i: (jnp.minimum((i + 1) * per, n_rows // HALO - 1), 0))
    return cur, prv, nxt


def _make_shifts(tm, n_rows):
    row = pl.program_id(0) * tm + lax.broadcasted_iota(jnp.int32, (tm, 1), 0)
    first, last = row == 0, row == n_rows - 1
    n_ext = tm + 2 * HALO

    def shifts(t):
        prev = jnp.where(first, 0.0, pltpu.roll(t, 1, 0)[HALO:HALO + tm])
        nxt = jnp.where(last, 0.0, pltpu.roll(t, n_ext - 1, 0)[HALO:HALO + tm])
        return prev, t[HALO:HALO + tm], nxt

    return shifts


def _even_prep_kernel(xc_ref, xp_ref, xn_ref, nw_ref, win_ref, qn_ref, kn_ref, cos_ref, sin_ref, smu_ref, lmu_ref,
                      lin_ref, lout_ref, w0_ref, a0_ref, kk_ref, ka_ref, rk_ref, ones_ref,
                      q_out, k_out, v_out, r_out, vb_out, a_out, lw0_out, lw1_out, k0_out, k1_out, b0_out, b1_out,
                      gate_out, bonus_out, *, tm, n_rows):
    shifts = _make_shifts(tm, n_rows)
    he = _rms(jnp.concatenate([xp_ref[...], xc_ref[...], xn_ref[...]], axis=0), nw_ref[...])
    proj = jnp.dot(he.astype(BF16), win_ref[...], preferred_element_type=F32)
    ones = ones_ref[...]
    lane = lax.broadcasted_iota(jnp.int32, (tm, WIDTH_A), 1) % HEAD_DIM

    def head_rms_rope(t, w):
        t = t * lax.rsqrt(_group_sum(t * t, ones) * (1.0 / HEAD_DIM) + NORM_EPS) * w
        half = ROPE_DIM // 2
        swapped = jnp.where(lane < half, pltpu.roll(t, WIDTH_A - half, 1), pltpu.roll(t, half, 1))
        return t * cos_ref[...] + swapped * sin_ref[...]

    cur = proj[HALO:HALO + tm]
    q = head_rms_rope(cur[:, :WIDTH_A], qn_ref[...]) * HEAD_DIM ** -0.5
    k = head_rms_rope(cur[:, WIDTH_A:2 * WIDTH_A], kn_ref[...])
    v = cur[:, 2 * WIDTH_A:3 * WIDTH_A]
    q_out[...] = q
    k_out[...] = k
    v_out[...] = v

    p_prev, p_cur, p_next = shifts(proj[:, 3 * WIDTH_A:])
    smu = smu_ref[...]
    rkv = p_cur + smu[0:1] * (p_prev - p_cur) + smu[1:2] * (p_next - p_cur)
    r, kin, vb = rkv[:, :WIDTH_B], rkv[:, WIDTH_B:2 * WIDTH_B], rkv[:, 2 * WIDTH_B:]
    h_prev, h_cur, h_next = shifts(he)
    lmu = lmu_ref[...]
    hx = h_cur + lmu[0:1] * (h_prev - h_cur) + lmu[1:2] * (h_next - h_cur)
    l1 = jnp.dot(hx.astype(BF16), lin_ref[...], preferred_element_type=F32)
    l1 = jnp.concatenate([jnp.tanh(l1[:, :128]), l1[:, 128:256], jax.nn.sigmoid(l1[:, 256:])], axis=1)
    l2 = jnp.dot(l1.astype(BF16), lout_ref[...], preferred_element_type=F32)
    w0, a0 = w0_ref[...], a0_ref[...]
    kk = kin * kk_ref[...]
    kk = kk * lax.rsqrt(_group_sum(kk * kk, ones) + 1e-6)
    kdirs = []
    for z, (lw_out, k_out_z, b_out_z) in enumerate(((lw0_out, k0_out, b0_out), (lw1_out, k1_out, b1_out))):
        w_pre = l2[:, z * WIDTH_B:(z + 1) * WIDTH_B] + w0[z:z + 1]
        lw_out[...] = -jnp.exp(-0.5) * jax.nn.sigmoid(w_pre)
        iclr = jax.nn.sigmoid(l2[:, (2 + z) * WIDTH_B:(3 + z) * WIDTH_B] + a0[z:z + 1])
        kdir = kin * (1.0 + (iclr - 1.0) * ka_ref[...])
        k_out_z[...] = kdir
        b_out_z[...] = kk * iclr
        kdirs.append(kdir)
    r_out[...] = r
    vb_out[...] = vb
    a_out[...] = -kk
    gate_out[...] = l2[:, 4 * WIDTH_B:]
    bonus_out[...] = _group_sum(r * (kdirs[0] + kdirs[1]) * rk_ref[...], ones) * vb


def even_prep(x, mix_norm, w_in, q_norm, k_norm, cos_t, sin_t, shift_mu, lora_mu, lora_in, lora_out, w0, a0, k_k, k_a,
              r_k, ones_bd, *, tm=256):
    S, D = x.shape
    cur, prv, nxt = _halo_specs(tm, S, D)
    full = lambda a: pl.BlockSpec(a.shape, lambda i: (0,) * a.ndim)
    rows = pl.BlockSpec((tm, WIDTH_B), lambda i: (i, 0))
    consts = [mix_norm.reshape(1, D), w_in, q_norm, k_norm]
    consts2 = [shift_mu, lora_mu, lora_in, lora_out, w0, a0, k_k, k_a, r_k, ones_bd]
    return pl.pallas_call(
        functools.partial(_even_prep_kernel, tm=tm, n_rows=S),
        grid=(S // tm,),
        in_specs=[cur, prv, nxt] + [full(a) for a in consts] + [rows, rows] + [full(a) for a in consts2],
        out_specs=[rows] * 14,
        out_shape=[jax.ShapeDtypeStruct((S, WIDTH_B), F32)] * 14,
        compiler_params=_params("parallel"),
        name="even_prep",
    )(x, x, x, *consts, cos_t, sin_t, *consts2)


def _odd_prep_kernel(xc_ref, xp_ref, xn_ref, nw_ref, win_ref, cc_ref, cdn_ref, nega_ref, dtb_ref, ones_ref,
                     yc_out, q_out, k_out, v_out, zs_out, bg_out, *, tm, n_rows):
    shifts = _make_shifts(tm, n_rows)
    he = _rms(jnp.concatenate([xp_ref[...], xc_ref[...], xn_ref[...]], axis=0), nw_ref[...])
    proj = jnp.dot(he.astype(BF16), win_ref[...], preferred_element_type=F32)
    cur = proj[HALO:HALO + tm]

    def conv3(t, w):
        prev, mid, nxt = shifts(t)
        return w[0:1] * prev + w[1:2] * mid + w[2:3] * nxt

    o_dn = 3 * WIDTH_C
    yc_out[...] = cur[:, :WIDTH_C] * conv3(proj[:, WIDTH_C:2 * WIDTH_C] * proj[:, 2 * WIDTH_C:o_dn], cc_ref[...])
    qkv = _silu(conv3(proj[:, o_dn:o_dn + 3 * WIDTH_D], cdn_ref[...]))
    ones = ones_ref[...]
    l2n = lambda t: t * lax.rsqrt(_group_sum(t * t, ones) + 1e-6)
    q_out[...] = l2n(qkv[:, :WIDTH_D]) * HEAD_DIM_D ** -0.5
    k_out[...] = l2n(qkv[:, WIDTH_D:2 * WIDTH_D])
    v_out[...] = qkv[:, 2 * WIDTH_D:]
    zs_out[...] = _silu(cur[:, o_dn + 3 * WIDTH_D:o_dn + 4 * WIDTH_D])
    tail = cur[:, o_dn + 4 * WIDTH_D:]
    t = tail + dtb_ref[...]
    softplus = jnp.maximum(t, 0.0) + jnp.log(1.0 + jnp.exp(-jnp.abs(t)))
    lane = lax.broadcasted_iota(jnp.int32, tail.shape, 1)
    bg_out[...] = jnp.where(lane < 2 * N_HEADS_D, jax.nn.sigmoid(tail), nega_ref[...] * softplus)


def odd_prep(x, mix_norm, w_in_pad, conv_c, conv_dn, neg_a, dt_b, ones_bd, *, tm=256):
    S, D = x.shape
    cur, prv, nxt = _halo_specs(tm, S, D)
    full = lambda a: pl.BlockSpec(a.shape, lambda i: (0,) * a.ndim)
    rows = pl.BlockSpec((tm, WIDTH_D), lambda i: (i, 0))
    consts = [mix_norm.reshape(1, D), w_in_pad, conv_c, conv_dn, neg_a, dt_b, ones_bd]
    return pl.pallas_call(
        functools.partial(_odd_prep_kernel, tm=tm, n_rows=S),
        grid=(S // tm,),
        in_specs=[cur, prv, nxt] + [full(a) for a in consts],
        out_specs=[rows] * 5 + [pl.BlockSpec((tm, 128), lambda i: (i, 0))],
        out_shape=[jax.ShapeDtypeStruct((S, WIDTH_D), F32)] * 5 + [jax.ShapeDtypeStruct((S, 128), F32)],
        compiler_params=_params("parallel"),
        name="odd_prep",
    )(x, x, x, *consts)


def _even_post_ffn_kernel(x_ref, ya_ref, y0_ref, y1_ref, gate_ref, bonus_ref, lnw_ref, lnb_ref, ones_ref, wo_ref,
                          fnw_ref, wg_ref, wu_ref, wd_ref, o_ref, h_scr):
    @pl.when(pl.program_id(1) == 0)
    def _():
        ones = ones_ref[...]
        yf = y0_ref[...] + y1_ref[...]
        dev = yf - _group_sum(yf, ones) * (1.0 / HEAD_DIM)
        var = _group_sum(dev * dev, ones) * (1.0 / HEAD_DIM)
        yn = dev * lax.rsqrt(var + RWKV_LN_EPS) * lnw_ref[...] + lnb_ref[...]
        y_b = (yn + bonus_ref[...]) * gate_ref[...]
        y = jnp.concatenate([ya_ref[...], y_b], axis=1)
        x = x_ref[...] + jnp.dot(y.astype(BF16), wo_ref[...], preferred_element_type=F32)
        h_scr[...] = _rms(x, fnw_ref[...]).astype(BF16)
        o_ref[...] = x

    h = h_scr[...]
    g = jnp.dot(h, wg_ref[...], preferred_element_type=F32)
    u = jnp.dot(h, wu_ref[...], preferred_element_type=F32)
    o_ref[...] += jnp.dot((_silu(g) * u).astype(BF16), wd_ref[...], preferred_element_type=F32)


def even_post_ffn(x, y_a, y0, y1, gate, bonus, ln_w, ln_b, ones_bd, w_out, ffn_norm, wg, wu, wd, *, tm=512, tf=1408):
    S, D = x.shape
    F = wg.shape[1]
    rows = pl.BlockSpec((tm, WIDTH_B), lambda i, f: (i, 0))
    full = lambda a: pl.BlockSpec(a.shape, lambda i, f: (0,) * a.ndim)
    consts = [ln_w, ln_b, ones_bd, w_out, ffn_norm.reshape(1, D)]
    return pl.pallas_call(
        _even_post_ffn_kernel,
        grid=(S // tm, F // tf),
        in_specs=[pl.BlockSpec((tm, D), lambda i, f: (i, 0)),
                  rows, rows, rows, rows, rows] + [full(a) for a in consts] + [
            pl.BlockSpec((D, tf), lambda i, f: (0, f)),
            pl.BlockSpec((D, tf), lambda i, f: (0, f)),
            pl.BlockSpec((tf, D), lambda i, f: (f, 0)),
        ],
        out_specs=pl.BlockSpec((tm, D), lambda i, f: (i, 0)),
        out_shape=jax.ShapeDtypeStruct((S, D), F32),
        scratch_shapes=[pltpu.VMEM((tm, D), BF16)],
        compiler_params=_params("parallel", "arbitrary"),
        name="even_post_ffn",
    )(x, y_a, y0, y1, gate, bonus, *consts, wg, wu, wd)


def _route_record(h, wr):
    h_hi, w_hi = h.astype(BF16), wr.astype(BF16)
    h_lo, w_lo = (h - h_hi.astype(F32)).astype(BF16), (wr - w_hi.astype(F32)).astype(BF16)
    logits = (jnp.dot(h_hi, w_hi, preferred_element_type=F32) + jnp.dot(h_lo, w_hi, preferred_element_type=F32)
              + jnp.dot(h_hi, w_lo, preferred_element_type=F32))
    lane = lax.broadcasted_iota(jnp.int32, logits.shape, 1)
    valid = lane < N_EXPERTS
    lg = jnp.where(valid, logits, NEG_INF)
    e = jnp.exp(lg - jnp.max(lg, axis=-1, keepdims=True))
    p = e / jnp.sum(e, axis=-1, keepdims=True)
    pm = jnp.where(valid, p, -1.0)
    m1 = jnp.max(pm, axis=-1, keepdims=True)
    i1 = jnp.min(jnp.where(pm == m1, lane, 128), axis=-1, keepdims=True)
    pm2 = jnp.where(lane == i1, -1.0, pm)
    m2 = jnp.max(pm2, axis=-1, keepdims=True)
    i2 = jnp.min(jnp.where(pm2 == m2, lane, 128), axis=-1, keepdims=True)
    tot = m1 + m2
    g1, g2 = m1 / tot, m2 / tot
    out = jnp.where(lane == i1, g1, 0.0) + jnp.where(lane == i2, g2, 0.0)
    out = jnp.where(lane == ROUTE_LANE_I1, i1.astype(F32), out)
    out = jnp.where(lane == ROUTE_LANE_I2, i2.astype(F32), out)
    out = jnp.where(lane == ROUTE_LANE_G1, g1, out)
    return jnp.where(lane == ROUTE_LANE_G2, g2, out)


def _odd_post_kernel(x_ref, yc_ref, o0_ref, o1_ref, zs_ref, dnw_ref, ones_ref, wo_ref, fnw_ref, wr_ref, x_out, route_out):
    o = o0_ref[...] + o1_ref[...]
    ms = _group_sum(o * o, ones_ref[...]) * (1.0 / HEAD_DIM_D)
    y_d = o * lax.rsqrt(ms + NORM_EPS) * dnw_ref[...] * zs_ref[...]
    y = jnp.concatenate([yc_ref[...], y_d], axis=1)
    x = x_ref[...] + jnp.dot(y.astype(BF16), wo_ref[...], preferred_element_type=F32)
    x_out[...] = x
    route_out[...] = _route_record(_rms(x, fnw_ref[...]), wr_ref[...])


def odd_post(x, y_c, o0, o1, zs, dn_norm, ones_bd, w_out, ffn_norm, wr_pad, *, tm=512):
    S, D = x.shape
    rows = pl.BlockSpec((tm, WIDTH_D), lambda i: (i, 0))
    full = lambda a: pl.BlockSpec(a.shape, lambda i: (0,) * a.ndim)
    consts = [dn_norm, ones_bd, w_out, ffn_norm.reshape(1, D), wr_pad]
    return pl.pallas_call(
        _odd_post_kernel,
        grid=(S // tm,),
        in_specs=[pl.BlockSpec((tm, D), lambda i: (i, 0)), rows, rows, rows, rows] + [full(a) for a in consts],
        out_specs=[pl.BlockSpec((tm, D), lambda i: (i, 0)), pl.BlockSpec((tm, 128), lambda i: (i, 0))],
        out_shape=[jax.ShapeDtypeStruct((S, D), F32), jax.ShapeDtypeStruct((S, 128), F32)],
        compiler_params=_params("parallel"),
        name="odd_post",
    )(x, y_c, o0, o1, zs, *consts)


def _row_copy(src_hbm, src_row, dst_vmem, dst_row, sem):
    return pltpu.make_async_copy(src_hbm.at[pl.ds(src_row, 1)], dst_vmem.at[pl.ds(dst_row, 1)], sem)


def _moe_group_kernel(te_ref, tv_ref, tok_ref, x_hbm, nw_ref, wg_ref, wu_ref, wd_ref, o_ref, xbuf, h_scr, sem, *, tm, n_f):
    m = pl.program_id(0)
    f = pl.program_id(1)
    valid = tv_ref[m] > 0
    slot = m % 2
    part = tm // n_f

    @pl.when((f == 0) & (m == 0))
    def _():
        def start(j, carry):
            _row_copy(x_hbm, tok_ref[j], xbuf.at[0], j, sem.at[0]).start()
            return carry

        lax.fori_loop(0, tm, start, 0, unroll=8)

    fed = valid | ((m > 0) & (tv_ref[jnp.maximum(m - 1, 0)] > 0))

    @pl.when(fed & (f == 0))
    def _():
        pltpu.make_async_copy(x_hbm.at[pl.ds(0, tm)], xbuf.at[slot], sem.at[slot]).wait()

    @pl.when(valid & (f == 0))
    def _():
        h_scr[...] = _rms(xbuf[slot], nw_ref[...]).astype(BF16)

    @pl.when(valid)
    def _():
        nbase = (m + 1) * tm + f * part
        for j in range(part):
            _row_copy(x_hbm, tok_ref[nbase + j], xbuf.at[1 - slot], f * part + j, sem.at[1 - slot]).start()

    quarter = tm // MOE_ROW_SPLITS
    need = (tv_ref[m] + quarter - 1) // quarter
    for n_q in range(1, MOE_ROW_SPLITS + 1):
        rows = n_q * quarter

        @pl.when(need == n_q)
        def _(rows=rows):
            h = h_scr[0:rows, :]
            g = jnp.dot(h, wg_ref[...], preferred_element_type=F32)
            u = jnp.dot(h, wu_ref[...], preferred_element_type=F32)
            y = jnp.dot((_silu(g) * u).astype(BF16), wd_ref[...], preferred_element_type=F32)

            @pl.when(f == 0)
            def _():
                o_ref[0:rows, :] = y
                if rows < tm:
                    o_ref[rows:, :] = jnp.zeros((tm - rows, o_ref.shape[1]), o_ref.dtype)

            @pl.when(f != 0)
            def _():
                o_ref[0:rows, :] += y

    @pl.when(jnp.logical_not(valid) & (f == 0))
    def _():
        o_ref[...] = jnp.zeros_like(o_ref)


def moe_grouped_ffn(x, nw, tile_expert, tile_rows, src_tok, wg, wu, wd, *, tm, tf=1792):
    S, D = x.shape
    E, _, F = wg.shape
    n_tiles = tile_expert.shape[0]
    nf = F // tf

    def w_in(m, f, te, tv, tok):
        return (te[m], 0, jnp.where(tv[m] > 0, f, nf - 1))

    def w_out(m, f, te, tv, tok):
        return (te[m], jnp.where(tv[m] > 0, f, nf - 1), 0)

    grid_spec = pltpu.PrefetchScalarGridSpec(
        num_scalar_prefetch=3,
        grid=(n_tiles, nf),
        in_specs=[
            pl.BlockSpec(memory_space=pl.ANY),
            pl.BlockSpec((1, D), lambda m, f, te, tv, tok: (0, 0)),
            pl.BlockSpec((None, D, tf), w_in),
            pl.BlockSpec((None, D, tf), w_in),
            pl.BlockSpec((None, tf, D), w_out),
        ],
        out_specs=pl.BlockSpec((tm, D), lambda m, f, te, tv, tok: (m, 0)),
        scratch_shapes=[pltpu.VMEM((2, tm, D), F32), pltpu.VMEM((tm, D), BF16), pltpu.SemaphoreType.DMA((2,))],
    )
    return pl.pallas_call(
        functools.partial(_moe_group_kernel, tm=tm, n_f=nf),
        grid_spec=grid_spec,
        out_shape=jax.ShapeDtypeStruct((n_tiles * tm, D), F32),
        compiler_params=_params("arbitrary", "arbitrary"),
        name="moe_grouped_ffn",
    )(tile_expert, tile_rows, src_tok, x, nw.reshape(1, D), wg, wu, wd)


def _moe_combine_kernel(pos_ref, x_ref, rt_ref, ys_hbm, o_ref, buf, sem, *, tc, n_tok):
    i = pl.program_id(0)
    slot = i % 2

    def gather_tile(tile, into):
        base = tile * tc

        def start(j, carry):
            for k in range(TOP_K):
                _row_copy(ys_hbm, pos_ref[k * n_tok + base + j], buf.at[into, k], j, sem.at[into]).start()
            return carry

        lax.fori_loop(0, tc, start, 0, unroll=8)

    @pl.when(i == 0)
    def _():
        gather_tile(0, 0)

    @pl.when(i + 1 < pl.num_programs(0))
    def _():
        gather_tile(i + 1, 1 - slot)

    for k in range(TOP_K):
        pltpu.make_async_copy(ys_hbm.at[pl.ds(0, tc)], buf.at[slot, k], sem.at[slot]).wait()
    rt = rt_ref[...]
    o_ref[...] = (x_ref[...] + rt[:, ROUTE_LANE_G1:ROUTE_LANE_G1 + 1] * buf[slot, 0]
                  + rt[:, ROUTE_LANE_G2:ROUTE_LANE_G2 + 1] * buf[slot, 1])


def moe_combine(x, route, ys, pos, *, tc=512):
    S, D = x.shape
    grid_spec = pltpu.PrefetchScalarGridSpec(
        num_scalar_prefetch=1,
        grid=(S // tc,),
        in_specs=[
            pl.BlockSpec((tc, D), lambda i, pos: (i, 0)),
            pl.BlockSpec((tc, 128), lambda i, pos: (i, 0)),
            pl.BlockSpec(memory_space=pl.ANY),
        ],
        out_specs=pl.BlockSpec((tc, D), lambda i, pos: (i, 0)),
        scratch_shapes=[pltpu.VMEM((2, TOP_K, tc, D), F32), pltpu.SemaphoreType.DMA((2,))],
    )
    return pl.pallas_call(
        functools.partial(_moe_combine_kernel, tc=tc, n_tok=S),
        grid_spec=grid_spec,
        out_shape=jax.ShapeDtypeStruct((S, D), F32),
        compiler_params=_params("arbitrary"),
        name="moe_combine",
    )(pos, x, route, ys)


def moe_top2(x, nw, route, wg, wu, wd, *, tm=512):
    S, D = x.shape
    E = wg.shape[0]
    experts = jnp.concatenate([route[:, ROUTE_LANE_I1], route[:, ROUTE_LANE_I2]]).astype(jnp.int32)
    onehot = (experts[:, None] == jnp.arange(E, dtype=jnp.int32)).astype(jnp.int32)
    csum = jnp.cumsum(onehot, axis=0)
    rank = jnp.sum(onehot * csum, axis=1) - 1
    padded = (csum[-1] + tm - 1) // tm * tm
    ends = jnp.cumsum(padded)
    pos = (jnp.sum(onehot * (ends - padded), axis=1) + rank).astype(jnp.int32)
    n_tiles = TOP_K * S // tm + E + 1
    tokens = jnp.tile(jnp.arange(S, dtype=jnp.int32), TOP_K)
    src_tok = jnp.zeros((n_tiles * tm,), jnp.int32).at[pos].set(tokens)
    tile_start = jnp.arange(n_tiles, dtype=jnp.int32) * tm
    tile_expert = jnp.minimum(jnp.sum(tile_start[:, None] >= ends[None, :], axis=1), E - 1).astype(jnp.int32)
    seg_end = (ends - padded + csum[-1])[tile_expert]
    tile_rows = jnp.clip(seg_end - tile_start, 0, tm).astype(jnp.int32)
    ys = moe_grouped_ffn(x, nw, tile_expert, tile_rows, src_tok, wg, wu, wd, tm=tm)
    return moe_combine(x, route, ys, pos)


ATTN_RADIUS = 64
ATTN_BQ = 128
ATTN_TILE = ATTN_BQ * max(d for _, d in DILATION_PATTERNS)
ATTN_HALO = ATTN_RADIUS * max(d for _, d in DILATION_PATTERNS)
ATTN_BLOCKS_IN_FLIGHT = 4
assert all(w // (2 * d) == ATTN_RADIUS for w, d in DILATION_PATTERNS)


def _attn_kernel(q_ref, kp_ref, kc_ref, kn_ref, vp_ref, vc_ref, vn_ref, o_ref, kbuf, vbuf, m_s, l_s, a_s, *, n_tok):
    tile, bq, halo, rad = ATTN_TILE, ATTN_BQ, ATTN_HALO, ATTN_RADIUS
    kbuf[0:halo, :] = kp_ref[...]
    kbuf[halo:halo + tile, :] = kc_ref[...]
    kbuf[halo + tile:, :] = kn_ref[...]
    vbuf[0:halo, :] = vp_ref[...]
    vbuf[halo:halo + tile, :] = vc_ref[...]
    vbuf[halo + tile:, :] = vn_ref[...]

    t0 = pl.program_id(1) * tile
    qi = lax.broadcasted_iota(jnp.int32, (bq, bq + 2 * rad), 0)
    kj = lax.broadcasted_iota(jnp.int32, (bq, bq + 2 * rad), 1)
    band = (kj >= qi) & (kj <= qi + 2 * rad)
    lo_half = lax.broadcasted_iota(jnp.int32, (bq, 2 * HEAD_DIM), 1) < HEAD_DIM

    for n_branch, (_, d) in enumerate(sorted(DILATION_PATTERNS, key=lambda wd: -wd[1])):
        span = bq * d
        reps = max(1, ATTN_BLOCKS_IN_FLIGHT // d)

        def block(blk, carry, d=d, span=span, reps=reps, first=n_branch == 0):
            stride = None if d == 1 else d
            problems = [(rep, r) for rep in range(reps) for r in range(d)]

            def scores(rep, r):
                base = pl.multiple_of(blk * (reps * span), span) + rep * span
                q_rows = pl.ds(base + r, bq, stride=stride)
                k_rows = pl.ds(halo + base + (r - rad * d), bq + 2 * rad, stride=stride)
                q = q_ref[q_rows, :]
                kw = kbuf[k_rows, :].astype(BF16)
                tok = t0 + base + (r - rad * d) + d * kj
                mask = band & (tok >= 0) & (tok < n_tok)
                sc = [jnp.where(mask, _bdot_nt(jnp.where(own, q, 0.0), kw), NEG_INF)
                      for own in (lo_half, jnp.logical_not(lo_half))]
                return q_rows, k_rows, sc

            pending = scores(*problems[0])
            for i in range(len(problems)):
                q_rows, k_rows, sc_pair = pending
                if i + 1 < len(problems):
                    pending = scores(*problems[i + 1])
                vw = vbuf[k_rows, :].astype(BF16)
                halves = []
                for sc in sc_pair:
                    m_h = jnp.max(sc, axis=-1, keepdims=True)
                    p = jnp.exp(sc - m_h)
                    halves.append((m_h, jnp.sum(p, axis=-1, keepdims=True),
                                   jnp.dot(p.astype(BF16), vw, preferred_element_type=F32)))
                m_b, l_b, a_b = (jnp.where(lo_half, x0, x1) for x0, x1 in zip(*halves))
                if first:
                    m_n, l_n, a_n = m_b, l_b, a_b
                else:
                    m_o = m_s[q_rows, :]
                    m_n = jnp.maximum(m_o, m_b)
                    w_o = jnp.exp(m_o - m_n)
                    w_b = jnp.exp(m_b - m_n)
                    l_n = l_s[q_rows, :] * w_o + l_b * w_b
                    a_n = a_s[q_rows, :] * w_o + a_b * w_b
                m_s[q_rows, :] = m_n
                l_s[q_rows, :] = l_n
                a_s[q_rows, :] = a_n
            return carry

        lax.fori_loop(0, tile // (reps * span), block, 0)

    o_ref[...] = a_s[...] / l_s[...]


def dilated_attention(q, k, v):
    S, W = q.shape
    pair = 2 * HEAD_DIM
    per = ATTN_TILE // ATTN_HALO
    cur = pl.BlockSpec((ATTN_TILE, pair), lambda p, i: (i, p))
    prv = pl.BlockSpec((ATTN_HALO, pair), lambda p, i: (jnp.maximum(i * per - 1, 0), p))
    nxt = pl.BlockSpec((ATTN_HALO, pair), lambda p, i: (jnp.minimum((i + 1) * per, S // ATTN_HALO - 1), p))
    ext = ATTN_TILE + 2 * ATTN_HALO
    return pl.pallas_call(
        functools.partial(_attn_kernel, n_tok=S),
        grid=(W // pair, S // ATTN_TILE),
        in_specs=[cur, prv, cur, nxt, prv, cur, nxt],
        out_specs=cur,
        out_shape=jax.ShapeDtypeStruct((S, W), F32),
        scratch_shapes=[pltpu.VMEM((ext, pair), F32), pltpu.VMEM((ext, pair), F32)]
        + [pltpu.VMEM((ATTN_TILE, pair), F32)] * 3,
        compiler_params=_params("parallel", "arbitrary"),
        name="dilated_attention",
    )(q, k, k, k, v, v, v)


def _tri_masks(z):
    row = lax.broadcasted_iota(jnp.int32, (CHUNK, CHUNK), 0)
    col = lax.broadcasted_iota(jnp.int32, (CHUNK, CHUNK), 1)
    if z == 0:
        return col <= row, col < row
    return col >= row, col > row


SUB_CHUNKS = 2
STEP_ROWS = SUB_CHUNKS * CHUNK


def _sub_rows(z, j):
    i = j if z == 0 else SUB_CHUNKS - 1 - j
    return slice(i * CHUNK, (i + 1) * CHUNK)


def _neumann_solve(ns, xs):
    steps = CHUNK.bit_length() - 1
    for i in range(steps):
        xs = [x + _bdot(n, x) for n, x in zip(ns, xs)]
        if i + 1 < steps:
            ns = [_bdot(n, n) for n in ns]
    return xs


def _rwkv_chunk_kernel(*refs):
    ins = (refs[0:6], refs[6:12])
    y_refs = refs[12:14]
    state = refs[14]

    @pl.when(pl.program_id(0) == 0)
    def _():
        state[...] = jnp.zeros_like(state)

    chains = [(j, z, h) for j in range(SUB_CHUNKS) for z in range(2) for h in range(N_HEADS_B)]
    masks = [_tri_masks(z) for z in range(2)]
    prep = {}
    for j in range(SUB_CHUNKS):
        for z in range(2):
            rows = _sub_rows(z, j)
            r_ref, v_ref, a_ref, lw_ref, k_ref, b_ref = ins[z]
            tri = jnp.where(masks[z][0], 1.0, 0.0).astype(BF16)
            lw = lw_ref[rows, :]
            hi, mid, lo = _split3(lw)
            cum = (jnp.dot(tri, hi, preferred_element_type=F32) + jnp.dot(tri, mid, preferred_element_type=F32)
                   + jnp.dot(tri, lo, preferred_element_type=F32))
            tot = jnp.sum(lw, axis=0, keepdims=True)
            e_neg = jnp.exp(-cum)
            e_end = jnp.exp(tot - cum)
            k = k_ref[rows, :]
            b = b_ref[rows, :]
            prep[j, z] = dict(rt=r_ref[rows, :] * jnp.exp(cum), at=a_ref[rows, :] * jnp.exp(cum - lw), kt=k * e_neg,
                              bt=b * e_neg, kh=k * e_end, bh=b * e_end, dw=jnp.exp(tot), v=v_ref[rows, :])

    def part(name, c):
        j, z, h = c
        return prep[j, z][name][:, h * HEAD_DIM:(h + 1) * HEAD_DIM]

    xs = [_bdot_nt(jnp.concatenate([part("at", c), part("rt", c)], axis=0),
                   jnp.concatenate([part("bt", c), part("kt", c)], axis=0)) for c in chains]
    a_ab = [jnp.where(masks[c[1]][1], x[:CHUNK, :CHUNK], 0.0) for c, x in zip(chains, xs)]
    a_rb = [jnp.where(masks[c[1]][0], x[CHUNK:, :CHUNK], 0.0) for c, x in zip(chains, xs)]
    cys = [_bdot(jnp.concatenate([jnp.where(masks[c[1]][1], x[:CHUNK, CHUNK:], 0.0),
                                  jnp.where(masks[c[1]][0], x[CHUNK:, CHUNK:], 0.0)], axis=0), part("v", c))
           for c, x in zip(chains, xs)]
    pqs = _neumann_solve(a_ab, [jnp.concatenate([part("at", c), cy[:CHUNK]], axis=1) for c, cy in zip(chains, cys)])
    kvs = [_bdot_tn(part("v", c), part("kh", c)) for c in chains]
    per = 2 * N_HEADS_B
    s_cur = [state[z, h] for _, z, h in chains[:per]]
    for j in range(SUB_CHUNKS):
        sel = slice(j * per, (j + 1) * per)
        cs = chains[sel]
        us = [_bdot_nt(pq[:, :HEAD_DIM], s) + pq[:, HEAD_DIM:] for pq, s in zip(pqs[sel], s_cur)]
        ys = [_bdot_nt(part("rt", c), s) for c, s in zip(cs, s_cur)]
        ys = [y + _bdot(arb, u) + cy[CHUNK:] for y, arb, u, cy in zip(ys, a_rb[sel], us, cys[sel])]
        s_cur = [s * part("dw", c) + _bdot_tn(u, part("bh", c)) + kv for c, s, u, kv in zip(cs, s_cur, us, kvs[sel])]
        for (_, z, h), y in zip(cs, ys):
            y_refs[z][_sub_rows(z, j), h * HEAD_DIM:(h + 1) * HEAD_DIM] = y
    for (_, z, h), s in zip(chains[:per], s_cur):
        state[z, h] = s


def rwkv7_scan(r, v, a, lw, k, b):
    S, C = r.shape
    n = S // STEP_ROWS
    fwd = pl.BlockSpec((STEP_ROWS, C), lambda c: (c, 0))
    bwd = pl.BlockSpec((STEP_ROWS, C), lambda c: (n - 1 - c, 0))
    y0, y1 = pl.pallas_call(
        _rwkv_chunk_kernel,
        grid=(n,),
        in_specs=[fwd] * 6 + [bwd] * 6,
        out_specs=[fwd, bwd],
        out_shape=[jax.ShapeDtypeStruct((S, C), F32)] * 2,
        scratch_shapes=[pltpu.VMEM((2, N_HEADS_B, HEAD_DIM, HEAD_DIM), F32)],
        compiler_params=_params("arbitrary"),
        name="rwkv7_scan",
    )(r, v, a, lw[0], k[0], b[0], r, v, a, lw[1], k[1], b[1])
    return y0, y1


def _dn_chunk_kernel(*refs):
    ins = (refs[0:6], refs[6:12])
    o_refs = refs[12:14]
    state = refs[14]

    @pl.when(pl.program_id(0) == 0)
    def _():
        state[...] = jnp.zeros_like(state)

    chains = [(j, z, h) for j in range(SUB_CHUNKS) for z in range(2) for h in range(N_HEADS_D)]
    masks = [_tri_masks(z) for z in range(2)]
    nt = (((1,), (1,)), ((), ()))
    gcs, decays, betas, g_lasts = [], [], [], []
    for j in range(SUB_CHUNKS):
        for z in range(2):
            rows = _sub_rows(z, j)
            _, _, _, bcol_ref, gcol_ref, grow_ref = ins[z]
            incl = masks[z][0]
            tri = jnp.where(incl, 1.0, 0.0).astype(BF16)
            c_hi, c_mid, c_lo = _split3(gcol_ref[rows, :])
            gc_cols = (jnp.dot(tri, c_hi, preferred_element_type=F32) + jnp.dot(tri, c_mid, preferred_element_type=F32)
                       + jnp.dot(tri, c_lo, preferred_element_type=F32))
            r_hi, r_mid, r_lo = _split3(grow_ref[rows.start // CHUNK])
            gc_rows = (lax.dot_general(r_hi, tri, nt, preferred_element_type=F32)
                       + lax.dot_general(r_mid, tri, nt, preferred_element_type=F32)
                       + lax.dot_general(r_lo, tri, nt, preferred_element_type=F32))
            last = CHUNK - 1 if z == 0 else 0
            bcol = bcol_ref[rows, :]
            for h in range(N_HEADS_D):
                idx = z * N_HEADS_D + h
                gc = gc_cols[:, idx:idx + 1]
                diff = gc - gc_rows[idx:idx + 1, :]
                gcs.append(gc)
                decays.append(jnp.where(incl, jnp.exp(jnp.where(incl, diff, 0.0)), 0.0))
                betas.append(bcol[:, idx:idx + 1])
                g_lasts.append(gc[last:last + 1, :])

    def part(i, c):
        j, z, h = c
        return ins[z][i][_sub_rows(z, j), h * HEAD_DIM_D:(h + 1) * HEAD_DIM_D]

    qs = [part(0, c) for c in chains]
    ks = [part(1, c) for c in chains]
    vs = [part(2, c) for c in chains]
    kbs = [k * beta for k, beta in zip(ks, betas)]
    e_gcs = [jnp.exp(gc) for gc in gcs]
    kqs = [_bdot_nt(jnp.concatenate([kb, q], axis=0), k) for kb, q, k in zip(kbs, qs, ks)]
    n_mats = [jnp.where(masks[c[1]][1], -(kq[:CHUNK] * dc), 0.0) for c, kq, dc in zip(chains, kqs, decays)]
    attns = [kq[CHUNK:] * dc for kq, dc in zip(kqs, decays)]
    uks = _neumann_solve(n_mats, [jnp.concatenate([v * beta, kb * e], axis=1)
                                  for v, beta, kb, e in zip(vs, betas, kbs, e_gcs)])
    per = 2 * N_HEADS_D
    s_cur = [state[z, h] for _, z, h in chains[:per]]
    for j in range(SUB_CHUNKS):
        sel = slice(j * per, (j + 1) * per)
        us = [uk[:, :HEAD_DIM_D] - _bdot(uk[:, HEAD_DIM_D:], s) for uk, s in zip(uks[sel], s_cur)]
        os_ = [_bdot(q * e, s) for q, e, s in zip(qs[sel], e_gcs[sel], s_cur)]
        os_ = [o + _bdot(attn, u) for o, attn, u in zip(os_, attns[sel], us)]
        s_cur = [s * jnp.exp(gl) + _bdot_tn(k * jnp.exp(gl - gc), u)
                 for s, gl, k, gc, u in zip(s_cur, g_lasts[sel], ks[sel], gcs[sel], us)]
        for (_, z, h), o in zip(chains[sel], os_):
            o_refs[z][_sub_rows(z, j), h * HEAD_DIM_D:(h + 1) * HEAD_DIM_D] = o
    for (_, z, h), s in zip(chains[:per], s_cur):
        state[z, h] = s


def deltanet_scan(q, k, v, beta, g):
    S, C = q.shape
    n = S // STEP_ROWS
    nz = 2 * N_HEADS_D
    g_rows = g.reshape(S // CHUNK, CHUNK, nz).transpose(0, 2, 1)

    def specs(idx):
        wide = pl.BlockSpec((STEP_ROWS, C), lambda c: (idx(c), 0))
        col = pl.BlockSpec((STEP_ROWS, nz), lambda c: (idx(c), 0))
        row = pl.BlockSpec((SUB_CHUNKS, nz, CHUNK), lambda c: (idx(c), 0, 0))
        return [wide, wide, wide, col, col, row], wide

    in_f, out_f = specs(lambda c: c)
    in_b, out_b = specs(lambda c: n - 1 - c)
    o0, o1 = pl.pallas_call(
        _dn_chunk_kernel,
        grid=(n,),
        in_specs=in_f + in_b,
        out_specs=[out_f, out_b],
        out_shape=[jax.ShapeDtypeStruct((S, C), F32)] * 2,
        scratch_shapes=[pltpu.VMEM((2, N_HEADS_D, HEAD_DIM_D, HEAD_DIM_D), F32)],
        compiler_params=_params("arbitrary"),
        name="deltanet_scan",
    )(q, k, v, beta, g, g_rows, q, k, v, beta, g, g_rows)
    return o0, o1


def _rope_tables(positions):
    S = positions.shape[0]
    half = ROPE_DIM // 2
    inv_freq = jnp.power(ROPE_THETA, -jnp.arange(half, dtype=F32) / half)
    ang = positions[:, None].astype(F32) * inv_freq
    cos, sin = jnp.cos(ang), jnp.sin(ang)
    rest = HEAD_DIM - ROPE_DIM
    cos_t = jnp.concatenate([cos, cos, jnp.ones((S, rest), F32)], axis=1)
    sin_t = jnp.concatenate([-sin, sin, jnp.zeros((S, rest), F32)], axis=1)
    return jnp.tile(cos_t, (1, N_HEADS_A)), jnp.tile(sin_t, (1, N_HEADS_A))


def _ones_block_diag(width, group):
    idx = jnp.arange(width) // group
    return (idx[:, None] == idx[None, :]).astype(BF16)


def _block_diag(blocks):
    rows = sum(b.shape[0] for b in blocks)
    cols = sum(b.shape[1] for b in blocks)
    out = jnp.zeros((rows, cols), blocks[0].dtype)
    r = c = 0
    for b in blocks:
        out = lax.dynamic_update_slice(out, b, (r, c))
        r += b.shape[0]
        c += b.shape[1]
    return out


def _even_layer(x, positions, mix_norm, w_in, q_norm, k_norm, shift_mu, lora_mu, w0, w1, w2, a0, a1, a2, g1, g2,
                k_k, k_a, r_k, ln_w, ln_b, w_out, ffn_norm, ffn_gate, ffn_up, ffn_down):
    row = lambda t: t.reshape(1, -1)
    cos_t, sin_t = _rope_tables(positions)
    ones_bd = _ones_block_diag(WIDTH_B, HEAD_DIM)
    lora_in = jnp.concatenate([w1[0], w1[1], a1[0], a1[1], g1], axis=1).astype(BF16)
    lora_out = _block_diag([w2[0], w2[1], a2[0], a2[1], g2]).astype(BF16)
    (q, k, v, r, vb, a_vec, lw0, lw1, k0, k1, b0, b1, gate, bonus) = even_prep(
        x, mix_norm, w_in.astype(BF16), row(jnp.tile(q_norm, N_HEADS_A)), row(jnp.tile(k_norm, N_HEADS_A)), cos_t, sin_t,
        shift_mu, lora_mu, lora_in, lora_out, w0, a0, row(k_k), row(k_a), row(r_k), ones_bd)
    y_a = dilated_attention(q, k, v)
    y0, y1 = rwkv7_scan(r, vb, a_vec, (lw0, lw1), (k0, k1), (b0, b1))
    return even_post_ffn(x, y_a, y0, y1, gate, bonus, row(ln_w), row(ln_b), ones_bd, w_out.astype(BF16), ffn_norm,
                         ffn_gate.astype(BF16), ffn_up.astype(BF16), ffn_down.astype(BF16))


def _odd_layer(x, mix_norm, w_in, conv_c, conv_dn, A_log, dt_bias, dn_norm, w_out, ffn_norm, router, moe_gate, moe_up, moe_down):
    n_in = w_in.shape[1]
    n_pad = -(-n_in // 128) * 128
    w_in_p = jnp.pad(w_in, ((0, 0), (0, n_pad - n_in))).astype(BF16)
    nz = 2 * N_HEADS_D
    neg_a = jnp.zeros((1, 128), F32).at[0, nz:2 * nz].set(-jnp.exp(A_log.reshape(-1)))
    dt_b = jnp.zeros((1, 128), F32).at[0, nz:2 * nz].set(dt_bias.reshape(-1))
    ones_bd = _ones_block_diag(WIDTH_D, HEAD_DIM_D)
    y_c, q, k, v, zs, bg = odd_prep(x, mix_norm, w_in_p, conv_c, conv_dn, neg_a, dt_b, ones_bd)
    o0, o1 = deltanet_scan(q, k, v, bg[:, :nz], bg[:, nz:2 * nz])
    wr_pad = jnp.pad(router, ((0, 0), (0, 128 - N_EXPERTS)))
    x, route = odd_post(x, y_c, o0, o1, zs, jnp.tile(dn_norm, N_HEADS_D).reshape(1, -1), ones_bd, w_out.astype(BF16),
                        ffn_norm, wr_pad)
    return moe_top2(x, ffn_norm, route, moe_gate.astype(BF16), moe_up.astype(BF16), moe_down.astype(BF16))


def kernel(x, positions, ev_mix_norm, ev_w_in, ev_q_norm, ev_k_norm, ev_shift_mu, ev_lora_mu, ev_w0, ev_w1, ev_w2, ev_a0, ev_a1, ev_a2, ev_g1, ev_g2, ev_k_k, ev_k_a, ev_r_k, ev_ln_w, ev_ln_b, ev_w_out, ev_ffn_norm, ev_ffn_gate, ev_ffn_up, ev_ffn_down, od_mix_norm, od_w_in, od_conv_c, od_conv_dn, od_A_log, od_dt_bias, od_dn_norm, od_w_out, od_ffn_norm, od_router, od_moe_gate, od_moe_up, od_moe_down):
    B, S, D = x.shape
    assert B == 1
    xs = x.reshape(S, D)
    pos = positions.reshape(S)
    n_layers = ev_mix_norm.shape[0] + od_mix_norm.shape[0]
    for layer in range(n_layers):
        i = layer // 2
        if layer % 2 == 0:
            xs = _even_layer(xs, pos, ev_mix_norm[i], ev_w_in[i], ev_q_norm[i], ev_k_norm[i], ev_shift_mu[i], ev_lora_mu[i],
                             ev_w0[i], ev_w1[i], ev_w2[i], ev_a0[i], ev_a1[i], ev_a2[i], ev_g1[i], ev_g2[i], ev_k_k[i],
                             ev_k_a[i], ev_r_k[i], ev_ln_w[i], ev_ln_b[i], ev_w_out[i], ev_ffn_norm[i], ev_ffn_gate[i],
                             ev_ffn_up[i], ev_ffn_down[i])
        else:
            xs = _odd_layer(xs, od_mix_norm[i], od_w_in[i], od_conv_c[i], od_conv_dn[i], od_A_log[i], od_dt_bias[i],
                            od_dn_norm[i], od_w_out[i], od_ffn_norm[i], od_router[i], od_moe_gate[i], od_moe_up[i],
                            od_moe_down[i])
    return xs.reshape(B, S, D)
```

```python
import functools

import jax
import jax.numpy as jnp
from jax import lax
from jax.experimental import pallas as pl
from jax.experimental.pallas import tpu as pltpu

F32 = jnp.float32
BF16 = jnp.bfloat16

HEAD_DIM = 64
N_HEADS_A = 8
WIDTH_A = N_HEADS_A * HEAD_DIM
DILATION_PATTERNS = ((128, 1), (512, 4), (2048, 16))
ROPE_DIM = HEAD_DIM // 4
ROPE_THETA = 500000.0
N_HEADS_B = 8
WIDTH_B = N_HEADS_B * HEAD_DIM
RWKV_LN_EPS = 64e-5
WIDTH_C = 512
N_HEADS_D = 4
HEAD_DIM_D = 128
WIDTH_D = N_HEADS_D * HEAD_DIM_D
CHUNK = 64
N_EXPERTS = 8
TOP_K = 2
ROUTE_LANE_I1, ROUTE_LANE_I2, ROUTE_LANE_G1, ROUTE_LANE_G2 = 8, 9, 10, 11
MOE_ROW_SPLITS = 4
NORM_EPS = 1e-6
NEG_INF = -1e30

V7X_VMEM_LIMIT_BYTES = 56 * 1024 * 1024


def _params(*sem):
    return pltpu.CompilerParams(dimension_semantics=sem, vmem_limit_bytes=V7X_VMEM_LIMIT_BYTES)


def _bdot(a, b):
    return jnp.dot(a.astype(BF16), b.astype(BF16), preferred_element_type=F32)


def _bdot_nt(a, b):
    return lax.dot_general(a.astype(BF16), b.astype(BF16), (((1,), (1,)), ((), ())), preferred_element_type=F32)


def _bdot_tn(a, b):
    return lax.dot_general(a.astype(BF16), b.astype(BF16), (((0,), (0,)), ((), ())), preferred_element_type=F32)


def _split3(x):
    hi = x.astype(BF16)
    r1 = x - hi.astype(F32)
    mid = r1.astype(BF16)
    lo = (r1 - mid.astype(F32)).astype(BF16)
    return hi, mid, lo


def _rms(x, w):
    return x * lax.rsqrt(jnp.mean(x * x, axis=-1, keepdims=True) + NORM_EPS) * w


HALO = 8


def _group_sum(x, ones_bd):
    hi = x.astype(BF16)
    lo = (x - hi.astype(F32)).astype(BF16)
    return jnp.dot(hi, ones_bd, preferred_element_type=F32) + jnp.dot(lo, ones_bd, preferred_element_type=F32)


def _silu(x):
    return x * jax.nn.sigmoid(x)


def _halo_specs(tm, n_rows, width):
    per = tm // HALO
    cur = pl.BlockSpec((tm, width), lambda i: (i, 0))
    prv = pl.BlockSpec((HALO, width), lambda i: (jnp.maximum(i * per - 1, 0), 0))
    nxt = pl.BlockSpec((HALO, width), lambda i: (jnp.minimum((i + 1) * per, n_rows // HALO - 1), 0))
    return cur, prv, nxt


def _make_shifts(tm, n_rows):
    row = pl.program_id(0) * tm + lax.broadcasted_iota(jnp.int32, (tm, 1), 0)
    first, last = row == 0, row == n_rows - 1
    n_ext = tm + 2 * HALO

    def shifts(t):
        prev = jnp.where(first, 0.0, pltpu.roll(t, 1, 0)[HALO:HALO + tm])
        nxt = jnp.where(last, 0.0, pltpu.roll(t, n_ext - 1, 0)[HALO:HALO + tm])
        return prev, t[HALO:HALO + tm], nxt

    return shifts


def _even_prep_kernel(xc_ref, xp_ref, xn_ref, nw_ref, win_ref, qn_ref, kn_ref, cos_ref, sin_ref, smu_ref, lmu_ref,
                      lin_ref, lout_ref, w0_ref, a0_ref, kk_ref, ka_ref, rk_ref, ones_ref,
                      q_out, k_out, v_out, r_out, vb_out, a_out, lw0_out, lw1_out, k0_out, k1_out, b0_out, b1_out,
                      gate_out, bonus_out, *, tm, n_rows):
    shifts = _make_shifts(tm, n_rows)
    he = _rms(jnp.concatenate([xp_ref[...], xc_ref[...], xn_ref[...]], axis=0), nw_ref[...])
    proj = jnp.dot(he.astype(BF16), win_ref[...], preferred_element_type=F32)
    ones = ones_ref[...]
    lane = lax.broadcasted_iota(jnp.int32, (tm, WIDTH_A), 1) % HEAD_DIM

    n_pairs = WIDTH_A // cos_ref.shape[1]
    cos_t = jnp.concatenate([cos_ref[...]] * n_pairs, axis=1)
    sin_t = jnp.concatenate([sin_ref[...]] * n_pairs, axis=1)

    def head_rms_rope(t, w):
        t = t * lax.rsqrt(_group_sum(t * t, ones) * (1.0 / HEAD_DIM) + NORM_EPS) * w
        half = ROPE_DIM // 2
        swapped = jnp.where(lane < half, pltpu.roll(t, WIDTH_A - half, 1), pltpu.roll(t, half, 1))
        return t * cos_t + swapped * sin_t

    cur = proj[HALO:HALO + tm]
    q = head_rms_rope(cur[:, :WIDTH_A], qn_ref[...]) * HEAD_DIM ** -0.5
    k = head_rms_rope(cur[:, WIDTH_A:2 * WIDTH_A], kn_ref[...])
    v = cur[:, 2 * WIDTH_A:3 * WIDTH_A]
    q_out[...] = q
    k_out[...] = k
    v_out[...] = v

    p_prev, p_cur, p_next = shifts(proj[:, 3 * WIDTH_A:])
    smu = smu_ref[...]
    rkv = p_cur + smu[0:1] * (p_prev - p_cur) + smu[1:2] * (p_next - p_cur)
    r, kin, vb = rkv[:, :WIDTH_B], rkv[:, WIDTH_B:2 * WIDTH_B], rkv[:, 2 * WIDTH_B:]
    h_prev, h_cur, h_next = shifts(he)
    lmu = lmu_ref[...]
    hx = h_cur + lmu[0:1] * (h_prev - h_cur) + lmu[1:2] * (h_next - h_cur)
    l1 = jnp.dot(hx.astype(BF16), lin_ref[...], preferred_element_type=F32)
    l1 = jnp.concatenate([jnp.tanh(l1[:, :128]), l1[:, 128:256], jax.nn.sigmoid(l1[:, 256:])], axis=1)
    l2 = jnp.dot(l1.astype(BF16), lout_ref[...], preferred_element_type=F32)
    w0, a0 = w0_ref[...], a0_ref[...]
    kk = kin * kk_ref[...]
    kk = kk * lax.rsqrt(_group_sum(kk * kk, ones) + 1e-6)
    kdirs = []
    for z, (lw_out, k_out_z, b_out_z) in enumerate(((lw0_out, k0_out, b0_out), (lw1_out, k1_out, b1_out))):
        w_pre = l2[:, z * WIDTH_B:(z + 1) * WIDTH_B] + w0[z:z + 1]
        lw_out[...] = -jnp.exp(-0.5) * jax.nn.sigmoid(w_pre)
        iclr = jax.nn.sigmoid(l2[:, (2 + z) * WIDTH_B:(3 + z) * WIDTH_B] + a0[z:z + 1])
        kdir = kin * (1.0 + (iclr - 1.0) * ka_ref[...])
        k_out_z[...] = kdir
        b_out_z[...] = kk * iclr
        kdirs.append(kdir)
    r_out[...] = r
    vb_out[...] = vb
    a_out[...] = -kk
    gate_out[...] = l2[:, 4 * WIDTH_B:]
    bonus_out[...] = _group_sum(r * (kdirs[0] + kdirs[1]) * rk_ref[...], ones) * vb


def even_prep(x, mix_norm, w_in, q_norm, k_norm, cos_t, sin_t, shift_mu, lora_mu, lora_in, lora_out, w0, a0, k_k, k_a,
              r_k, ones_bd, *, tm=256):
    S, D = x.shape
    cur, prv, nxt = _halo_specs(tm, S, D)
    full = lambda a: pl.BlockSpec(a.shape, lambda i: (0,) * a.ndim)
    rows = pl.BlockSpec((tm, WIDTH_B), lambda i: (i, 0))
    consts = [mix_norm.reshape(1, D), w_in, q_norm, k_norm]
    consts2 = [shift_mu, lora_mu, lora_in, lora_out, w0, a0, k_k, k_a, r_k, ones_bd]
    return pl.pallas_call(
        functools.partial(_even_prep_kernel, tm=tm, n_rows=S),
        grid=(S // tm,),
        in_specs=[cur, prv, nxt] + [full(a) for a in consts]
        + [pl.BlockSpec((tm, cos_t.shape[1]), lambda i: (i, 0))] * 2 + [full(a) for a in consts2],
        out_specs=[rows] * 14,
        out_shape=[jax.ShapeDtypeStruct((S, WIDTH_B), F32)] * 14,
        compiler_params=_params("parallel"),
        name="even_prep",
    )(x, x, x, *consts, cos_t, sin_t, *consts2)


def _odd_prep_kernel(xc_ref, xp_ref, xn_ref, nw_ref, win_ref, cc_ref, cdn_ref, nega_ref, dtb_ref, ones_ref,
                     yc_out, q_out, k_out, v_out, zs_out, bg_out, *, tm, n_rows):
    shifts = _make_shifts(tm, n_rows)
    he = _rms(jnp.concatenate([xp_ref[...], xc_ref[...], xn_ref[...]], axis=0), nw_ref[...])
    proj = jnp.dot(he.astype(BF16), win_ref[...], preferred_element_type=F32)
    cur = proj[HALO:HALO + tm]

    def conv3(t, w):
        prev, mid, nxt = shifts(t)
        return w[0:1] * prev + w[1:2] * mid + w[2:3] * nxt

    o_dn = 3 * WIDTH_C
    yc_out[...] = cur[:, :WIDTH_C] * conv3(proj[:, WIDTH_C:2 * WIDTH_C] * proj[:, 2 * WIDTH_C:o_dn], cc_ref[...])
    qkv = _silu(conv3(proj[:, o_dn:o_dn + 3 * WIDTH_D], cdn_ref[...]))
    ones = ones_ref[...]
    l2n = lambda t: t * lax.rsqrt(_group_sum(t * t, ones) + 1e-6)
    q_out[...] = l2n(qkv[:, :WIDTH_D]) * HEAD_DIM_D ** -0.5
    k_out[...] = l2n(qkv[:, WIDTH_D:2 * WIDTH_D])
    v_out[...] = qkv[:, 2 * WIDTH_D:]
    zs_out[...] = _silu(cur[:, o_dn + 3 * WIDTH_D:o_dn + 4 * WIDTH_D])
    tail = cur[:, o_dn + 4 * WIDTH_D:]
    t = tail + dtb_ref[...]
    softplus = jnp.maximum(t, 0.0) + jnp.log(1.0 + jnp.exp(-jnp.abs(t)))
    lane = lax.broadcasted_iota(jnp.int32, tail.shape, 1)
    bg_out[...] = jnp.where(lane < 2 * N_HEADS_D, jax.nn.sigmoid(tail), nega_ref[...] * softplus)


def odd_prep(x, mix_norm, w_in_pad, conv_c, conv_dn, neg_a, dt_b, ones_bd, *, tm=256):
    S, D = x.shape
    cur, prv, nxt = _halo_specs(tm, S, D)
    full = lambda a: pl.BlockSpec(a.shape, lambda i: (0,) * a.ndim)
    rows = pl.BlockSpec((tm, WIDTH_D), lambda i: (i, 0))
    consts = [mix_norm.reshape(1, D), w_in_pad, conv_c, conv_dn, neg_a, dt_b, ones_bd]
    return pl.pallas_call(
        functools.partial(_odd_prep_kernel, tm=tm, n_rows=S),
        grid=(S // tm,),
        in_specs=[cur, prv, nxt] + [full(a) for a in consts],
        out_specs=[rows] * 5 + [pl.BlockSpec((tm, 128), lambda i: (i, 0))],
        out_shape=[jax.ShapeDtypeStruct((S, WIDTH_D), F32)] * 5 + [jax.ShapeDtypeStruct((S, 128), F32)],
        compiler_params=_params("parallel"),
        name="odd_prep",
    )(x, x, x, *consts)


def _even_post_ffn_kernel(x_ref, ya_ref, y0_ref, y1_ref, gate_ref, bonus_ref, lnw_ref, lnb_ref, ones_ref, wo_ref,
                          fnw_ref, wg_ref, wu_ref, wd_ref, o_ref, h_scr):
    @pl.when(pl.program_id(1) == 0)
    def _():
        ones = ones_ref[...]
        yf = y0_ref[...] + y1_ref[...]
        dev = yf - _group_sum(yf, ones) * (1.0 / HEAD_DIM)
        var = _group_sum(dev * dev, ones) * (1.0 / HEAD_DIM)
        yn = dev * lax.rsqrt(var + RWKV_LN_EPS) * lnw_ref[...] + lnb_ref[...]
        y_b = (yn + bonus_ref[...]) * gate_ref[...]
        y = jnp.concatenate([ya_ref[...], y_b], axis=1)
        x = x_ref[...] + jnp.dot(y.astype(BF16), wo_ref[...], preferred_element_type=F32)
        h_scr[...] = _rms(x, fnw_ref[...]).astype(BF16)
        o_ref[...] = x

    h = h_scr[...]
    g = jnp.dot(h, wg_ref[...], preferred_element_type=F32)
    u = jnp.dot(h, wu_ref[...], preferred_element_type=F32)
    o_ref[...] += jnp.dot((_silu(g) * u).astype(BF16), wd_ref[...], preferred_element_type=F32)


def even_post_ffn(x, y_a, y0, y1, gate, bonus, ln_w, ln_b, ones_bd, w_out, ffn_norm, wg, wu, wd, *, tm=512, tf=1408):
    S, D = x.shape
    F = wg.shape[1]
    rows = pl.BlockSpec((tm, WIDTH_B), lambda i, f: (i, 0))
    full = lambda a: pl.BlockSpec(a.shape, lambda i, f: (0,) * a.ndim)
    consts = [ln_w, ln_b, ones_bd, w_out, ffn_norm.reshape(1, D)]
    return pl.pallas_call(
        _even_post_ffn_kernel,
        grid=(S // tm, F // tf),
        in_specs=[pl.BlockSpec((tm, D), lambda i, f: (i, 0)),
                  rows, rows, rows, rows, rows] + [full(a) for a in consts] + [
            pl.BlockSpec((D, tf), lambda i, f: (0, f)),
            pl.BlockSpec((D, tf), lambda i, f: (0, f)),
            pl.BlockSpec((tf, D), lambda i, f: (f, 0)),
        ],
        out_specs=pl.BlockSpec((tm, D), lambda i, f: (i, 0)),
        out_shape=jax.ShapeDtypeStruct((S, D), F32),
        scratch_shapes=[pltpu.VMEM((tm, D), BF16)],
        compiler_params=_params("parallel", "arbitrary"),
        name="even_post_ffn",
    )(x, y_a, y0, y1, gate, bonus, *consts, wg, wu, wd)


def _route_record(h, wr):
    h_hi, w_hi = h.astype(BF16), wr.astype(BF16)
    h_lo, w_lo = (h - h_hi.astype(F32)).astype(BF16), (wr - w_hi.astype(F32)).astype(BF16)
    logits = (jnp.dot(h_hi, w_hi, preferred_element_type=F32) + jnp.dot(h_lo, w_hi, preferred_element_type=F32)
              + jnp.dot(h_hi, w_lo, preferred_element_type=F32))
    lane = lax.broadcasted_iota(jnp.int32, logits.shape, 1)
    valid = lane < N_EXPERTS
    lg = jnp.where(valid, logits, NEG_INF)
    e = jnp.exp(lg - jnp.max(lg, axis=-1, keepdims=True))
    p = e / jnp.sum(e, axis=-1, keepdims=True)
    pm = jnp.where(valid, p, -1.0)
    m1 = jnp.max(pm, axis=-1, keepdims=True)
    i1 = jnp.min(jnp.where(pm == m1, lane, 128), axis=-1, keepdims=True)
    pm2 = jnp.where(lane == i1, -1.0, pm)
    m2 = jnp.max(pm2, axis=-1, keepdims=True)
    i2 = jnp.min(jnp.where(pm2 == m2, lane, 128), axis=-1, keepdims=True)
    tot = m1 + m2
    g1, g2 = m1 / tot, m2 / tot
    out = jnp.where(lane == i1, g1, 0.0) + jnp.where(lane == i2, g2, 0.0)
    out = jnp.where(lane == ROUTE_LANE_I1, i1.astype(F32), out)
    out = jnp.where(lane == ROUTE_LANE_I2, i2.astype(F32), out)
    out = jnp.where(lane == ROUTE_LANE_G1, g1, out)
    return jnp.where(lane == ROUTE_LANE_G2, g2, out)


def _odd_post_kernel(x_ref, yc_ref, o0_ref, o1_ref, zs_ref, dnw_ref, ones_ref, wo_ref, fnw_ref, wr_ref, x_out, route_out):
    o = o0_ref[...] + o1_ref[...]
    ms = _group_sum(o * o, ones_ref[...]) * (1.0 / HEAD_DIM_D)
    y_d = o * lax.rsqrt(ms + NORM_EPS) * dnw_ref[...] * zs_ref[...]
    y = jnp.concatenate([yc_ref[...], y_d], axis=1)
    x = x_ref[...] + jnp.dot(y.astype(BF16), wo_ref[...], preferred_element_type=F32)
    x_out[...] = x
    route_out[...] = _route_record(_rms(x, fnw_ref[...]), wr_ref[...])


def odd_post(x, y_c, o0, o1, zs, dn_norm, ones_bd, w_out, ffn_norm, wr_pad, *, tm=512):
    S, D = x.shape
    rows = pl.BlockSpec((tm, WIDTH_D), lambda i: (i, 0))
    full = lambda a: pl.BlockSpec(a.shape, lambda i: (0,) * a.ndim)
    consts = [dn_norm, ones_bd, w_out, ffn_norm.reshape(1, D), wr_pad]
    return pl.pallas_call(
        _odd_post_kernel,
        grid=(S // tm,),
        in_specs=[pl.BlockSpec((tm, D), lambda i: (i, 0)), rows, rows, rows, rows] + [full(a) for a in consts],
        out_specs=[pl.BlockSpec((tm, D), lambda i: (i, 0)), pl.BlockSpec((tm, 128), lambda i: (i, 0))],
        out_shape=[jax.ShapeDtypeStruct((S, D), F32), jax.ShapeDtypeStruct((S, 128), F32)],
        compiler_params=_params("parallel"),
        name="odd_post",
    )(x, y_c, o0, o1, zs, *consts)


def _row_copy(src_hbm, src_row, dst_vmem, dst_row, sem):
    return pltpu.make_async_copy(src_hbm.at[pl.ds(src_row, 1)], dst_vmem.at[pl.ds(dst_row, 1)], sem)


def _moe_group_kernel(te_ref, tv_ref, tok_ref, x_hbm, nw_ref, wg_ref, wu_ref, wd_ref, o_ref, xbuf, h_scr, sem, *, tm, n_f):
    m = pl.program_id(0)
    f = pl.program_id(1)
    valid = tv_ref[m] > 0
    slot = m % 2
    part = tm // n_f

    @pl.when((f == 0) & (m == 0))
    def _():
        def start(j, carry):
            _row_copy(x_hbm, tok_ref[j], xbuf.at[0], j, sem.at[0]).start()
            return carry

        lax.fori_loop(0, tm, start, 0, unroll=8)

    fed = valid | ((m > 0) & (tv_ref[jnp.maximum(m - 1, 0)] > 0))

    @pl.when(fed & (f == 0))
    def _():
        pltpu.make_async_copy(x_hbm.at[pl.ds(0, tm)], xbuf.at[slot], sem.at[slot]).wait()

    @pl.when(valid & (f == 0))
    def _():
        h_scr[...] = _rms(xbuf[slot], nw_ref[...]).astype(BF16)

    @pl.when(valid)
    def _():
        nbase = (m + 1) * tm + f * part
        for j in range(part):
            _row_copy(x_hbm, tok_ref[nbase + j], xbuf.at[1 - slot], f * part + j, sem.at[1 - slot]).start()

    quarter = tm // MOE_ROW_SPLITS
    need = (tv_ref[m] + quarter - 1) // quarter
    for n_q in range(1, MOE_ROW_SPLITS + 1):
        rows = n_q * quarter

        @pl.when(need == n_q)
        def _(rows=rows):
            h = h_scr[0:rows, :]
            g = jnp.dot(h, wg_ref[...], preferred_element_type=F32)
            u = jnp.dot(h, wu_ref[...], preferred_element_type=F32)
            y = jnp.dot((_silu(g) * u).astype(BF16), wd_ref[...], preferred_element_type=F32)

            @pl.when(f == 0)
            def _():
                o_ref[0:rows, :] = y
                if rows < tm:
                    o_ref[rows:, :] = jnp.zeros((tm - rows, o_ref.shape[1]), o_ref.dtype)

            @pl.when(f != 0)
            def _():
                o_ref[0:rows, :] += y

    @pl.when(jnp.logical_not(valid) & (f == 0))
    def _():
        o_ref[...] = jnp.zeros_like(o_ref)


def moe_grouped_ffn(x, nw, tile_expert, tile_rows, src_tok, wg, wu, wd, *, tm, tf=1792):
    S, D = x.shape
    E, _, F = wg.shape
    n_tiles = tile_expert.shape[0]
    nf = F // tf

    def w_in(m, f, te, tv, tok):
        return (te[m], 0, jnp.where(tv[m] > 0, f, nf - 1))

    def w_out(m, f, te, tv, tok):
        return (te[m], jnp.where(tv[m] > 0, f, nf - 1), 0)

    grid_spec = pltpu.PrefetchScalarGridSpec(
        num_scalar_prefetch=3,
        grid=(n_tiles, nf),
        in_specs=[
            pl.BlockSpec(memory_space=pl.ANY),
            pl.BlockSpec((1, D), lambda m, f, te, tv, tok: (0, 0)),
            pl.BlockSpec((None, D, tf), w_in),
            pl.BlockSpec((None, D, tf), w_in),
            pl.BlockSpec((None, tf, D), w_out),
        ],
        out_specs=pl.BlockSpec((tm, D), lambda m, f, te, tv, tok: (m, 0)),
        scratch_shapes=[pltpu.VMEM((2, tm, D), F32), pltpu.VMEM((tm, D), BF16), pltpu.SemaphoreType.DMA((2,))],
    )
    return pl.pallas_call(
        functools.partial(_moe_group_kernel, tm=tm, n_f=nf),
        grid_spec=grid_spec,
        out_shape=jax.ShapeDtypeStruct((n_tiles * tm, D), F32),
        compiler_params=_params("arbitrary", "arbitrary"),
        name="moe_grouped_ffn",
    )(tile_expert, tile_rows, src_tok, x, nw.reshape(1, D), wg, wu, wd)


def _moe_combine_kernel(pos_ref, x_ref, rt_ref, ys_hbm, o_ref, buf, sem, *, tc, n_tok):
    i = pl.program_id(0)
    slot = i % 2

    def gather_tile(tile, into):
        base = tile * tc

        def start(j, carry):
            for k in range(TOP_K):
                _row_copy(ys_hbm, pos_ref[k * n_tok + base + j], buf.at[into, k], j, sem.at[into]).start()
            return carry

        lax.fori_loop(0, tc, start, 0, unroll=8)

    @pl.when(i == 0)
    def _():
        gather_tile(0, 0)

    @pl.when(i + 1 < pl.num_programs(0))
    def _():
        gather_tile(i + 1, 1 - slot)

    for k in range(TOP_K):
        pltpu.make_async_copy(ys_hbm.at[pl.ds(0, tc)], buf.at[slot, k], sem.at[slot]).wait()
    rt = rt_ref[...]
    o_ref[...] = (x_ref[...] + rt[:, ROUTE_LANE_G1:ROUTE_LANE_G1 + 1] * buf[slot, 0]
                  + rt[:, ROUTE_LANE_G2:ROUTE_LANE_G2 + 1] * buf[slot, 1])


def moe_combine(x, route, ys, pos, *, tc=512):
    S, D = x.shape
    grid_spec = pltpu.PrefetchScalarGridSpec(
        num_scalar_prefetch=1,
        grid=(S // tc,),
        in_specs=[
            pl.BlockSpec((tc, D), lambda i, pos: (i, 0)),
            pl.BlockSpec((tc, 128), lambda i, pos: (i, 0)),
            pl.BlockSpec(memory_space=pl.ANY),
        ],
        out_specs=pl.BlockSpec((tc, D), lambda i, pos: (i, 0)),
        scratch_shapes=[pltpu.VMEM((2, TOP_K, tc, D), F32), pltpu.SemaphoreType.DMA((2,))],
    )
    return pl.pallas_call(
        functools.partial(_moe_combine_kernel, tc=tc, n_tok=S),
        grid_spec=grid_spec,
        out_shape=jax.ShapeDtypeStruct((S, D), F32),
        compiler_params=_params("arbitrary"),
        name="moe_combine",
    )(pos, x, route, ys)


def moe_top2(x, nw, route, wg, wu, wd, *, tm=512):
    S, D = x.shape
    E = wg.shape[0]
    experts = jnp.concatenate([route[:, ROUTE_LANE_I1], route[:, ROUTE_LANE_I2]]).astype(jnp.int32)
    onehot = (experts[:, None] == jnp.arange(E, dtype=jnp.int32)).astype(jnp.int32)
    csum = jnp.cumsum(onehot, axis=0)
    rank = jnp.sum(onehot * csum, axis=1) - 1
    padded = (csum[-1] + tm - 1) // tm * tm
    ends = jnp.cumsum(padded)
    pos = (jnp.sum(onehot * (ends - padded), axis=1) + rank).astype(jnp.int32)
    n_tiles = TOP_K * S // tm + E + 1
    tokens = jnp.tile(jnp.arange(S, dtype=jnp.int32), TOP_K)
    src_tok = jnp.zeros((n_tiles * tm,), jnp.int32).at[pos].set(tokens)
    tile_start = jnp.arange(n_tiles, dtype=jnp.int32) * tm
    tile_expert = jnp.minimum(jnp.sum(tile_start[:, None] >= ends[None, :], axis=1), E - 1).astype(jnp.int32)
    seg_end = (ends - padded + csum[-1])[tile_expert]
    tile_rows = jnp.clip(seg_end - tile_start, 0, tm).astype(jnp.int32)
    ys = moe_grouped_ffn(x, nw, tile_expert, tile_rows, src_tok, wg, wu, wd, tm=tm)
    return moe_combine(x, route, ys, pos)


ATTN_RADIUS = 64
ATTN_BQ = 128
ATTN_TILE = ATTN_BQ * max(d for _, d in DILATION_PATTERNS)
ATTN_HALO = ATTN_RADIUS * max(d for _, d in DILATION_PATTERNS)
ATTN_BLOCKS_IN_FLIGHT = 4
assert all(w // (2 * d) == ATTN_RADIUS for w, d in DILATION_PATTERNS)


def _attn_kernel(q_ref, kp_ref, kc_ref, kn_ref, vp_ref, vc_ref, vn_ref, o_ref, kbuf, vbuf, m_s, l_s, a_s, *, n_tok):
    tile, bq, halo, rad = ATTN_TILE, ATTN_BQ, ATTN_HALO, ATTN_RADIUS
    kbuf[0:halo, :] = kp_ref[...]
    kbuf[halo:halo + tile, :] = kc_ref[...]
    kbuf[halo + tile:, :] = kn_ref[...]
    vbuf[0:halo, :] = vp_ref[...]
    vbuf[halo:halo + tile, :] = vc_ref[...]
    vbuf[halo + tile:, :] = vn_ref[...]

    t0 = pl.program_id(1) * tile
    qi = lax.broadcasted_iota(jnp.int32, (bq, bq + 2 * rad), 0)
    kj = lax.broadcasted_iota(jnp.int32, (bq, bq + 2 * rad), 1)
    band = (kj >= qi) & (kj <= qi + 2 * rad)
    lo_half = lax.broadcasted_iota(jnp.int32, (bq, 2 * HEAD_DIM), 1) < HEAD_DIM

    for n_branch, (_, d) in enumerate(sorted(DILATION_PATTERNS, key=lambda wd: -wd[1])):
        span = bq * d
        reps = max(1, ATTN_BLOCKS_IN_FLIGHT // d)

        def block(blk, carry, d=d, span=span, reps=reps, first=n_branch == 0):
            stride = None if d == 1 else d
            problems = [(rep, r) for rep in range(reps) for r in range(d)]

            def scores(rep, r):
                base = pl.multiple_of(blk * (reps * span), span) + rep * span
                q_rows = pl.ds(base + r, bq, stride=stride)
                k_rows = pl.ds(halo + base + (r - rad * d), bq + 2 * rad, stride=stride)
                q = q_ref[q_rows, :]
                kw = kbuf[k_rows, :].astype(BF16)
                tok = t0 + base + (r - rad * d) + d * kj
                mask = band & (tok >= 0) & (tok < n_tok)
                sc = [jnp.where(mask, _bdot_nt(jnp.where(own, q, 0.0), kw), NEG_INF)
                      for own in (lo_half, jnp.logical_not(lo_half))]
                return q_rows, k_rows, sc

            pending = scores(*problems[0])
            for i in range(len(problems)):
                q_rows, k_rows, sc_pair = pending
                if i + 1 < len(problems):
                    pending = scores(*problems[i + 1])
                vw = vbuf[k_rows, :].astype(BF16)
                halves = []
                for sc in sc_pair:
                    m_h = jnp.max(sc, axis=-1, keepdims=True)
                    p = jnp.exp(sc - m_h)
                    halves.append((m_h, jnp.sum(p, axis=-1, keepdims=True),
                                   jnp.dot(p.astype(BF16), vw, preferred_element_type=F32)))
                m_b, l_b, a_b = (jnp.where(lo_half, x0, x1) for x0, x1 in zip(*halves))
                if first:
                    m_n, l_n, a_n = m_b, l_b, a_b
                else:
                    m_o = m_s[q_rows, :]
                    m_n = jnp.maximum(m_o, m_b)
                    w_o = jnp.exp(m_o - m_n)
                    w_b = jnp.exp(m_b - m_n)
                    l_n = l_s[q_rows, :] * w_o + l_b * w_b
                    a_n = a_s[q_rows, :] * w_o + a_b * w_b
                m_s[q_rows, :] = m_n
                l_s[q_rows, :] = l_n
                a_s[q_rows, :] = a_n
            return carry

        lax.fori_loop(0, tile // (reps * span), block, 0)

    o_ref[...] = a_s[...] / l_s[...]


def dilated_attention(q, k, v):
    S, W = q.shape
    pair = 2 * HEAD_DIM
    per = ATTN_TILE // ATTN_HALO
    cur = pl.BlockSpec((ATTN_TILE, pair), lambda p, i: (i, p))
    prv = pl.BlockSpec((ATTN_HALO, pair), lambda p, i: (jnp.maximum(i * per - 1, 0), p))
    nxt = pl.BlockSpec((ATTN_HALO, pair), lambda p, i: (jnp.minimum((i + 1) * per, S // ATTN_HALO - 1), p))
    ext = ATTN_TILE + 2 * ATTN_HALO
    return pl.pallas_call(
        functools.partial(_attn_kernel, n_tok=S),
        grid=(W // pair, S // ATTN_TILE),
        in_specs=[cur, prv, cur, nxt, prv, cur, nxt],
        out_specs=cur,
        out_shape=jax.ShapeDtypeStruct((S, W), F32),
        scratch_shapes=[pltpu.VMEM((ext, pair), F32), pltpu.VMEM((ext, pair), F32)]
        + [pltpu.VMEM((ATTN_TILE, pair), F32)] * 3,
        compiler_params=_params("parallel", "arbitrary"),
        name="dilated_attention",
    )(q, k, k, k, v, v, v)


def _tri_masks(z):
    row = lax.broadcasted_iota(jnp.int32, (CHUNK, CHUNK), 0)
    col = lax.broadcasted_iota(jnp.int32, (CHUNK, CHUNK), 1)
    if z == 0:
        return col <= row, col < row
    return col >= row, col > row


SUB_CHUNKS = 2
STEP_ROWS = SUB_CHUNKS * CHUNK


def _sub_rows(z, j):
    i = j if z == 0 else SUB_CHUNKS - 1 - j
    return slice(i * CHUNK, (i + 1) * CHUNK)


def _neumann_solve(ns, xs):
    steps = CHUNK.bit_length() - 1
    for i in range(steps):
        xs = [x + _bdot(n, x) for n, x in zip(ns, xs)]
        if i + 1 < steps:
            ns = [_bdot(n, n) for n in ns]
    return xs


def _rwkv_chunk_kernel(*refs, n_cast):
    ins = (refs[0:6], refs[6:12])
    cast_in = refs[12:12 + n_cast]
    y_refs = refs[12 + n_cast:14 + n_cast]
    cast_out = refs[14 + n_cast:14 + 2 * n_cast]
    state = refs[14 + 2 * n_cast]
    for src, dst in zip(cast_in, cast_out):
        dst[...] = src[...].astype(BF16)

    @pl.when(pl.program_id(0) == 0)
    def _():
        state[...] = jnp.zeros_like(state)

    chains = [(j, z, h) for j in range(SUB_CHUNKS) for z in range(2) for h in range(N_HEADS_B)]
    masks = [_tri_masks(z) for z in range(2)]
    prep = {}
    for j in range(SUB_CHUNKS):
        for z in range(2):
            rows = _sub_rows(z, j)
            r_ref, v_ref, a_ref, lw_ref, k_ref, b_ref = ins[z]
            tri = jnp.where(masks[z][0], 1.0, 0.0).astype(BF16)
            lw = lw_ref[rows, :]
            hi, mid, lo = _split3(lw)
            cum = (jnp.dot(tri, hi, preferred_element_type=F32) + jnp.dot(tri, mid, preferred_element_type=F32)
                   + jnp.dot(tri, lo, preferred_element_type=F32))
            tot = jnp.sum(lw, axis=0, keepdims=True)
            e_neg = jnp.exp(-cum)
            e_end = jnp.exp(tot - cum)
            k = k_ref[rows, :]
            b = b_ref[rows, :]
            prep[j, z] = dict(rt=r_ref[rows, :] * jnp.exp(cum), at=a_ref[rows, :] * jnp.exp(cum - lw), kt=k * e_neg,
                              bt=b * e_neg, kh=k * e_end, bh=b * e_end, dw=jnp.exp(tot), v=v_ref[rows, :])

    def part(name, c):
        j, z, h = c
        return prep[j, z][name][:, h * HEAD_DIM:(h + 1) * HEAD_DIM]

    xs = [_bdot_nt(jnp.concatenate([part("at", c), part("rt", c)], axis=0),
                   jnp.concatenate([part("bt", c), part("kt", c)], axis=0)) for c in chains]
    a_ab = [jnp.where(masks[c[1]][1], x[:CHUNK, :CHUNK], 0.0) for c, x in zip(chains, xs)]
    a_rb = [jnp.where(masks[c[1]][0], x[CHUNK:, :CHUNK], 0.0) for c, x in zip(chains, xs)]
    cys = [_bdot(jnp.concatenate([jnp.where(masks[c[1]][1], x[:CHUNK, CHUNK:], 0.0),
                                  jnp.where(masks[c[1]][0], x[CHUNK:, CHUNK:], 0.0)], axis=0), part("v", c))
           for c, x in zip(chains, xs)]
    pqs = _neumann_solve(a_ab, [jnp.concatenate([part("at", c), cy[:CHUNK]], axis=1) for c, cy in zip(chains, cys)])
    kvs = [_bdot_tn(part("v", c), part("kh", c)) for c in chains]
    per = 2 * N_HEADS_B
    s_cur = [state[z, h] for _, z, h in chains[:per]]
    for j in range(SUB_CHUNKS):
        sel = slice(j * per, (j + 1) * per)
        cs = chains[sel]
        us = [_bdot_nt(pq[:, :HEAD_DIM], s) + pq[:, HEAD_DIM:] for pq, s in zip(pqs[sel], s_cur)]
        ys = [_bdot_nt(part("rt", c), s) for c, s in zip(cs, s_cur)]
        ys = [y + _bdot(arb, u) + cy[CHUNK:] for y, arb, u, cy in zip(ys, a_rb[sel], us, cys[sel])]
        s_cur = [s * part("dw", c) + _bdot_tn(u, part("bh", c)) + kv for c, s, u, kv in zip(cs, s_cur, us, kvs[sel])]
        for (_, z, h), y in zip(cs, ys):
            y_refs[z][_sub_rows(z, j), h * HEAD_DIM:(h + 1) * HEAD_DIM] = y
    for (_, z, h), s in zip(chains[:per], s_cur):
        state[z, h] = s


def rwkv7_scan(r, v, a, lw, k, b, cast_through=()):
    S, C = r.shape
    n = S // STEP_ROWS
    fwd = pl.BlockSpec((STEP_ROWS, C), lambda c: (c, 0))
    bwd = pl.BlockSpec((STEP_ROWS, C), lambda c: (n - 1 - c, 0))
    cast_specs = [pl.BlockSpec((w.shape[0] // n, w.shape[1]), lambda c: (c, 0)) for w in cast_through]
    y0, y1, *cast = pl.pallas_call(
        functools.partial(_rwkv_chunk_kernel, n_cast=len(cast_through)),
        grid=(n,),
        in_specs=[fwd] * 6 + [bwd] * 6 + cast_specs,
        out_specs=[fwd, bwd] + cast_specs,
        out_shape=[jax.ShapeDtypeStruct((S, C), F32)] * 2 + [jax.ShapeDtypeStruct(w.shape, BF16) for w in cast_through],
        scratch_shapes=[pltpu.VMEM((2, N_HEADS_B, HEAD_DIM, HEAD_DIM), F32)],
        compiler_params=_params("arbitrary"),
        name="rwkv7_scan",
    )(r, v, a, lw[0], k[0], b[0], r, v, a, lw[1], k[1], b[1], *cast_through)
    return y0, y1, cast


def _dn_chunk_kernel(*refs):
    ins = (refs[0:6], refs[6:12])
    o_refs = refs[12:14]
    state = refs[14]

    @pl.when(pl.program_id(0) == 0)
    def _():
        state[...] = jnp.zeros_like(state)

    chains = [(j, z, h) for j in range(SUB_CHUNKS) for z in range(2) for h in range(N_HEADS_D)]
    masks = [_tri_masks(z) for z in range(2)]
    nt = (((1,), (1,)), ((), ()))
    gcs, decays, betas, g_lasts = [], [], [], []
    for j in range(SUB_CHUNKS):
        for z in range(2):
            rows = _sub_rows(z, j)
            _, _, _, bcol_ref, gcol_ref, grow_ref = ins[z]
            incl = masks[z][0]
            tri = jnp.where(incl, 1.0, 0.0).astype(BF16)
            c_hi, c_mid, c_lo = _split3(gcol_ref[rows, :])
            gc_cols = (jnp.dot(tri, c_hi, preferred_element_type=F32) + jnp.dot(tri, c_mid, preferred_element_type=F32)
                       + jnp.dot(tri, c_lo, preferred_element_type=F32))
            r_hi, r_mid, r_lo = _split3(grow_ref[rows.start // CHUNK])
            gc_rows = (lax.dot_general(r_hi, tri, nt, preferred_element_type=F32)
                       + lax.dot_general(r_mid, tri, nt, preferred_element_type=F32)
                       + lax.dot_general(r_lo, tri, nt, preferred_element_type=F32))
            last = CHUNK - 1 if z == 0 else 0
            bcol = bcol_ref[rows, :]
            for h in range(N_HEADS_D):
                idx = z * N_HEADS_D + h
                gc = gc_cols[:, idx:idx + 1]
                diff = gc - gc_rows[idx:idx + 1, :]
                gcs.append(gc)
                decays.append(jnp.where(incl, jnp.exp(jnp.where(incl, diff, 0.0)), 0.0))
                betas.append(bcol[:, idx:idx + 1])
                g_lasts.append(gc[last:last + 1, :])

    def part(i, c):
        j, z, h = c
        return ins[z][i][_sub_rows(z, j), h * HEAD_DIM_D:(h + 1) * HEAD_DIM_D]

    qs = [part(0, c) for c in chains]
    ks = [part(1, c) for c in chains]
    vs = [part(2, c) for c in chains]
    kbs = [k * beta for k, beta in zip(ks, betas)]
    e_gcs = [jnp.exp(gc) for gc in gcs]
    kqs = [_bdot_nt(jnp.concatenate([kb, q], axis=0), k) for kb, q, k in zip(kbs, qs, ks)]
    n_mats = [jnp.where(masks[c[1]][1], -(kq[:CHUNK] * dc), 0.0) for c, kq, dc in zip(chains, kqs, decays)]
    attns = [kq[CHUNK:] * dc for kq, dc in zip(kqs, decays)]
    uks = _neumann_solve(n_mats, [jnp.concatenate([v * beta, kb * e], axis=1)
                                  for v, beta, kb, e in zip(vs, betas, kbs, e_gcs)])
    per = 2 * N_HEADS_D
    s_cur = [state[z, h] for _, z, h in chains[:per]]
    for j in range(SUB_CHUNKS):
        sel = slice(j * per, (j + 1) * per)
        us = [uk[:, :HEAD_DIM_D] - _bdot(uk[:, HEAD_DIM_D:], s) for uk, s in zip(uks[sel], s_cur)]
        os_ = [_bdot(q * e, s) for q, e, s in zip(qs[sel], e_gcs[sel], s_cur)]
        os_ = [o + _bdot(attn, u) for o, attn, u in zip(os_, attns[sel], us)]
        s_cur = [s * jnp.exp(gl) + _bdot_tn(k * jnp.exp(gl - gc), u)
                 for s, gl, k, gc, u in zip(s_cur, g_lasts[sel], ks[sel], gcs[sel], us)]
        for (_, z, h), o in zip(chains[sel], os_):
            o_refs[z][_sub_rows(z, j), h * HEAD_DIM_D:(h + 1) * HEAD_DIM_D] = o
    for (_, z, h), s in zip(chains[:per], s_cur):
        state[z, h] = s


def deltanet_scan(q, k, v, beta, g):
    S, C = q.shape
    n = S // STEP_ROWS
    nz = 2 * N_HEADS_D
    g_rows = g.reshape(S // CHUNK, CHUNK, nz).transpose(0, 2, 1)

    def specs(idx):
        wide = pl.BlockSpec((STEP_ROWS, C), lambda c: (idx(c), 0))
        col = pl.BlockSpec((STEP_ROWS, nz), lambda c: (idx(c), 0))
        row = pl.BlockSpec((SUB_CHUNKS, nz, CHUNK), lambda c: (idx(c), 0, 0))
        return [wide, wide, wide, col, col, row], wide

    in_f, out_f = specs(lambda c: c)
    in_b, out_b = specs(lambda c: n - 1 - c)
    o0, o1 = pl.pallas_call(
        _dn_chunk_kernel,
        grid=(n,),
        in_specs=in_f + in_b,
        out_specs=[out_f, out_b],
        out_shape=[jax.ShapeDtypeStruct((S, C), F32)] * 2,
        scratch_shapes=[pltpu.VMEM((2, N_HEADS_D, HEAD_DIM_D, HEAD_DIM_D), F32)],
        compiler_params=_params("arbitrary"),
        name="deltanet_scan",
    )(q, k, v, beta, g, g_rows, q, k, v, beta, g, g_rows)
    return o0, o1


def _rope_tables(positions):
    half = ROPE_DIM // 2
    dim = jnp.arange(2 * HEAD_DIM) % HEAD_DIM
    freq = jnp.where(dim < ROPE_DIM, jnp.power(ROPE_THETA, -(dim % half).astype(F32) / half), 0.0)
    sign = jnp.where(dim < half, -1.0, 1.0)
    ang = positions[:, None].astype(F32) * freq
    return jnp.cos(ang), sign * jnp.sin(ang)


def _ones_block_diag(width, group):
    idx = jnp.arange(width) // group
    return (idx[:, None] == idx[None, :]).astype(BF16)


def _block_diag(blocks):
    rows = sum(b.shape[0] for b in blocks)
    cols = sum(b.shape[1] for b in blocks)
    out = jnp.zeros((rows, cols), blocks[0].dtype)
    r = c = 0
    for b in blocks:
        out = lax.dynamic_update_slice(out, b, (r, c))
        r += b.shape[0]
        c += b.shape[1]
    return out


def _even_layer(x, positions, mix_norm, w_in, q_norm, k_norm, shift_mu, lora_mu, w0, w1, w2, a0, a1, a2, g1, g2,
                k_k, k_a, r_k, ln_w, ln_b, w_out, ffn_norm, ffn_gate, ffn_up, ffn_down, cast_through=()):
    row = lambda t: t.reshape(1, -1)
    cos_t, sin_t = _rope_tables(positions)
    ones_bd = _ones_block_diag(WIDTH_B, HEAD_DIM)
    lora_in = jnp.concatenate([w1[0], w1[1], a1[0], a1[1], g1], axis=1).astype(BF16)
    lora_out = _block_diag([w2[0], w2[1], a2[0], a2[1], g2]).astype(BF16)
    (q, k, v, r, vb, a_vec, lw0, lw1, k0, k1, b0, b1, gate, bonus) = even_prep(
        x, mix_norm, w_in.astype(BF16), row(jnp.tile(q_norm, N_HEADS_A)), row(jnp.tile(k_norm, N_HEADS_A)), cos_t, sin_t,
        shift_mu, lora_mu, lora_in, lora_out, w0, a0, row(k_k), row(k_a), row(r_k), ones_bd)
    y_a = dilated_attention(q, k, v)
    y0, y1, cast = rwkv7_scan(r, vb, a_vec, (lw0, lw1), (k0, k1), (b0, b1),
                              [w.reshape(-1, w.shape[-1]) for w in cast_through])
    cast = [c.reshape(w.shape) for c, w in zip(cast, cast_through)]
    x = even_post_ffn(x, y_a, y0, y1, gate, bonus, row(ln_w), row(ln_b), ones_bd, w_out.astype(BF16), ffn_norm,
                      ffn_gate.astype(BF16), ffn_up.astype(BF16), ffn_down.astype(BF16))
    return x, cast


def _odd_layer(x, mix_norm, w_in, conv_c, conv_dn, A_log, dt_bias, dn_norm, w_out, ffn_norm, router, moe_gate, moe_up, moe_down):
    n_in = w_in.shape[1]
    n_pad = -(-n_in // 128) * 128
    w_in_p = jnp.pad(w_in, ((0, 0), (0, n_pad - n_in))).astype(BF16)
    nz = 2 * N_HEADS_D
    neg_a = jnp.zeros((1, 128), F32).at[0, nz:2 * nz].set(-jnp.exp(A_log.reshape(-1)))
    dt_b = jnp.zeros((1, 128), F32).at[0, nz:2 * nz].set(dt_bias.reshape(-1))
    ones_bd = _ones_block_diag(WIDTH_D, HEAD_DIM_D)
    y_c, q, k, v, zs, bg = odd_prep(x, mix_norm, w_in_p, conv_c, conv_dn, neg_a, dt_b, ones_bd)
    o0, o1 = deltanet_scan(q, k, v, bg[:, :nz], bg[:, nz:2 * nz])
    wr_pad = jnp.pad(router, ((0, 0), (0, 128 - N_EXPERTS)))
    x, route = odd_post(x, y_c, o0, o1, zs, jnp.tile(dn_norm, N_HEADS_D).reshape(1, -1), ones_bd, w_out.astype(BF16),
                        ffn_norm, wr_pad)
    return moe_top2(x, ffn_norm, route, moe_gate.astype(BF16), moe_up.astype(BF16), moe_down.astype(BF16))


def kernel(x, positions, ev_mix_norm, ev_w_in, ev_q_norm, ev_k_norm, ev_shift_mu, ev_lora_mu, ev_w0, ev_w1, ev_w2, ev_a0, ev_a1, ev_a2, ev_g1, ev_g2, ev_k_k, ev_k_a, ev_r_k, ev_ln_w, ev_ln_b, ev_w_out, ev_ffn_norm, ev_ffn_gate, ev_ffn_up, ev_ffn_down, od_mix_norm, od_w_in, od_conv_c, od_conv_dn, od_A_log, od_dt_bias, od_dn_norm, od_w_out, od_ffn_norm, od_router, od_moe_gate, od_moe_up, od_moe_down):
    B, S, D = x.shape
    assert B == 1
    xs = x.reshape(S, D)
    pos = positions.reshape(S)
    n_layers = ev_mix_norm.shape[0] + od_mix_norm.shape[0]
    for layer in range(n_layers):
        i = layer // 2
        if layer % 2 == 0:
            nxt = (od_moe_gate[i], od_moe_up[i], od_moe_down[i]) if layer + 1 < n_layers else ()
            xs, moe_w = _even_layer(xs, pos, ev_mix_norm[i], ev_w_in[i], ev_q_norm[i], ev_k_norm[i], ev_shift_mu[i],
                                    ev_lora_mu[i], ev_w0[i], ev_w1[i], ev_w2[i], ev_a0[i], ev_a1[i], ev_a2[i], ev_g1[i],
                                    ev_g2[i], ev_k_k[i], ev_k_a[i], ev_r_k[i], ev_ln_w[i], ev_ln_b[i], ev_w_out[i],
                                    ev_ffn_norm[i], ev_ffn_gate[i], ev_ffn_up[i], ev_ffn_down[i], cast_through=nxt)
        else:
            xs = _odd_layer(xs, od_mix_norm[i], od_w_in[i], od_conv_c[i], od_conv_dn[i], od_A_log[i], od_dt_bias[i],
                            od_dn_norm[i], od_w_out[i], od_ffn_norm[i], od_router[i], *moe_w)
    return xs.reshape(B, S, D)
```

```python
import functools

import jax
import jax.numpy as jnp
from jax import lax
from jax.experimental import pallas as pl
from jax.experimental.pallas import tpu as pltpu

F32 = jnp.float32
BF16 = jnp.bfloat16

HEAD_DIM = 64
N_HEADS_A = 8
WIDTH_A = N_HEADS_A * HEAD_DIM
DILATION_PATTERNS = ((128, 1), (512, 4), (2048, 16))
ROPE_DIM = HEAD_DIM // 4
ROPE_THETA = 500000.0
N_HEADS_B = 8
WIDTH_B = N_HEADS_B * HEAD_DIM
RWKV_LN_EPS = 64e-5
WIDTH_C = 512
N_HEADS_D = 4
HEAD_DIM_D = 128
WIDTH_D = N_HEADS_D * HEAD_DIM_D
CHUNK = 64
N_EXPERTS = 8
TOP_K = 2
ROUTE_LANE_I1, ROUTE_LANE_I2, ROUTE_LANE_G1, ROUTE_LANE_G2 = 8, 9, 10, 11
MOE_ROW_SPLITS = 4
NORM_EPS = 1e-6
NEG_INF = -1e30

V7X_VMEM_LIMIT_BYTES = 56 * 1024 * 1024


def _params(*sem):
    return pltpu.CompilerParams(dimension_semantics=sem, vmem_limit_bytes=V7X_VMEM_LIMIT_BYTES)


def _bdot(a, b):
    return jnp.dot(a.astype(BF16), b.astype(BF16), preferred_element_type=F32)


def _bdot_nt(a, b):
    return lax.dot_general(a.astype(BF16), b.astype(BF16), (((1,), (1,)), ((), ())), preferred_element_type=F32)


def _bdot_tn(a, b):
    return lax.dot_general(a.astype(BF16), b.astype(BF16), (((0,), (0,)), ((), ())), preferred_element_type=F32)


def _split3(x):
    hi = x.astype(BF16)
    r1 = x - hi.astype(F32)
    mid = r1.astype(BF16)
    lo = (r1 - mid.astype(F32)).astype(BF16)
    return hi, mid, lo


def _rms(x, w):
    return x * lax.rsqrt(jnp.mean(x * x, axis=-1, keepdims=True) + NORM_EPS) * w


HALO = 8


def _group_sum(x, ones_bd):
    hi = x.astype(BF16)
    lo = (x - hi.astype(F32)).astype(BF16)
    return jnp.dot(hi, ones_bd, preferred_element_type=F32) + jnp.dot(lo, ones_bd, preferred_element_type=F32)


def _silu(x):
    return x * jax.nn.sigmoid(x)


def _halo_specs(tm, n_rows, width):
    per = tm // HALO
    cur = pl.BlockSpec((tm, width), lambda i: (i, 0))
    prv = pl.BlockSpec((HALO, width), lambda i: (jnp.maximum(i * per - 1, 0), 0))
    nxt = pl.BlockSpec((HALO, width), lambda i: (jnp.minimum((i + 1) * per, n_rows // HALO - 1), 0))
    return cur, prv, nxt


def _make_shifts(tm, n_rows):
    row = pl.program_id(0) * tm + lax.broadcasted_iota(jnp.int32, (tm, 1), 0)
    first, last = row == 0, row == n_rows - 1
    n_ext = tm + 2 * HALO

    def shifts(t):
        prev = jnp.where(first, 0.0, pltpu.roll(t, 1, 0)[HALO:HALO + tm])
        nxt = jnp.where(last, 0.0, pltpu.roll(t, n_ext - 1, 0)[HALO:HALO + tm])
        return prev, t[HALO:HALO + tm], nxt

    return shifts


def _even_prep_kernel(xc_ref, xp_ref, xn_ref, nw_ref, win_ref, qn_ref, kn_ref, cos_ref, sin_ref, smu_ref, lmu_ref,
                      lin_ref, lout_ref, w0_ref, a0_ref, kk_ref, ka_ref, rk_ref, ones_ref,
                      q_out, k_out, v_out, r_out, vb_out, a_out, lw0_out, lw1_out, k0_out, k1_out, b0_out, b1_out,
                      gate_out, bonus_out, *, tm, n_rows):
    shifts = _make_shifts(tm, n_rows)
    he = _rms(jnp.concatenate([xp_ref[...], xc_ref[...], xn_ref[...]], axis=0), nw_ref[...])
    proj = jnp.dot(he.astype(BF16), win_ref[...], preferred_element_type=F32)
    ones = ones_ref[...]
    lane = lax.broadcasted_iota(jnp.int32, (tm, WIDTH_A), 1) % HEAD_DIM

    n_pairs = WIDTH_A // cos_ref.shape[1]
    cos_t = jnp.concatenate([cos_ref[...]] * n_pairs, axis=1)
    sin_t = jnp.concatenate([sin_ref[...]] * n_pairs, axis=1)

    def head_rms_rope(t, w):
        t = t * lax.rsqrt(_group_sum(t * t, ones) * (1.0 / HEAD_DIM) + NORM_EPS) * w
        half = ROPE_DIM // 2
        swapped = jnp.where(lane < half, pltpu.roll(t, WIDTH_A - half, 1), pltpu.roll(t, half, 1))
        return t * cos_t + swapped * sin_t

    cur = proj[HALO:HALO + tm]
    q = head_rms_rope(cur[:, :WIDTH_A], qn_ref[...]) * HEAD_DIM ** -0.5
    k = head_rms_rope(cur[:, WIDTH_A:2 * WIDTH_A], kn_ref[...])
    v = cur[:, 2 * WIDTH_A:3 * WIDTH_A]
    q_out[...] = q
    k_out[...] = k
    v_out[...] = v

    p_prev, p_cur, p_next = shifts(proj[:, 3 * WIDTH_A:])
    smu = smu_ref[...]
    rkv = p_cur + smu[0:1] * (p_prev - p_cur) + smu[1:2] * (p_next - p_cur)
    r, kin, vb = rkv[:, :WIDTH_B], rkv[:, WIDTH_B:2 * WIDTH_B], rkv[:, 2 * WIDTH_B:]
    h_prev, h_cur, h_next = shifts(he)
    lmu = lmu_ref[...]
    hx = h_cur + lmu[0:1] * (h_prev - h_cur) + lmu[1:2] * (h_next - h_cur)
    l1 = jnp.dot(hx.astype(BF16), lin_ref[...], preferred_element_type=F32)
    l1 = jnp.concatenate([jnp.tanh(l1[:, :128]), l1[:, 128:256], jax.nn.sigmoid(l1[:, 256:])], axis=1)
    l2 = jnp.dot(l1.astype(BF16), lout_ref[...], preferred_element_type=F32)
    w0, a0 = w0_ref[...], a0_ref[...]
    kk = kin * kk_ref[...]
    kk = kk * lax.rsqrt(_group_sum(kk * kk, ones) + 1e-6)
    kdirs = []
    for z, (lw_out, k_out_z, b_out_z) in enumerate(((lw0_out, k0_out, b0_out), (lw1_out, k1_out, b1_out))):
        w_pre = l2[:, z * WIDTH_B:(z + 1) * WIDTH_B] + w0[z:z + 1]
        lw_out[...] = -jnp.exp(-0.5) * jax.nn.sigmoid(w_pre)
        iclr = jax.nn.sigmoid(l2[:, (2 + z) * WIDTH_B:(3 + z) * WIDTH_B] + a0[z:z + 1])
        kdir = kin * (1.0 + (iclr - 1.0) * ka_ref[...])
        k_out_z[...] = kdir
        b_out_z[...] = kk * iclr
        kdirs.append(kdir)
    r_out[...] = r
    vb_out[...] = vb
    a_out[...] = -kk
    gate_out[...] = l2[:, 4 * WIDTH_B:]
    bonus_out[...] = _group_sum(r * (kdirs[0] + kdirs[1]) * rk_ref[...], ones) * vb


def even_prep(x, mix_norm, w_in, q_norm, k_norm, cos_t, sin_t, shift_mu, lora_mu, lora_in, lora_out, w0, a0, k_k, k_a,
              r_k, ones_bd, *, tm=256):
    S, D = x.shape
    cur, prv, nxt = _halo_specs(tm, S, D)
    full = lambda a: pl.BlockSpec(a.shape, lambda i: (0,) * a.ndim)
    rows = pl.BlockSpec((tm, WIDTH_B), lambda i: (i, 0))
    consts = [mix_norm.reshape(1, D), w_in, q_norm, k_norm]
    consts2 = [shift_mu, lora_mu, lora_in, lora_out, w0, a0, k_k, k_a, r_k, ones_bd]
    return pl.pallas_call(
        functools.partial(_even_prep_kernel, tm=tm, n_rows=S),
        grid=(S // tm,),
        in_specs=[cur, prv, nxt] + [full(a) for a in consts]
        + [pl.BlockSpec((tm, cos_t.shape[1]), lambda i: (i, 0))] * 2 + [full(a) for a in consts2],
        out_specs=[rows] * 14,
        out_shape=[jax.ShapeDtypeStruct((S, WIDTH_B), F32)] * 14,
        compiler_params=_params("parallel"),
        name="even_prep",
    )(x, x, x, *consts, cos_t, sin_t, *consts2)


def _odd_prep_kernel(xc_ref, xp_ref, xn_ref, nw_ref, win_ref, cc_ref, cdn_ref, nega_ref, dtb_ref, ones_ref,
                     yc_out, q_out, k_out, v_out, zs_out, bg_out, *, tm, n_rows):
    shifts = _make_shifts(tm, n_rows)
    he = _rms(jnp.concatenate([xp_ref[...], xc_ref[...], xn_ref[...]], axis=0), nw_ref[...])
    proj = jnp.dot(he.astype(BF16), win_ref[...], preferred_element_type=F32)
    cur = proj[HALO:HALO + tm]

    def conv3(t, w):
        prev, mid, nxt = shifts(t)
        return w[0:1] * prev + w[1:2] * mid + w[2:3] * nxt

    o_dn = 3 * WIDTH_C
    yc_out[...] = cur[:, :WIDTH_C] * conv3(proj[:, WIDTH_C:2 * WIDTH_C] * proj[:, 2 * WIDTH_C:o_dn], cc_ref[...])
    qkv = _silu(conv3(proj[:, o_dn:o_dn + 3 * WIDTH_D], cdn_ref[...]))
    ones = ones_ref[...]
    l2n = lambda t: t * lax.rsqrt(_group_sum(t * t, ones) + 1e-6)
    q_out[...] = l2n(qkv[:, :WIDTH_D]) * HEAD_DIM_D ** -0.5
    k_out[...] = l2n(qkv[:, WIDTH_D:2 * WIDTH_D])
    v_out[...] = qkv[:, 2 * WIDTH_D:]
    zs_out[...] = _silu(cur[:, o_dn + 3 * WIDTH_D:o_dn + 4 * WIDTH_D])
    tail = cur[:, o_dn + 4 * WIDTH_D:]
    t = tail + dtb_ref[...]
    softplus = jnp.maximum(t, 0.0) + jnp.log(1.0 + jnp.exp(-jnp.abs(t)))
    lane = lax.broadcasted_iota(jnp.int32, tail.shape, 1)
    bg_out[...] = jnp.where(lane < 2 * N_HEADS_D, jax.nn.sigmoid(tail), nega_ref[...] * softplus)


def odd_prep(x, mix_norm, w_in_pad, conv_c, conv_dn, neg_a, dt_b, ones_bd, *, tm=256):
    S, D = x.shape
    cur, prv, nxt = _halo_specs(tm, S, D)
    full = lambda a: pl.BlockSpec(a.shape, lambda i: (0,) * a.ndim)
    rows = pl.BlockSpec((tm, WIDTH_D), lambda i: (i, 0))
    consts = [mix_norm.reshape(1, D), w_in_pad, conv_c, conv_dn, neg_a, dt_b, ones_bd]
    return pl.pallas_call(
        functools.partial(_odd_prep_kernel, tm=tm, n_rows=S),
        grid=(S // tm,),
        in_specs=[cur, prv, nxt] + [full(a) for a in consts],
        out_specs=[rows] * 5 + [pl.BlockSpec((tm, 128), lambda i: (i, 0))],
        out_shape=[jax.ShapeDtypeStruct((S, WIDTH_D), F32)] * 5 + [jax.ShapeDtypeStruct((S, 128), F32)],
        compiler_params=_params("parallel"),
        name="odd_prep",
    )(x, x, x, *consts)


def _even_post_ffn_kernel(x_ref, ya_ref, y0_ref, y1_ref, gate_ref, bonus_ref, lnw_ref, lnb_ref, ones_ref, wo_ref,
                          fnw_ref, wg_ref, wu_ref, wd_ref, o_ref, h_scr):
    @pl.when(pl.program_id(1) == 0)
    def _():
        ones = ones_ref[...]
        yf = y0_ref[...] + y1_ref[...]
        dev = yf - _group_sum(yf, ones) * (1.0 / HEAD_DIM)
        var = _group_sum(dev * dev, ones) * (1.0 / HEAD_DIM)
        yn = dev * lax.rsqrt(var + RWKV_LN_EPS) * lnw_ref[...] + lnb_ref[...]
        y_b = (yn + bonus_ref[...]) * gate_ref[...]
        y = jnp.concatenate([ya_ref[...], y_b], axis=1)
        x = x_ref[...] + jnp.dot(y.astype(BF16), wo_ref[...], preferred_element_type=F32)
        h_scr[...] = _rms(x, fnw_ref[...]).astype(BF16)
        o_ref[...] = x

    h = h_scr[...]
    g = jnp.dot(h, wg_ref[...], preferred_element_type=F32)
    u = jnp.dot(h, wu_ref[...], preferred_element_type=F32)
    o_ref[...] += jnp.dot((_silu(g) * u).astype(BF16), wd_ref[...], preferred_element_type=F32)


def even_post_ffn(x, y_a, y0, y1, gate, bonus, ln_w, ln_b, ones_bd, w_out, ffn_norm, wg, wu, wd, *, tm=512, tf=1408):
    S, D = x.shape
    F = wg.shape[1]
    rows = pl.BlockSpec((tm, WIDTH_B), lambda i, f: (i, 0))
    full = lambda a: pl.BlockSpec(a.shape, lambda i, f: (0,) * a.ndim)
    consts = [ln_w, ln_b, ones_bd, w_out, ffn_norm.reshape(1, D)]
    return pl.pallas_call(
        _even_post_ffn_kernel,
        grid=(S // tm, F // tf),
        in_specs=[pl.BlockSpec((tm, D), lambda i, f: (i, 0)),
                  rows, rows, rows, rows, rows] + [full(a) for a in consts] + [
            pl.BlockSpec((D, tf), lambda i, f: (0, f)),
            pl.BlockSpec((D, tf), lambda i, f: (0, f)),
            pl.BlockSpec((tf, D), lambda i, f: (f, 0)),
        ],
        out_specs=pl.BlockSpec((tm, D), lambda i, f: (i, 0)),
        out_shape=jax.ShapeDtypeStruct((S, D), F32),
        scratch_shapes=[pltpu.VMEM((tm, D), BF16)],
        compiler_params=_params("parallel", "arbitrary"),
        name="even_post_ffn",
    )(x, y_a, y0, y1, gate, bonus, *consts, wg, wu, wd)


def _route_record(h, wr):
    h_hi, w_hi = h.astype(BF16), wr.astype(BF16)
    h_lo, w_lo = (h - h_hi.astype(F32)).astype(BF16), (wr - w_hi.astype(F32)).astype(BF16)
    logits = (jnp.dot(h_hi, w_hi, preferred_element_type=F32) + jnp.dot(h_lo, w_hi, preferred_element_type=F32)
              + jnp.dot(h_hi, w_lo, preferred_element_type=F32))
    lane = lax.broadcasted_iota(jnp.int32, logits.shape, 1)
    valid = lane < N_EXPERTS
    lg = jnp.where(valid, logits, NEG_INF)
    e = jnp.exp(lg - jnp.max(lg, axis=-1, keepdims=True))
    p = e / jnp.sum(e, axis=-1, keepdims=True)
    pm = jnp.where(valid, p, -1.0)
    m1 = jnp.max(pm, axis=-1, keepdims=True)
    i1 = jnp.min(jnp.where(pm == m1, lane, 128), axis=-1, keepdims=True)
    pm2 = jnp.where(lane == i1, -1.0, pm)
    m2 = jnp.max(pm2, axis=-1, keepdims=True)
    i2 = jnp.min(jnp.where(pm2 == m2, lane, 128), axis=-1, keepdims=True)
    tot = m1 + m2
    g1, g2 = m1 / tot, m2 / tot
    out = jnp.where(lane == i1, g1, 0.0) + jnp.where(lane == i2, g2, 0.0)
    out = jnp.where(lane == ROUTE_LANE_I1, i1.astype(F32), out)
    out = jnp.where(lane == ROUTE_LANE_I2, i2.astype(F32), out)
    out = jnp.where(lane == ROUTE_LANE_G1, g1, out)
    return jnp.where(lane == ROUTE_LANE_G2, g2, out)


def _odd_post_kernel(x_ref, yc_ref, o0_ref, o1_ref, zs_ref, dnw_ref, ones_ref, wo_ref, fnw_ref, wr_ref, x_out, route_out):
    o = o0_ref[...] + o1_ref[...]
    ms = _group_sum(o * o, ones_ref[...]) * (1.0 / HEAD_DIM_D)
    y_d = o * lax.rsqrt(ms + NORM_EPS) * dnw_ref[...] * zs_ref[...]
    y = jnp.concatenate([yc_ref[...], y_d], axis=1)
    x = x_ref[...] + jnp.dot(y.astype(BF16), wo_ref[...], preferred_element_type=F32)
    x_out[...] = x
    route_out[...] = _route_record(_rms(x, fnw_ref[...]), wr_ref[...])


def odd_post(x, y_c, o0, o1, zs, dn_norm, ones_bd, w_out, ffn_norm, wr_pad, *, tm=512):
    S, D = x.shape
    rows = pl.BlockSpec((tm, WIDTH_D), lambda i: (i, 0))
    full = lambda a: pl.BlockSpec(a.shape, lambda i: (0,) * a.ndim)
    consts = [dn_norm, ones_bd, w_out, ffn_norm.reshape(1, D), wr_pad]
    return pl.pallas_call(
        _odd_post_kernel,
        grid=(S // tm,),
        in_specs=[pl.BlockSpec((tm, D), lambda i: (i, 0)), rows, rows, rows, rows] + [full(a) for a in consts],
        out_specs=[pl.BlockSpec((tm, D), lambda i: (i, 0)), pl.BlockSpec((tm, 128), lambda i: (i, 0))],
        out_shape=[jax.ShapeDtypeStruct((S, D), F32), jax.ShapeDtypeStruct((S, 128), F32)],
        compiler_params=_params("parallel"),
        name="odd_post",
    )(x, y_c, o0, o1, zs, *consts)


def _row_copy(src_hbm, src_row, dst_vmem, dst_row, sem):
    return pltpu.make_async_copy(src_hbm.at[pl.ds(src_row, 1)], dst_vmem.at[pl.ds(dst_row, 1)], sem)


def _moe_group_kernel(te_ref, tv_ref, tok_ref, x_hbm, nw_ref, wg_ref, wu_ref, wd_ref, o_ref, xbuf, h_scr, sem, *, tm, n_f):
    m = pl.program_id(0)
    f = pl.program_id(1)
    valid = tv_ref[m] > 0
    slot = m % 2
    part = tm // n_f

    @pl.when((f == 0) & (m == 0))
    def _():
        def start(j, carry):
            _row_copy(x_hbm, tok_ref[j], xbuf.at[0], j, sem.at[0]).start()
            return carry

        lax.fori_loop(0, tm, start, 0, unroll=8)

    fed = valid | ((m > 0) & (tv_ref[jnp.maximum(m - 1, 0)] > 0))

    @pl.when(fed & (f == 0))
    def _():
        pltpu.make_async_copy(x_hbm.at[pl.ds(0, tm)], xbuf.at[slot], sem.at[slot]).wait()

    @pl.when(valid & (f == 0))
    def _():
        h_scr[...] = _rms(xbuf[slot], nw_ref[...]).astype(BF16)

    for into in range(2):
        for ff in range(n_f):
            @pl.when(valid & (slot == 1 - into) & (f == ff))
            def _(into=into, ff=ff):
                nbase = (m + 1) * tm + ff * part
                for j in range(part):
                    _row_copy(x_hbm, tok_ref[nbase + j], xbuf.at[into], ff * part + j, sem.at[into]).start()

    quarter = tm // MOE_ROW_SPLITS
    need = (tv_ref[m] + quarter - 1) // quarter
    for n_q in range(1, MOE_ROW_SPLITS + 1):
        rows = n_q * quarter

        @pl.when(need == n_q)
        def _(rows=rows):
            h = h_scr[0:rows, :]
            g = jnp.dot(h, wg_ref[...], preferred_element_type=F32)
            u = jnp.dot(h, wu_ref[...], preferred_element_type=F32)
            y = jnp.dot((_silu(g) * u).astype(BF16), wd_ref[...], preferred_element_type=F32)

            @pl.when(f == 0)
            def _():
                o_ref[0:rows, :] = y
                if rows < tm:
                    o_ref[rows:, :] = jnp.zeros((tm - rows, o_ref.shape[1]), o_ref.dtype)

            @pl.when(f != 0)
            def _():
                o_ref[0:rows, :] += y

    @pl.when(jnp.logical_not(valid) & (f == 0))
    def _():
        o_ref[...] = jnp.zeros_like(o_ref)


def moe_grouped_ffn(x, nw, tile_expert, tile_rows, src_tok, wg, wu, wd, *, tm, tf=1792):
    S, D = x.shape
    E, _, F = wg.shape
    n_tiles = tile_expert.shape[0]
    nf = F // tf

    def w_in(m, f, te, tv, tok):
        return (te[m], 0, jnp.where(tv[m] > 0, f, nf - 1))

    def w_out(m, f, te, tv, tok):
        return (te[m], jnp.where(tv[m] > 0, f, nf - 1), 0)

    grid_spec = pltpu.PrefetchScalarGridSpec(
        num_scalar_prefetch=3,
        grid=(n_tiles, nf),
        in_specs=[
            pl.BlockSpec(memory_space=pl.ANY),
            pl.BlockSpec((1, D), lambda m, f, te, tv, tok: (0, 0)),
            pl.BlockSpec((None, D, tf), w_in),
            pl.BlockSpec((None, D, tf), w_in),
            pl.BlockSpec((None, tf, D), w_out),
        ],
        out_specs=pl.BlockSpec((tm, D), lambda m, f, te, tv, tok: (m, 0)),
        scratch_shapes=[pltpu.VMEM((2, tm, D), F32), pltpu.VMEM((tm, D), BF16), pltpu.SemaphoreType.DMA((2,))],
    )
    return pl.pallas_call(
        functools.partial(_moe_group_kernel, tm=tm, n_f=nf),
        grid_spec=grid_spec,
        out_shape=jax.ShapeDtypeStruct((n_tiles * tm, D), F32),
        compiler_params=_params("arbitrary", "arbitrary"),
        name="moe_grouped_ffn",
    )(tile_expert, tile_rows, src_tok, x, nw.reshape(1, D), wg, wu, wd)


def _moe_combine_kernel(pos_ref, x_ref, rt_ref, ys_hbm, o_ref, buf, sem, *, tc, n_tok):
    i = pl.program_id(0)
    slot = i % 2

    def gather_tile(tile, into):
        base = tile * tc

        def start(j, carry):
            for k in range(TOP_K):
                _row_copy(ys_hbm, pos_ref[k * n_tok + base + j], buf.at[into, k], j, sem.at[into]).start()
            return carry

        lax.fori_loop(0, tc, start, 0, unroll=8)

    @pl.when(i == 0)
    def _():
        gather_tile(0, 0)

    for into in range(2):
        @pl.when((i + 1 < pl.num_programs(0)) & (slot == 1 - into))
        def _(into=into):
            base = (i + 1) * tc
            for j in range(tc):
                for k in range(TOP_K):
                    _row_copy(ys_hbm, pos_ref[k * n_tok + base + j], buf.at[into, k], j, sem.at[into]).start()

    for k in range(TOP_K):
        pltpu.make_async_copy(ys_hbm.at[pl.ds(0, tc)], buf.at[slot, k], sem.at[slot]).wait()
    rt = rt_ref[...]
    o_ref[...] = (x_ref[...] + rt[:, ROUTE_LANE_G1:ROUTE_LANE_G1 + 1] * buf[slot, 0]
                  + rt[:, ROUTE_LANE_G2:ROUTE_LANE_G2 + 1] * buf[slot, 1])


def moe_combine(x, route, ys, pos, *, tc=512):
    S, D = x.shape
    grid_spec = pltpu.PrefetchScalarGridSpec(
        num_scalar_prefetch=1,
        grid=(S // tc,),
        in_specs=[
            pl.BlockSpec((tc, D), lambda i, pos: (i, 0)),
            pl.BlockSpec((tc, 128), lambda i, pos: (i, 0)),
            pl.BlockSpec(memory_space=pl.ANY),
        ],
        out_specs=pl.BlockSpec((tc, D), lambda i, pos: (i, 0)),
        scratch_shapes=[pltpu.VMEM((2, TOP_K, tc, D), F32), pltpu.SemaphoreType.DMA((2,))],
    )
    return pl.pallas_call(
        functools.partial(_moe_combine_kernel, tc=tc, n_tok=S),
        grid_spec=grid_spec,
        out_shape=jax.ShapeDtypeStruct((S, D), F32),
        compiler_params=_params("arbitrary"),
        name="moe_combine",
    )(pos, x, route, ys)


def moe_top2(x, nw, route, wg, wu, wd, *, tm=512):
    S, D = x.shape
    E = wg.shape[0]
    experts = jnp.concatenate([route[:, ROUTE_LANE_I1], route[:, ROUTE_LANE_I2]]).astype(jnp.int32)
    onehot = (experts[:, None] == jnp.arange(E, dtype=jnp.int32)).astype(jnp.int32)
    csum = jnp.cumsum(onehot, axis=0)
    rank = jnp.sum(onehot * csum, axis=1) - 1
    padded = (csum[-1] + tm - 1) // tm * tm
    ends = jnp.cumsum(padded)
    pos = (jnp.sum(onehot * (ends - padded), axis=1) + rank).astype(jnp.int32)
    n_tiles = TOP_K * S // tm + E + 1
    tokens = jnp.tile(jnp.arange(S, dtype=jnp.int32), TOP_K)
    src_tok = jnp.zeros((n_tiles * tm,), jnp.int32).at[pos].set(tokens)
    tile_start = jnp.arange(n_tiles, dtype=jnp.int32) * tm
    tile_expert = jnp.minimum(jnp.sum(tile_start[:, None] >= ends[None, :], axis=1), E - 1).astype(jnp.int32)
    seg_end = (ends - padded + csum[-1])[tile_expert]
    tile_rows = jnp.clip(seg_end - tile_start, 0, tm).astype(jnp.int32)
    ys = moe_grouped_ffn(x, nw, tile_expert, tile_rows, src_tok, wg, wu, wd, tm=tm)
    return moe_combine(x, route, ys, pos)


ATTN_RADIUS = 64
ATTN_BQ = 128
ATTN_TILE = ATTN_BQ * max(d for _, d in DILATION_PATTERNS)
ATTN_HALO = ATTN_RADIUS * max(d for _, d in DILATION_PATTERNS)
ATTN_BLOCKS_IN_FLIGHT = 4
assert all(w // (2 * d) == ATTN_RADIUS for w, d in DILATION_PATTERNS)


def _attn_kernel(q_ref, kp_ref, kc_ref, kn_ref, vp_ref, vc_ref, vn_ref, o_ref, kbuf, vbuf, m_s, l_s, a_s, *, n_tok):
    tile, bq, halo, rad = ATTN_TILE, ATTN_BQ, ATTN_HALO, ATTN_RADIUS
    kbuf[0:halo, :] = kp_ref[...]
    kbuf[halo:halo + tile, :] = kc_ref[...]
    kbuf[halo + tile:, :] = kn_ref[...]
    vbuf[0:halo, :] = vp_ref[...]
    vbuf[halo:halo + tile, :] = vc_ref[...]
    vbuf[halo + tile:, :] = vn_ref[...]

    t0 = pl.program_id(1) * tile
    qi = lax.broadcasted_iota(jnp.int32, (bq, bq + 2 * rad), 0)
    kj = lax.broadcasted_iota(jnp.int32, (bq, bq + 2 * rad), 1)
    band = (kj >= qi) & (kj <= qi + 2 * rad)
    lo_half = lax.broadcasted_iota(jnp.int32, (bq, 2 * HEAD_DIM), 1) < HEAD_DIM

    for n_branch, (_, d) in enumerate(sorted(DILATION_PATTERNS, key=lambda wd: -wd[1])):
        span = bq * d
        reps = max(1, ATTN_BLOCKS_IN_FLIGHT // d)

        def block(blk, carry, d=d, span=span, reps=reps, first=n_branch == 0):
            stride = None if d == 1 else d
            problems = [(rep, r) for rep in range(reps) for r in range(d)]

            def scores(rep, r):
                base = pl.multiple_of(blk * (reps * span), span) + rep * span
                q_rows = pl.ds(base + r, bq, stride=stride)
                k_rows = pl.ds(halo + base + (r - rad * d), bq + 2 * rad, stride=stride)
                q = q_ref[q_rows, :]
                kw = kbuf[k_rows, :].astype(BF16)
                tok = t0 + base + (r - rad * d) + d * kj
                mask = band & (tok >= 0) & (tok < n_tok)
                sc = [jnp.where(mask, _bdot_nt(jnp.where(own, q, 0.0), kw), NEG_INF)
                      for own in (lo_half, jnp.logical_not(lo_half))]
                return q_rows, k_rows, sc

            pending = scores(*problems[0])
            for i in range(len(problems)):
                q_rows, k_rows, sc_pair = pending
                if i + 1 < len(problems):
                    pending = scores(*problems[i + 1])
                vw = vbuf[k_rows, :].astype(BF16)
                halves = []
                for sc in sc_pair:
                    m_h = jnp.max(sc, axis=-1, keepdims=True)
                    p = jnp.exp(sc - m_h)
                    halves.append((m_h, jnp.sum(p, axis=-1, keepdims=True),
                                   jnp.dot(p.astype(BF16), vw, preferred_element_type=F32)))
                m_b, l_b, a_b = (jnp.where(lo_half, x0, x1) for x0, x1 in zip(*halves))
                if first:
                    m_n, l_n, a_n = m_b, l_b, a_b
                else:
                    m_o = m_s[q_rows, :]
                    m_n = jnp.maximum(m_o, m_b)
                    w_o = jnp.exp(m_o - m_n)
                    w_b = jnp.exp(m_b - m_n)
                    l_n = l_s[q_rows, :] * w_o + l_b * w_b
                    a_n = a_s[q_rows, :] * w_o + a_b * w_b
                m_s[q_rows, :] = m_n
                l_s[q_rows, :] = l_n
                a_s[q_rows, :] = a_n
            return carry

        lax.fori_loop(0, tile // (reps * span), block, 0)

    o_ref[...] = a_s[...] / l_s[...]


def dilated_attention(q, k, v):
    S, W = q.shape
    pair = 2 * HEAD_DIM
    per = ATTN_TILE // ATTN_HALO
    cur = pl.BlockSpec((ATTN_TILE, pair), lambda p, i: (i, p))
    prv = pl.BlockSpec((ATTN_HALO, pair), lambda p, i: (jnp.maximum(i * per - 1, 0), p))
    nxt = pl.BlockSpec((ATTN_HALO, pair), lambda p, i: (jnp.minimum((i + 1) * per, S // ATTN_HALO - 1), p))
    ext = ATTN_TILE + 2 * ATTN_HALO
    return pl.pallas_call(
        functools.partial(_attn_kernel, n_tok=S),
        grid=(W // pair, S // ATTN_TILE),
        in_specs=[cur, prv, cur, nxt, prv, cur, nxt],
        out_specs=cur,
        out_shape=jax.ShapeDtypeStruct((S, W), F32),
        scratch_shapes=[pltpu.VMEM((ext, pair), F32), pltpu.VMEM((ext, pair), F32)]
        + [pltpu.VMEM((ATTN_TILE, pair), F32)] * 3,
        compiler_params=_params("parallel", "arbitrary"),
        name="dilated_attention",
    )(q, k, k, k, v, v, v)


def _tri_masks(z):
    row = lax.broadcasted_iota(jnp.int32, (CHUNK, CHUNK), 0)
    col = lax.broadcasted_iota(jnp.int32, (CHUNK, CHUNK), 1)
    if z == 0:
        return col <= row, col < row
    return col >= row, col > row


RWKV_SUB_CHUNKS = 2
DN_SUB_CHUNKS = 4


def _sub_rows(z, j, sub):
    i = j if z == 0 else sub - 1 - j
    return slice(i * CHUNK, (i + 1) * CHUNK)


def _neumann_solve(ns, xs):
    steps = CHUNK.bit_length() - 1
    for i in range(steps):
        xs = [x + _bdot(n, x) for n, x in zip(ns, xs)]
        if i + 1 < steps:
            ns = [_bdot(n, n) for n in ns]
    return xs


def _rwkv_chunk_kernel(*refs, n_cast):
    sub = RWKV_SUB_CHUNKS
    ins = (refs[0:6], refs[6:12])
    cast_in = refs[12:12 + n_cast]
    y_refs = refs[12 + n_cast:14 + n_cast]
    cast_out = refs[14 + n_cast:14 + 2 * n_cast]
    state = refs[14 + 2 * n_cast]
    for src, dst in zip(cast_in, cast_out):
        dst[...] = src[...].astype(BF16)

    @pl.when(pl.program_id(0) == 0)
    def _():
        state[...] = jnp.zeros_like(state)

    chains = [(j, z, h) for j in range(sub) for z in range(2) for h in range(N_HEADS_B)]
    masks = [_tri_masks(z) for z in range(2)]
    prep = {}
    for j in range(sub):
        for z in range(2):
            rows = _sub_rows(z, j, sub)
            r_ref, v_ref, a_ref, lw_ref, k_ref, b_ref = ins[z]
            tri = jnp.where(masks[z][0], 1.0, 0.0).astype(BF16)
            lw = lw_ref[rows, :]
            hi, mid, lo = _split3(lw)
            cum = (jnp.dot(tri, hi, preferred_element_type=F32) + jnp.dot(tri, mid, preferred_element_type=F32)
                   + jnp.dot(tri, lo, preferred_element_type=F32))
            tot = jnp.sum(lw, axis=0, keepdims=True)
            e_neg = jnp.exp(-cum)
            e_end = jnp.exp(tot - cum)
            k = k_ref[rows, :]
            b = b_ref[rows, :]
            prep[j, z] = dict(rt=r_ref[rows, :] * jnp.exp(cum), at=a_ref[rows, :] * jnp.exp(cum - lw), kt=k * e_neg,
                              bt=b * e_neg, kh=k * e_end, bh=b * e_end, dw=jnp.exp(tot), v=v_ref[rows, :])

    def part(name, c):
        j, z, h = c
        return prep[j, z][name][:, h * HEAD_DIM:(h + 1) * HEAD_DIM]

    xs = [_bdot_nt(jnp.concatenate([part("at", c), part("rt", c)], axis=0),
                   jnp.concatenate([part("bt", c), part("kt", c)], axis=0)) for c in chains]
    a_ab = [jnp.where(masks[c[1]][1], x[:CHUNK, :CHUNK], 0.0) for c, x in zip(chains, xs)]
    a_rb = [jnp.where(masks[c[1]][0], x[CHUNK:, :CHUNK], 0.0) for c, x in zip(chains, xs)]
    cys = [_bdot(jnp.concatenate([jnp.where(masks[c[1]][1], x[:CHUNK, CHUNK:], 0.0),
                                  jnp.where(masks[c[1]][0], x[CHUNK:, CHUNK:], 0.0)], axis=0), part("v", c))
           for c, x in zip(chains, xs)]
    pqs = _neumann_solve(a_ab, [jnp.concatenate([part("at", c), cy[:CHUNK]], axis=1) for c, cy in zip(chains, cys)])
    kvs = [_bdot_tn(part("v", c), part("kh", c)) for c in chains]
    per = 2 * N_HEADS_B
    s_cur = [state[z, h] for _, z, h in chains[:per]]
    for j in range(sub):
        sel = slice(j * per, (j + 1) * per)
        cs = chains[sel]
        us = [_bdot_nt(pq[:, :HEAD_DIM], s) + pq[:, HEAD_DIM:] for pq, s in zip(pqs[sel], s_cur)]
        ys = [_bdot_nt(part("rt", c), s) for c, s in zip(cs, s_cur)]
        ys = [y + _bdot(arb, u) + cy[CHUNK:] for y, arb, u, cy in zip(ys, a_rb[sel], us, cys[sel])]
        s_cur = [s * part("dw", c) + _bdot_tn(u, part("bh", c)) + kv for c, s, u, kv in zip(cs, s_cur, us, kvs[sel])]
        for (_, z, h), y in zip(cs, ys):
            y_refs[z][_sub_rows(z, j, sub), h * HEAD_DIM:(h + 1) * HEAD_DIM] = y
    for (_, z, h), s in zip(chains[:per], s_cur):
        state[z, h] = s


def rwkv7_scan(r, v, a, lw, k, b, cast_through=()):
    S, C = r.shape
    step_rows = RWKV_SUB_CHUNKS * CHUNK
    n = S // step_rows
    fwd = pl.BlockSpec((step_rows, C), lambda c: (c, 0))
    bwd = pl.BlockSpec((step_rows, C), lambda c: (n - 1 - c, 0))
    cast_specs = [pl.BlockSpec((w.shape[0] // n, w.shape[1]), lambda c: (c, 0)) for w in cast_through]
    y0, y1, *cast = pl.pallas_call(
        functools.partial(_rwkv_chunk_kernel, n_cast=len(cast_through)),
        grid=(n,),
        in_specs=[fwd] * 6 + [bwd] * 6 + cast_specs,
        out_specs=[fwd, bwd] + cast_specs,
        out_shape=[jax.ShapeDtypeStruct((S, C), F32)] * 2 + [jax.ShapeDtypeStruct(w.shape, BF16) for w in cast_through],
        scratch_shapes=[pltpu.VMEM((2, N_HEADS_B, HEAD_DIM, HEAD_DIM), F32)],
        compiler_params=_params("arbitrary"),
        name="rwkv7_scan",
    )(r, v, a, lw[0], k[0], b[0], r, v, a, lw[1], k[1], b[1], *cast_through)
    return y0, y1, cast


def _dn_chunk_kernel(*refs):
    sub = DN_SUB_CHUNKS
    ins = (refs[0:6], refs[6:12])
    o_refs = refs[12:14]
    state = refs[14]

    @pl.when(pl.program_id(0) == 0)
    def _():
        state[...] = jnp.zeros_like(state)

    chains = [(j, z, h) for j in range(sub) for z in range(2) for h in range(N_HEADS_D)]
    masks = [_tri_masks(z) for z in range(2)]
    nt = (((1,), (1,)), ((), ()))
    gcs, decays, betas, g_lasts = [], [], [], []
    for j in range(sub):
        for z in range(2):
            rows = _sub_rows(z, j, sub)
            _, _, _, bcol_ref, gcol_ref, grow_ref = ins[z]
            incl = masks[z][0]
            tri = jnp.where(incl, 1.0, 0.0).astype(BF16)
            c_hi, c_mid, c_lo = _split3(gcol_ref[rows, :])
            gc_cols = (jnp.dot(tri, c_hi, preferred_element_type=F32) + jnp.dot(tri, c_mid, preferred_element_type=F32)
                       + jnp.dot(tri, c_lo, preferred_element_type=F32))
            r_hi, r_mid, r_lo = _split3(grow_ref[rows.start // CHUNK])
            gc_rows = (lax.dot_general(r_hi, tri, nt, preferred_element_type=F32)
                       + lax.dot_general(r_mid, tri, nt, preferred_element_type=F32)
                       + lax.dot_general(r_lo, tri, nt, preferred_element_type=F32))
            last = CHUNK - 1 if z == 0 else 0
            bcol = bcol_ref[rows, :]
            for h in range(N_HEADS_D):
                idx = z * N_HEADS_D + h
                gc = gc_cols[:, idx:idx + 1]
                diff = gc - gc_rows[idx:idx + 1, :]
                gcs.append(gc)
                decays.append(jnp.where(incl, jnp.exp(jnp.where(incl, diff, 0.0)), 0.0))
                betas.append(bcol[:, idx:idx + 1])
                g_lasts.append(gc[last:last + 1, :])

    def part(i, c):
        j, z, h = c
        return ins[z][i][_sub_rows(z, j, sub), h * HEAD_DIM_D:(h + 1) * HEAD_DIM_D]

    qs = [part(0, c) for c in chains]
    ks = [part(1, c) for c in chains]
    vs = [part(2, c) for c in chains]
    kbs = [k * beta for k, beta in zip(ks, betas)]
    e_gcs = [jnp.exp(gc) for gc in gcs]
    kqs = [_bdot_nt(jnp.concatenate([kb, q], axis=0), k) for kb, q, k in zip(kbs, qs, ks)]
    n_mats = [jnp.where(masks[c[1]][1], -(kq[:CHUNK] * dc), 0.0) for c, kq, dc in zip(chains, kqs, decays)]
    attns = [kq[CHUNK:] * dc for kq, dc in zip(kqs, decays)]
    uks = _neumann_solve(n_mats, [jnp.concatenate([v * beta, kb * e], axis=1)
                                  for v, beta, kb, e in zip(vs, betas, kbs, e_gcs)])
    per = 2 * N_HEADS_D
    s_cur = [state[z, h] for _, z, h in chains[:per]]
    for j in range(sub):
        sel = slice(j * per, (j + 1) * per)
        us = [uk[:, :HEAD_DIM_D] - _bdot(uk[:, HEAD_DIM_D:], s) for uk, s in zip(uks[sel], s_cur)]
        os_ = [_bdot(q * e, s) for q, e, s in zip(qs[sel], e_gcs[sel], s_cur)]
        os_ = [o + _bdot(attn, u) for o, attn, u in zip(os_, attns[sel], us)]
        s_cur = [s * jnp.exp(gl) + _bdot_tn(k * jnp.exp(gl - gc), u)
                 for s, gl, k, gc, u in zip(s_cur, g_lasts[sel], ks[sel], gcs[sel], us)]
        for (_, z, h), o in zip(chains[sel], os_):
            o_refs[z][_sub_rows(z, j, sub), h * HEAD_DIM_D:(h + 1) * HEAD_DIM_D] = o
    for (_, z, h), s in zip(chains[:per], s_cur):
        state[z, h] = s


def deltanet_scan(q, k, v, beta, g):
    S, C = q.shape
    step_rows = DN_SUB_CHUNKS * CHUNK
    n = S // step_rows
    nz = 2 * N_HEADS_D
    g_rows = g.reshape(S // CHUNK, CHUNK, nz).transpose(0, 2, 1)

    def specs(idx):
        wide = pl.BlockSpec((step_rows, C), lambda c: (idx(c), 0))
        col = pl.BlockSpec((step_rows, nz), lambda c: (idx(c), 0))
        row = pl.BlockSpec((DN_SUB_CHUNKS, nz, CHUNK), lambda c: (idx(c), 0, 0))
        return [wide, wide, wide, col, col, row], wide

    in_f, out_f = specs(lambda c: c)
    in_b, out_b = specs(lambda c: n - 1 - c)
    o0, o1 = pl.pallas_call(
        _dn_chunk_kernel,
        grid=(n,),
        in_specs=in_f + in_b,
        out_specs=[out_f, out_b],
        out_shape=[jax.ShapeDtypeStruct((S, C), F32)] * 2,
        scratch_shapes=[pltpu.VMEM((2, N_HEADS_D, HEAD_DIM_D, HEAD_DIM_D), F32)],
        compiler_params=_params("arbitrary"),
        name="deltanet_scan",
    )(q, k, v, beta, g, g_rows, q, k, v, beta, g, g_rows)
    return o0, o1


def _rope_tables(positions):
    half = ROPE_DIM // 2
    dim = jnp.arange(2 * HEAD_DIM) % HEAD_DIM
    freq = jnp.where(dim < ROPE_DIM, jnp.power(ROPE_THETA, -(dim % half).astype(F32) / half), 0.0)
    sign = jnp.where(dim < half, -1.0, 1.0)
    ang = positions[:, None].astype(F32) * freq
    return jnp.cos(ang), sign * jnp.sin(ang)


def _ones_block_diag(width, group):
    idx = jnp.arange(width) // group
    return (idx[:, None] == idx[None, :]).astype(BF16)


def _block_diag(blocks):
    rows = sum(b.shape[0] for b in blocks)
    cols = sum(b.shape[1] for b in blocks)
    out = jnp.zeros((rows, cols), blocks[0].dtype)
    r = c = 0
    for b in blocks:
        out = lax.dynamic_update_slice(out, b, (r, c))
        r += b.shape[0]
        c += b.shape[1]
    return out


def _even_layer(x, positions, mix_norm, w_in, q_norm, k_norm, shift_mu, lora_mu, w0, w1, w2, a0, a1, a2, g1, g2,
                k_k, k_a, r_k, ln_w, ln_b, w_out, ffn_norm, ffn_gate, ffn_up, ffn_down, cast_through=()):
    row = lambda t: t.reshape(1, -1)
    cos_t, sin_t = _rope_tables(positions)
    ones_bd = _ones_block_diag(WIDTH_B, HEAD_DIM)
    lora_in = jnp.concatenate([w1[0], w1[1], a1[0], a1[1], g1], axis=1).astype(BF16)
    lora_out = _block_diag([w2[0], w2[1], a2[0], a2[1], g2]).astype(BF16)
    (q, k, v, r, vb, a_vec, lw0, lw1, k0, k1, b0, b1, gate, bonus) = even_prep(
        x, mix_norm, w_in.astype(BF16), row(jnp.tile(q_norm, N_HEADS_A)), row(jnp.tile(k_norm, N_HEADS_A)), cos_t, sin_t,
        shift_mu, lora_mu, lora_in, lora_out, w0, a0, row(k_k), row(k_a), row(r_k), ones_bd)
    y_a = dilated_attention(q, k, v)
    y0, y1, cast = rwkv7_scan(r, vb, a_vec, (lw0, lw1), (k0, k1), (b0, b1),
                              [w.reshape(-1, w.shape[-1]) for w in cast_through])
    cast = [c.reshape(w.shape) for c, w in zip(cast, cast_through)]
    x = even_post_ffn(x, y_a, y0, y1, gate, bonus, row(ln_w), row(ln_b), ones_bd, w_out.astype(BF16), ffn_norm,
                      ffn_gate.astype(BF16), ffn_up.astype(BF16), ffn_down.astype(BF16))
    return x, cast


def _odd_layer(x, mix_norm, w_in, conv_c, conv_dn, A_log, dt_bias, dn_norm, w_out, ffn_norm, router, moe_gate, moe_up, moe_down):
    n_in = w_in.shape[1]
    n_pad = -(-n_in // 128) * 128
    w_in_p = jnp.pad(w_in, ((0, 0), (0, n_pad - n_in))).astype(BF16)
    nz = 2 * N_HEADS_D
    neg_a = jnp.zeros((1, 128), F32).at[0, nz:2 * nz].set(-jnp.exp(A_log.reshape(-1)))
    dt_b = jnp.zeros((1, 128), F32).at[0, nz:2 * nz].set(dt_bias.reshape(-1))
    ones_bd = _ones_block_diag(WIDTH_D, HEAD_DIM_D)
    y_c, q, k, v, zs, bg = odd_prep(x, mix_norm, w_in_p, conv_c, conv_dn, neg_a, dt_b, ones_bd)
    o0, o1 = deltanet_scan(q, k, v, bg[:, :nz], bg[:, nz:2 * nz])
    wr_pad = jnp.pad(router, ((0, 0), (0, 128 - N_EXPERTS)))
    x, route = odd_post(x, y_c, o0, o1, zs, jnp.tile(dn_norm, N_HEADS_D).reshape(1, -1), ones_bd, w_out.astype(BF16),
                        ffn_norm, wr_pad)
    return moe_top2(x, ffn_norm, route, moe_gate.astype(BF16), moe_up.astype(BF16), moe_down.astype(BF16))


def kernel(x, positions, ev_mix_norm, ev_w_in, ev_q_norm, ev_k_norm, ev_shift_mu, ev_lora_mu, ev_w0, ev_w1, ev_w2, ev_a0, ev_a1, ev_a2, ev_g1, ev_g2, ev_k_k, ev_k_a, ev_r_k, ev_ln_w, ev_ln_b, ev_w_out, ev_ffn_norm, ev_ffn_gate, ev_ffn_up, ev_ffn_down, od_mix_norm, od_w_in, od_conv_c, od_conv_dn, od_A_log, od_dt_bias, od_dn_norm, od_w_out, od_ffn_norm, od_router, od_moe_gate, od_moe_up, od_moe_down):
    B, S, D = x.shape
    assert B == 1
    xs = x.reshape(S, D)
    pos = positions.reshape(S)
    n_layers = ev_mix_norm.shape[0] + od_mix_norm.shape[0]
    for layer in range(n_layers):
        i = layer // 2
        if layer % 2 == 0:
            nxt = (od_moe_gate[i], od_moe_up[i], od_moe_down[i]) if layer + 1 < n_layers else ()
            xs, moe_w = _even_layer(xs, pos, ev_mix_norm[i], ev_w_in[i], ev_q_norm[i], ev_k_norm[i], ev_shift_mu[i],
                                    ev_lora_mu[i], ev_w0[i], ev_w1[i], ev_w2[i], ev_a0[i], ev_a1[i], ev_a2[i], ev_g1[i],
                                    ev_g2[i], ev_k_k[i], ev_k_a[i], ev_r_k[i], ev_ln_w[i], ev_ln_b[i], ev_w_out[i],
                                    ev_ffn_norm[i], ev_ffn_gate[i], ev_ffn_up[i], ev_ffn_down[i], cast_through=nxt)
        else:
            xs = _odd_layer(xs, od_mix_norm[i], od_w_in[i], od_conv_c[i], od_conv_dn[i], od_A_log[i], od_dt_bias[i],
                            od_dn_norm[i], od_w_out[i], od_ffn_norm[i], od_router[i], *moe_w)
    return xs.reshape(B, S, D)
```

```python
import functools

import jax
import jax.numpy as jnp
from jax import lax
from jax.experimental import pallas as pl
from jax.experimental.pallas import tpu as pltpu

F32 = jnp.float32
BF16 = jnp.bfloat16

HEAD_DIM = 64
N_HEADS_A = 8
WIDTH_A = N_HEADS_A * HEAD_DIM
DILATION_PATTERNS = ((128, 1), (512, 4), (2048, 16))
ROPE_DIM = HEAD_DIM // 4
ROPE_THETA = 500000.0
N_HEADS_B = 8
WIDTH_B = N_HEADS_B * HEAD_DIM
DECAY_LORA, ICLR_LORA, GATE_LORA = 64, 64, 128
RWKV_LN_EPS = 64e-5
WIDTH_C = 512
N_HEADS_D = 4
HEAD_DIM_D = 128
WIDTH_D = N_HEADS_D * HEAD_DIM_D
CHUNK = 64
N_EXPERTS = 8
TOP_K = 2
ROUTE_LANE_I1, ROUTE_LANE_I2, ROUTE_LANE_G1, ROUTE_LANE_G2 = 8, 9, 10, 11
MOE_ROW_SPLITS = 4
NORM_EPS = 1e-6
NEG_INF = -1e30

V7X_VMEM_LIMIT_BYTES = 56 * 1024 * 1024
LANES = 128


def _params(*sem):
    return pltpu.CompilerParams(dimension_semantics=sem, vmem_limit_bytes=V7X_VMEM_LIMIT_BYTES)


def _bdot(a, b):
    return jnp.dot(a.astype(BF16), b.astype(BF16), preferred_element_type=F32)


def _bdot_nt(a, b):
    return lax.dot_general(a.astype(BF16), b.astype(BF16), (((1,), (1,)), ((), ())), preferred_element_type=F32)


def _bdot_tn(a, b):
    return lax.dot_general(a.astype(BF16), b.astype(BF16), (((0,), (0,)), ((), ())), preferred_element_type=F32)


def _split3(x):
    hi = x.astype(BF16)
    r1 = x - hi.astype(F32)
    mid = r1.astype(BF16)
    lo = (r1 - mid.astype(F32)).astype(BF16)
    return hi, mid, lo


def _rms(x, w):
    return x * lax.rsqrt(jnp.mean(x * x, axis=-1, keepdims=True) + NORM_EPS) * w


HALO = 8


def _group_sum(x, ones_bd):
    hi = x.astype(BF16)
    lo = (x - hi.astype(F32)).astype(BF16)
    return jnp.dot(hi, ones_bd, preferred_element_type=F32) + jnp.dot(lo, ones_bd, preferred_element_type=F32)


def _silu(x):
    return x * jax.nn.sigmoid(x)


def _halo_specs(tm, n_rows, width):
    per = tm // HALO
    cur = pl.BlockSpec((tm, width), lambda i: (i, 0))
    prv = pl.BlockSpec((HALO, width), lambda i: (jnp.maximum(i * per - 1, 0), 0))
    nxt = pl.BlockSpec((HALO, width), lambda i: (jnp.minimum((i + 1) * per, n_rows // HALO - 1), 0))
    return cur, prv, nxt


def _make_shifts(tm, n_rows):
    row = pl.program_id(0) * tm + lax.broadcasted_iota(jnp.int32, (tm, 1), 0)
    first, last = row == 0, row == n_rows - 1
    n_ext = tm + 2 * HALO

    def shifts(t):
        prev = jnp.where(first, 0.0, pltpu.roll(t, 1, 0)[HALO:HALO + tm])
        nxt = jnp.where(last, 0.0, pltpu.roll(t, n_ext - 1, 0)[HALO:HALO + tm])
        return prev, t[HALO:HALO + tm], nxt

    return shifts


def _even_prep_kernel(xc_ref, xp_ref, xn_ref, nw_ref, win_ref, qn_ref, kn_ref, cos_ref, sin_ref, smu_ref, lmu_ref,
                      lin_ref, lout_ref, w0_ref, a0_ref, kk_ref, ka_ref, rk_ref, ones_ref,
                      q_out, k_out, v_out, r_out, vb_out, a_out, lw0_out, lw1_out, k0_out, k1_out, b0_out, b1_out,
                      gate_out, bonus_out, *, tm, n_rows):
    shifts = _make_shifts(tm, n_rows)
    he = _rms(jnp.concatenate([xp_ref[...], xc_ref[...], xn_ref[...]], axis=0), nw_ref[...])
    proj = jnp.dot(he.astype(BF16), win_ref[...], preferred_element_type=F32)
    ones = ones_ref[...]
    lane = lax.broadcasted_iota(jnp.int32, (tm, WIDTH_A), 1) % HEAD_DIM

    n_pairs = WIDTH_A // cos_ref.shape[1]
    cos_t = jnp.concatenate([cos_ref[...]] * n_pairs, axis=1)
    sin_t = jnp.concatenate([sin_ref[...]] * n_pairs, axis=1)

    def head_rms_rope(t, w):
        t = t * lax.rsqrt(_group_sum(t * t, ones) * (1.0 / HEAD_DIM) + NORM_EPS) * w
        half = ROPE_DIM // 2
        swapped = jnp.where(lane < half, pltpu.roll(t, WIDTH_A - half, 1), pltpu.roll(t, half, 1))
        return t * cos_t + swapped * sin_t

    cur = proj[HALO:HALO + tm]
    q = head_rms_rope(cur[:, :WIDTH_A], qn_ref[...]) * HEAD_DIM ** -0.5
    k = head_rms_rope(cur[:, WIDTH_A:2 * WIDTH_A], kn_ref[...])
    v = cur[:, 2 * WIDTH_A:3 * WIDTH_A]
    q_out[...] = q
    k_out[...] = k
    v_out[...] = v

    p_prev, p_cur, p_next = shifts(proj[:, 3 * WIDTH_A:])
    smu = smu_ref[...]
    rkv = p_cur + smu[0:1] * (p_prev - p_cur) + smu[1:2] * (p_next - p_cur)
    r, kin, vb = rkv[:, :WIDTH_B], rkv[:, WIDTH_B:2 * WIDTH_B], rkv[:, 2 * WIDTH_B:]
    h_prev, h_cur, h_next = shifts(he)
    lmu = lmu_ref[...]
    hx = h_cur + lmu[0:1] * (h_prev - h_cur) + lmu[1:2] * (h_next - h_cur)
    l1 = jnp.dot(hx.astype(BF16), lin_ref[...], preferred_element_type=F32)
    n_w, n_a = 2 * DECAY_LORA, 2 * ICLR_LORA
    l1 = jnp.concatenate([jnp.tanh(l1[:, :n_w]), l1[:, n_w:n_w + n_a], jax.nn.sigmoid(l1[:, n_w + n_a:])], axis=1)
    l2 = jnp.dot(l1.astype(BF16), lout_ref[...], preferred_element_type=F32)
    w0, a0 = w0_ref[...], a0_ref[...]
    kk = kin * kk_ref[...]
    kk = kk * lax.rsqrt(_group_sum(kk * kk, ones) + 1e-6)
    kdirs = []
    for z, (lw_out, k_out_z, b_out_z) in enumerate(((lw0_out, k0_out, b0_out), (lw1_out, k1_out, b1_out))):
        w_pre = l2[:, z * WIDTH_B:(z + 1) * WIDTH_B] + w0[z:z + 1]
        lw_out[...] = -jnp.exp(-0.5) * jax.nn.sigmoid(w_pre)
        iclr = jax.nn.sigmoid(l2[:, (2 + z) * WIDTH_B:(3 + z) * WIDTH_B] + a0[z:z + 1])
        kdir = kin * (1.0 + (iclr - 1.0) * ka_ref[...])
        k_out_z[...] = kdir
        b_out_z[...] = kk * iclr
        kdirs.append(kdir)
    r_out[...] = r
    vb_out[...] = vb
    a_out[...] = -kk
    gate_out[...] = l2[:, 4 * WIDTH_B:]
    bonus_out[...] = _group_sum(r * (kdirs[0] + kdirs[1]) * rk_ref[...], ones) * vb


def even_prep(x, mix_norm, w_in, q_norm, k_norm, cos_t, sin_t, shift_mu, lora_mu, lora_in, lora_out, w0, a0, k_k, k_a,
              r_k, ones_bd, *, tm=256):
    S, D = x.shape
    cur, prv, nxt = _halo_specs(tm, S, D)
    full = lambda a: pl.BlockSpec(a.shape, lambda i: (0,) * a.ndim)
    rows = pl.BlockSpec((tm, WIDTH_B), lambda i: (i, 0))
    consts = [mix_norm.reshape(1, D), w_in, q_norm, k_norm]
    consts2 = [shift_mu, lora_mu, lora_in, lora_out, w0, a0, k_k, k_a, r_k, ones_bd]
    return pl.pallas_call(
        functools.partial(_even_prep_kernel, tm=tm, n_rows=S),
        grid=(S // tm,),
        in_specs=[cur, prv, nxt] + [full(a) for a in consts]
        + [pl.BlockSpec((tm, cos_t.shape[1]), lambda i: (i, 0))] * 2 + [full(a) for a in consts2],
        out_specs=[rows] * 14,
        out_shape=[jax.ShapeDtypeStruct((S, WIDTH_B), F32)] * 14,
        compiler_params=_params("parallel"),
        name="even_prep",
    )(x, x, x, *consts, cos_t, sin_t, *consts2)


def _odd_prep_kernel(xc_ref, xp_ref, xn_ref, nw_ref, win_ref, cc_ref, cdn_ref, nega_ref, dtb_ref, ones_ref,
                     yc_out, q_out, k_out, v_out, zs_out, bg_out, *, tm, n_rows):
    shifts = _make_shifts(tm, n_rows)
    he = _rms(jnp.concatenate([xp_ref[...], xc_ref[...], xn_ref[...]], axis=0), nw_ref[...])
    proj = jnp.dot(he.astype(BF16), win_ref[...], preferred_element_type=F32)
    cur = proj[HALO:HALO + tm]

    def conv3(t, w):
        prev, mid, nxt = shifts(t)
        return w[0:1] * prev + w[1:2] * mid + w[2:3] * nxt

    o_dn = 3 * WIDTH_C
    yc_out[...] = cur[:, :WIDTH_C] * conv3(proj[:, WIDTH_C:2 * WIDTH_C] * proj[:, 2 * WIDTH_C:o_dn], cc_ref[...])
    qkv = _silu(conv3(proj[:, o_dn:o_dn + 3 * WIDTH_D], cdn_ref[...]))
    ones = ones_ref[...]
    l2n = lambda t: t * lax.rsqrt(_group_sum(t * t, ones) + 1e-6)
    q_out[...] = l2n(qkv[:, :WIDTH_D]) * HEAD_DIM_D ** -0.5
    k_out[...] = l2n(qkv[:, WIDTH_D:2 * WIDTH_D])
    v_out[...] = qkv[:, 2 * WIDTH_D:]
    zs_out[...] = _silu(cur[:, o_dn + 3 * WIDTH_D:o_dn + 4 * WIDTH_D])
    tail = cur[:, o_dn + 4 * WIDTH_D:]
    t = tail + dtb_ref[...]
    softplus = jnp.maximum(t, 0.0) + jnp.log(1.0 + jnp.exp(-jnp.abs(t)))
    lane = lax.broadcasted_iota(jnp.int32, tail.shape, 1)
    bg_out[...] = jnp.where(lane < 2 * N_HEADS_D, jax.nn.sigmoid(tail), nega_ref[...] * softplus)


def odd_prep(x, mix_norm, w_in_pad, conv_c, conv_dn, neg_a, dt_b, ones_bd, *, tm=512):
    S, D = x.shape
    cur, prv, nxt = _halo_specs(tm, S, D)
    full = lambda a: pl.BlockSpec(a.shape, lambda i: (0,) * a.ndim)
    rows = pl.BlockSpec((tm, WIDTH_D), lambda i: (i, 0))
    consts = [mix_norm.reshape(1, D), w_in_pad, conv_c, conv_dn, neg_a, dt_b, ones_bd]
    return pl.pallas_call(
        functools.partial(_odd_prep_kernel, tm=tm, n_rows=S),
        grid=(S // tm,),
        in_specs=[cur, prv, nxt] + [full(a) for a in consts],
        out_specs=[rows] * 5 + [pl.BlockSpec((tm, LANES), lambda i: (i, 0))],
        out_shape=[jax.ShapeDtypeStruct((S, WIDTH_D), F32)] * 5 + [jax.ShapeDtypeStruct((S, LANES), F32)],
        compiler_params=_params("parallel"),
        name="odd_prep",
    )(x, x, x, *consts)


def _even_post_ffn_kernel(x_ref, ya_ref, y0_ref, y1_ref, gate_ref, bonus_ref, lnw_ref, lnb_ref, ones_ref, wo_ref,
                          fnw_ref, wg_ref, wu_ref, wd_ref, o_ref, h_scr):
    @pl.when(pl.program_id(1) == 0)
    def _():
        ones = ones_ref[...]
        yf = y0_ref[...] + y1_ref[...]
        dev = yf - _group_sum(yf, ones) * (1.0 / HEAD_DIM)
        var = _group_sum(dev * dev, ones) * (1.0 / HEAD_DIM)
        yn = dev * lax.rsqrt(var + RWKV_LN_EPS) * lnw_ref[...] + lnb_ref[...]
        y_b = (yn + bonus_ref[...]) * gate_ref[...]
        y = jnp.concatenate([ya_ref[...], y_b], axis=1)
        x = x_ref[...] + jnp.dot(y.astype(BF16), wo_ref[...], preferred_element_type=F32)
        h_scr[...] = _rms(x, fnw_ref[...]).astype(BF16)
        o_ref[...] = x

    h = h_scr[...]
    g = jnp.dot(h, wg_ref[...], preferred_element_type=F32)
    u = jnp.dot(h, wu_ref[...], preferred_element_type=F32)
    o_ref[...] += jnp.dot((_silu(g) * u).astype(BF16), wd_ref[...], preferred_element_type=F32)


def even_post_ffn(x, y_a, y0, y1, gate, bonus, ln_w, ln_b, ones_bd, w_out, ffn_norm, wg, wu, wd, *, tm=512, tf=1408):
    S, D = x.shape
    F = wg.shape[1]
    rows = pl.BlockSpec((tm, WIDTH_B), lambda i, f: (i, 0))
    full = lambda a: pl.BlockSpec(a.shape, lambda i, f: (0,) * a.ndim)
    consts = [ln_w, ln_b, ones_bd, w_out, ffn_norm.reshape(1, D)]
    return pl.pallas_call(
        _even_post_ffn_kernel,
        grid=(S // tm, F // tf),
        in_specs=[pl.BlockSpec((tm, D), lambda i, f: (i, 0)),
                  rows, rows, rows, rows, rows] + [full(a) for a in consts] + [
            pl.BlockSpec((D, tf), lambda i, f: (0, f)),
            pl.BlockSpec((D, tf), lambda i, f: (0, f)),
            pl.BlockSpec((tf, D), lambda i, f: (f, 0)),
        ],
        out_specs=pl.BlockSpec((tm, D), lambda i, f: (i, 0)),
        out_shape=jax.ShapeDtypeStruct((S, D), F32),
        scratch_shapes=[pltpu.VMEM((tm, D), BF16)],
        compiler_params=_params("parallel", "arbitrary"),
        name="even_post_ffn",
    )(x, y_a, y0, y1, gate, bonus, *consts, wg, wu, wd)


def _route_record(h, wr):
    h_hi, w_hi = h.astype(BF16), wr.astype(BF16)
    h_lo, w_lo = (h - h_hi.astype(F32)).astype(BF16), (wr - w_hi.astype(F32)).astype(BF16)
    logits = (jnp.dot(h_hi, w_hi, preferred_element_type=F32) + jnp.dot(h_lo, w_hi, preferred_element_type=F32)
              + jnp.dot(h_hi, w_lo, preferred_element_type=F32))
    lane = lax.broadcasted_iota(jnp.int32, logits.shape, 1)
    valid = lane < N_EXPERTS
    lg = jnp.where(valid, logits, NEG_INF)
    e = jnp.exp(lg - jnp.max(lg, axis=-1, keepdims=True))
    p = e / jnp.sum(e, axis=-1, keepdims=True)
    pm = jnp.where(valid, p, -1.0)
    m1 = jnp.max(pm, axis=-1, keepdims=True)
    i1 = jnp.min(jnp.where(pm == m1, lane, LANES), axis=-1, keepdims=True)
    pm2 = jnp.where(lane == i1, -1.0, pm)
    m2 = jnp.max(pm2, axis=-1, keepdims=True)
    i2 = jnp.min(jnp.where(pm2 == m2, lane, LANES), axis=-1, keepdims=True)
    tot = m1 + m2
    g1, g2 = m1 / tot, m2 / tot
    out = jnp.where(lane == i1, g1, 0.0) + jnp.where(lane == i2, g2, 0.0)
    out = jnp.where(lane == ROUTE_LANE_I1, i1.astype(F32), out)
    out = jnp.where(lane == ROUTE_LANE_I2, i2.astype(F32), out)
    out = jnp.where(lane == ROUTE_LANE_G1, g1, out)
    return jnp.where(lane == ROUTE_LANE_G2, g2, out)


def _odd_post_kernel(x_ref, yc_ref, o0_ref, o1_ref, zs_ref, dnw_ref, ones_ref, wo_ref, fnw_ref, wr_ref, x_out, route_out):
    o = o0_ref[...] + o1_ref[...]
    ms = _group_sum(o * o, ones_ref[...]) * (1.0 / HEAD_DIM_D)
    y_d = o * lax.rsqrt(ms + NORM_EPS) * dnw_ref[...] * zs_ref[...]
    y = jnp.concatenate([yc_ref[...], y_d], axis=1)
    x = x_ref[...] + jnp.dot(y.astype(BF16), wo_ref[...], preferred_element_type=F32)
    x_out[...] = x
    route_out[...] = _route_record(_rms(x, fnw_ref[...]), wr_ref[...])


def odd_post(x, y_c, o0, o1, zs, dn_norm, ones_bd, w_out, ffn_norm, wr_pad, *, tm=512):
    S, D = x.shape
    rows = pl.BlockSpec((tm, WIDTH_D), lambda i: (i, 0))
    full = lambda a: pl.BlockSpec(a.shape, lambda i: (0,) * a.ndim)
    consts = [dn_norm, ones_bd, w_out, ffn_norm.reshape(1, D), wr_pad]
    return pl.pallas_call(
        _odd_post_kernel,
        grid=(S // tm,),
        in_specs=[pl.BlockSpec((tm, D), lambda i: (i, 0)), rows, rows, rows, rows] + [full(a) for a in consts],
        out_specs=[pl.BlockSpec((tm, D), lambda i: (i, 0)), pl.BlockSpec((tm, LANES), lambda i: (i, 0))],
        out_shape=[jax.ShapeDtypeStruct((S, D), F32), jax.ShapeDtypeStruct((S, LANES), F32)],
        compiler_params=_params("parallel"),
        name="odd_post",
    )(x, y_c, o0, o1, zs, *consts)


def _row_copy(src_hbm, src_row, dst_vmem, dst_row, sem):
    return pltpu.make_async_copy(src_hbm.at[pl.ds(src_row, 1)], dst_vmem.at[pl.ds(dst_row, 1)], sem)


def _moe_group_kernel(te_ref, tv_ref, tok_ref, x_hbm, nw_ref, wg_ref, wu_ref, wd_ref, o_ref, xbuf, h_scr, sem, *, tm, n_f):
    m = pl.program_id(0)
    f = pl.program_id(1)
    valid = tv_ref[m] > 0
    slot = m % 2
    part = tm // n_f

    @pl.when((f == 0) & (m == 0))
    def _():
        def start(j, carry):
            _row_copy(x_hbm, tok_ref[j], xbuf.at[0], j, sem.at[0]).start()
            return carry

        lax.fori_loop(0, tm, start, 0, unroll=8)

    fed = valid | ((m > 0) & (tv_ref[jnp.maximum(m - 1, 0)] > 0))

    @pl.when(fed & (f == 0))
    def _():
        pltpu.make_async_copy(x_hbm.at[pl.ds(0, tm)], xbuf.at[slot], sem.at[slot]).wait()

    @pl.when(valid & (f == 0))
    def _():
        h_scr[...] = _rms(xbuf[slot], nw_ref[...]).astype(BF16)

    for into in range(2):
        for ff in range(n_f):
            @pl.when(valid & (slot == 1 - into) & (f == ff))
            def _(into=into, ff=ff):
                nbase = (m + 1) * tm + ff * part
                for j in range(part):
                    _row_copy(x_hbm, tok_ref[nbase + j], xbuf.at[into], ff * part + j, sem.at[into]).start()

    quarter = tm // MOE_ROW_SPLITS
    need = (tv_ref[m] + quarter - 1) // quarter
    for n_q in range(1, MOE_ROW_SPLITS + 1):
        rows = n_q * quarter

        @pl.when(need == n_q)
        def _(rows=rows):
            h = h_scr[0:rows, :]
            g = jnp.dot(h, wg_ref[...], preferred_element_type=F32)
            u = jnp.dot(h, wu_ref[...], preferred_element_type=F32)
            y = jnp.dot((_silu(g) * u).astype(BF16), wd_ref[...], preferred_element_type=F32)

            @pl.when(f == 0)
            def _():
                o_ref[0:rows, :] = y
                if rows < tm:
                    o_ref[rows:, :] = jnp.zeros((tm - rows, o_ref.shape[1]), o_ref.dtype)

            @pl.when(f != 0)
            def _():
                o_ref[0:rows, :] += y

    @pl.when(jnp.logical_not(valid) & (f == 0))
    def _():
        o_ref[...] = jnp.zeros_like(o_ref)


def moe_grouped_ffn(x, nw, tile_expert, tile_rows, src_tok, wg, wu, wd, *, tm, tf=1792):
    S, D = x.shape
    E, _, F = wg.shape
    n_tiles = tile_expert.shape[0]
    nf = F // tf

    def w_in(m, f, te, tv, tok):
        return (te[m], 0, jnp.where(tv[m] > 0, f, nf - 1))

    def w_out(m, f, te, tv, tok):
        return (te[m], jnp.where(tv[m] > 0, f, nf - 1), 0)

    grid_spec = pltpu.PrefetchScalarGridSpec(
        num_scalar_prefetch=3,
        grid=(n_tiles, nf),
        in_specs=[
            pl.BlockSpec(memory_space=pl.ANY),
            pl.BlockSpec((1, D), lambda m, f, te, tv, tok: (0, 0)),
            pl.BlockSpec((None, D, tf), w_in),
            pl.BlockSpec((None, D, tf), w_in),
            pl.BlockSpec((None, tf, D), w_out),
        ],
        out_specs=pl.BlockSpec((tm, D), lambda m, f, te, tv, tok: (m, 0)),
        scratch_shapes=[pltpu.VMEM((2, tm, D), F32), pltpu.VMEM((tm, D), BF16), pltpu.SemaphoreType.DMA((2,))],
    )
    return pl.pallas_call(
        functools.partial(_moe_group_kernel, tm=tm, n_f=nf),
        grid_spec=grid_spec,
        out_shape=jax.ShapeDtypeStruct((n_tiles * tm, D), F32),
        compiler_params=_params("arbitrary", "arbitrary"),
        name="moe_grouped_ffn",
    )(tile_expert, tile_rows, src_tok, x, nw.reshape(1, D), wg, wu, wd)


def _moe_combine_kernel(pos_ref, x_ref, rt_ref, ys_hbm, o_ref, buf, sem, *, tc, n_tok):
    i = pl.program_id(0)
    slot = i % 2

    def gather_tile(tile, into):
        base = tile * tc

        def start(j, carry):
            for k in range(TOP_K):
                _row_copy(ys_hbm, pos_ref[k * n_tok + base + j], buf.at[into, k], j, sem.at[into]).start()
            return carry

        lax.fori_loop(0, tc, start, 0, unroll=8)

    @pl.when(i == 0)
    def _():
        gather_tile(0, 0)

    for into in range(2):
        @pl.when((i + 1 < pl.num_programs(0)) & (slot == 1 - into))
        def _(into=into):
            base = (i + 1) * tc
            for j in range(tc):
                for k in range(TOP_K):
                    _row_copy(ys_hbm, pos_ref[k * n_tok + base + j], buf.at[into, k], j, sem.at[into]).start()

    for k in range(TOP_K):
        pltpu.make_async_copy(ys_hbm.at[pl.ds(0, tc)], buf.at[slot, k], sem.at[slot]).wait()
    rt = rt_ref[...]
    o_ref[...] = (x_ref[...] + rt[:, ROUTE_LANE_G1:ROUTE_LANE_G1 + 1] * buf[slot, 0]
                  + rt[:, ROUTE_LANE_G2:ROUTE_LANE_G2 + 1] * buf[slot, 1])


def moe_combine(x, route, ys, pos, *, tc=512):
    S, D = x.shape
    grid_spec = pltpu.PrefetchScalarGridSpec(
        num_scalar_prefetch=1,
        grid=(S // tc,),
        in_specs=[
            pl.BlockSpec((tc, D), lambda i, pos: (i, 0)),
            pl.BlockSpec((tc, LANES), lambda i, pos: (i, 0)),
            pl.BlockSpec(memory_space=pl.ANY),
        ],
        out_specs=pl.BlockSpec((tc, D), lambda i, pos: (i, 0)),
        scratch_shapes=[pltpu.VMEM((2, TOP_K, tc, D), F32), pltpu.SemaphoreType.DMA((2,))],
    )
    return pl.pallas_call(
        functools.partial(_moe_combine_kernel, tc=tc, n_tok=S),
        grid_spec=grid_spec,
        out_shape=jax.ShapeDtypeStruct((S, D), F32),
        compiler_params=_params("arbitrary"),
        name="moe_combine",
    )(pos, x, route, ys)


def moe_top2(x, nw, route, wg, wu, wd, *, tm=512):
    S, D = x.shape
    E = wg.shape[0]
    experts = jnp.concatenate([route[:, ROUTE_LANE_I1], route[:, ROUTE_LANE_I2]]).astype(jnp.int32)
    onehot = (experts[:, None] == jnp.arange(E, dtype=jnp.int32)).astype(jnp.int32)
    csum = jnp.cumsum(onehot, axis=0)
    rank = jnp.sum(onehot * csum, axis=1) - 1
    padded = (csum[-1] + tm - 1) // tm * tm
    ends = jnp.cumsum(padded)
    pos = (jnp.sum(onehot * (ends - padded), axis=1) + rank).astype(jnp.int32)
    n_tiles = TOP_K * S // tm + E + 1
    tokens = jnp.tile(jnp.arange(S, dtype=jnp.int32), TOP_K)
    src_tok = jnp.zeros((n_tiles * tm,), jnp.int32).at[pos].set(tokens)
    tile_start = jnp.arange(n_tiles, dtype=jnp.int32) * tm
    tile_expert = jnp.minimum(jnp.sum(tile_start[:, None] >= ends[None, :], axis=1), E - 1).astype(jnp.int32)
    seg_end = (ends - padded + csum[-1])[tile_expert]
    tile_rows = jnp.clip(seg_end - tile_start, 0, tm).astype(jnp.int32)
    ys = moe_grouped_ffn(x, nw, tile_expert, tile_rows, src_tok, wg, wu, wd, tm=tm)
    return moe_combine(x, route, ys, pos)


ATTN_RADIUS = 64
ATTN_BQ = 128
ATTN_TILE = ATTN_BQ * max(d for _, d in DILATION_PATTERNS)
ATTN_HALO = ATTN_RADIUS * max(d for _, d in DILATION_PATTERNS)
ATTN_BLOCKS_IN_FLIGHT = 4
assert all(w // (2 * d) == ATTN_RADIUS for w, d in DILATION_PATTERNS)


def _attn_kernel(q_ref, kp_ref, kc_ref, kn_ref, vp_ref, vc_ref, vn_ref, o_ref, kbuf, vbuf, m_s, l_s, a_s, *, n_tok):
    tile, bq, halo, rad = ATTN_TILE, ATTN_BQ, ATTN_HALO, ATTN_RADIUS
    kbuf[0:halo, :] = kp_ref[...]
    kbuf[halo:halo + tile, :] = kc_ref[...]
    kbuf[halo + tile:, :] = kn_ref[...]
    vbuf[0:halo, :] = vp_ref[...]
    vbuf[halo:halo + tile, :] = vc_ref[...]
    vbuf[halo + tile:, :] = vn_ref[...]

    t0 = pl.program_id(1) * tile
    qi = lax.broadcasted_iota(jnp.int32, (bq, bq + 2 * rad), 0)
    kj = lax.broadcasted_iota(jnp.int32, (bq, bq + 2 * rad), 1)
    band = (kj >= qi) & (kj <= qi + 2 * rad)
    lo_half = lax.broadcasted_iota(jnp.int32, (bq, 2 * HEAD_DIM), 1) < HEAD_DIM

    for n_branch, (_, d) in enumerate(sorted(DILATION_PATTERNS, key=lambda wd: -wd[1])):
        span = bq * d
        reps = max(1, ATTN_BLOCKS_IN_FLIGHT // d)

        def block(blk, carry, d=d, span=span, reps=reps, first=n_branch == 0):
            stride = None if d == 1 else d
            problems = [(rep, r) for rep in range(reps) for r in range(d)]

            def scores(rep, r):
                base = pl.multiple_of(blk * (reps * span), span) + rep * span
                q_rows = pl.ds(base + r, bq, stride=stride)
                k_rows = pl.ds(halo + base + (r - rad * d), bq + 2 * rad, stride=stride)
                q = q_ref[q_rows, :]
                kw = kbuf[k_rows, :].astype(BF16)
                tok = t0 + base + (r - rad * d) + d * kj
                mask = band & (tok >= 0) & (tok < n_tok)
                sc = [jnp.where(mask, _bdot_nt(jnp.where(own, q, 0.0), kw), NEG_INF)
                      for own in (lo_half, jnp.logical_not(lo_half))]
                return q_rows, k_rows, sc

            pending = scores(*problems[0])
            for i in range(len(problems)):
                q_rows, k_rows, sc_pair = pending
                if i + 1 < len(problems):
                    pending = scores(*problems[i + 1])
                vw = vbuf[k_rows, :].astype(BF16)
                halves = []
                for sc in sc_pair:
                    m_h = jnp.max(sc, axis=-1, keepdims=True)
                    p = jnp.exp(sc - m_h)
                    halves.append((m_h, jnp.sum(p, axis=-1, keepdims=True),
                                   jnp.dot(p.astype(BF16), vw, preferred_element_type=F32)))
                m_b, l_b, a_b = (jnp.where(lo_half, x0, x1) for x0, x1 in zip(*halves))
                if first:
                    m_n, l_n, a_n = m_b, l_b, a_b
                else:
                    m_o = m_s[q_rows, :]
                    m_n = jnp.maximum(m_o, m_b)
                    w_o = jnp.exp(m_o - m_n)
                    w_b = jnp.exp(m_b - m_n)
                    l_n = l_s[q_rows, :] * w_o + l_b * w_b
                    a_n = a_s[q_rows, :] * w_o + a_b * w_b
                m_s[q_rows, :] = m_n
                l_s[q_rows, :] = l_n
                a_s[q_rows, :] = a_n
            return carry

        lax.fori_loop(0, tile // (reps * span), block, 0)

    o_ref[...] = a_s[...] / l_s[...]


def dilated_attention(q, k, v):
    S, W = q.shape
    pair = 2 * HEAD_DIM
    per = ATTN_TILE // ATTN_HALO
    cur = pl.BlockSpec((ATTN_TILE, pair), lambda p, i: (i, p))
    prv = pl.BlockSpec((ATTN_HALO, pair), lambda p, i: (jnp.maximum(i * per - 1, 0), p))
    nxt = pl.BlockSpec((ATTN_HALO, pair), lambda p, i: (jnp.minimum((i + 1) * per, S // ATTN_HALO - 1), p))
    ext = ATTN_TILE + 2 * ATTN_HALO
    return pl.pallas_call(
        functools.partial(_attn_kernel, n_tok=S),
        grid=(W // pair, S // ATTN_TILE),
        in_specs=[cur, prv, cur, nxt, prv, cur, nxt],
        out_specs=cur,
        out_shape=jax.ShapeDtypeStruct((S, W), F32),
        scratch_shapes=[pltpu.VMEM((ext, pair), F32), pltpu.VMEM((ext, pair), F32)]
        + [pltpu.VMEM((ATTN_TILE, pair), F32)] * 3,
        compiler_params=_params("parallel", "arbitrary"),
        name="dilated_attention",
    )(q, k, k, k, v, v, v)


def _tri_masks(z):
    row = lax.broadcasted_iota(jnp.int32, (CHUNK, CHUNK), 0)
    col = lax.broadcasted_iota(jnp.int32, (CHUNK, CHUNK), 1)
    if z == 0:
        return col <= row, col < row
    return col >= row, col > row


RWKV_SUB_CHUNKS = 2
DN_SUB_CHUNKS = 4


def _sub_rows(z, j, sub):
    i = j if z == 0 else sub - 1 - j
    return slice(i * CHUNK, (i + 1) * CHUNK)


def _neumann_solve(ns, xs):
    steps = CHUNK.bit_length() - 1
    for i in range(steps):
        xs = [x + _bdot(n, x) for n, x in zip(ns, xs)]
        if i + 1 < steps:
            ns = [_bdot(n, n) for n in ns]
    return xs


def _rwkv_chunk_kernel(*refs, n_cast):
    sub = RWKV_SUB_CHUNKS
    ins = (refs[0:6], refs[6:12])
    cast_in = refs[12:12 + n_cast]
    y_refs = refs[12 + n_cast:14 + n_cast]
    cast_out = refs[14 + n_cast:14 + 2 * n_cast]
    state = refs[14 + 2 * n_cast]
    for src, dst in zip(cast_in, cast_out):
        dst[...] = src[...].astype(BF16)

    @pl.when(pl.program_id(0) == 0)
    def _():
        state[...] = jnp.zeros_like(state)

    chains = [(j, z, h) for j in range(sub) for z in range(2) for h in range(N_HEADS_B)]
    masks = [_tri_masks(z) for z in range(2)]
    prep = {}
    for j in range(sub):
        for z in range(2):
            rows = _sub_rows(z, j, sub)
            r_ref, v_ref, a_ref, lw_ref, k_ref, b_ref = ins[z]
            tri = jnp.where(masks[z][0], 1.0, 0.0).astype(BF16)
            lw = lw_ref[rows, :]
            hi, mid, lo = _split3(lw)
            cum = (jnp.dot(tri, hi, preferred_element_type=F32) + jnp.dot(tri, mid, preferred_element_type=F32)
                   + jnp.dot(tri, lo, preferred_element_type=F32))
            tot = jnp.sum(lw, axis=0, keepdims=True)
            e_neg = jnp.exp(-cum)
            e_end = jnp.exp(tot - cum)
            k = k_ref[rows, :]
            b = b_ref[rows, :]
            prep[j, z] = dict(rt=r_ref[rows, :] * jnp.exp(cum), at=a_ref[rows, :] * jnp.exp(cum - lw), kt=k * e_neg,
                              bt=b * e_neg, kh=k * e_end, bh=b * e_end, dw=jnp.exp(tot), v=v_ref[rows, :])

    def part(name, c):
        j, z, h = c
        return prep[j, z][name][:, h * HEAD_DIM:(h + 1) * HEAD_DIM]

    xs = [_bdot_nt(jnp.concatenate([part("at", c), part("rt", c)], axis=0),
                   jnp.concatenate([part("bt", c), part("kt", c)], axis=0)) for c in chains]
    a_ab = [jnp.where(masks[c[1]][1], x[:CHUNK, :CHUNK], 0.0) for c, x in zip(chains, xs)]
    a_rb = [jnp.where(masks[c[1]][0], x[CHUNK:, :CHUNK], 0.0) for c, x in zip(chains, xs)]
    cys = [_bdot(jnp.concatenate([jnp.where(masks[c[1]][1], x[:CHUNK, CHUNK:], 0.0),
                                  jnp.where(masks[c[1]][0], x[CHUNK:, CHUNK:], 0.0)], axis=0), part("v", c))
           for c, x in zip(chains, xs)]
    pqs = _neumann_solve(a_ab, [jnp.concatenate([part("at", c), cy[:CHUNK]], axis=1) for c, cy in zip(chains, cys)])
    kvs = [_bdot_tn(part("v", c), part("kh", c)) for c in chains]
    per = 2 * N_HEADS_B
    s_cur = [state[z, h] for _, z, h in chains[:per]]
    for j in range(sub):
        sel = slice(j * per, (j + 1) * per)
        cs = chains[sel]
        us = [_bdot_nt(pq[:, :HEAD_DIM], s) + pq[:, HEAD_DIM:] for pq, s in zip(pqs[sel], s_cur)]
        ys = [_bdot_nt(part("rt", c), s) for c, s in zip(cs, s_cur)]
        ys = [y + _bdot(arb, u) + cy[CHUNK:] for y, arb, u, cy in zip(ys, a_rb[sel], us, cys[sel])]
        s_cur = [s * part("dw", c) + _bdot_tn(u, part("bh", c)) + kv for c, s, u, kv in zip(cs, s_cur, us, kvs[sel])]
        for (_, z, h), y in zip(cs, ys):
            y_refs[z][_sub_rows(z, j, sub), h * HEAD_DIM:(h + 1) * HEAD_DIM] = y
    for (_, z, h), s in zip(chains[:per], s_cur):
        state[z, h] = s


def rwkv7_scan(r, v, a, lw, k, b, cast_through=()):
    S, C = r.shape
    step_rows = RWKV_SUB_CHUNKS * CHUNK
    n = S // step_rows
    fwd = pl.BlockSpec((step_rows, C), lambda c: (c, 0))
    bwd = pl.BlockSpec((step_rows, C), lambda c: (n - 1 - c, 0))
    cast_specs = [pl.BlockSpec((w.shape[0] // n, w.shape[1]), lambda c: (c, 0)) for w in cast_through]
    y0, y1, *cast = pl.pallas_call(
        functools.partial(_rwkv_chunk_kernel, n_cast=len(cast_through)),
        grid=(n,),
        in_specs=[fwd] * 6 + [bwd] * 6 + cast_specs,
        out_specs=[fwd, bwd] + cast_specs,
        out_shape=[jax.ShapeDtypeStruct((S, C), F32)] * 2 + [jax.ShapeDtypeStruct(w.shape, BF16) for w in cast_through],
        scratch_shapes=[pltpu.VMEM((2, N_HEADS_B, HEAD_DIM, HEAD_DIM), F32)],
        compiler_params=_params("arbitrary"),
        name="rwkv7_scan",
    )(r, v, a, lw[0], k[0], b[0], r, v, a, lw[1], k[1], b[1], *cast_through)
    return y0, y1, cast


def _dn_chunk_kernel(*refs):
    sub = DN_SUB_CHUNKS
    ins = (refs[0:6], refs[6:12])
    o_refs = refs[12:14]
    state = refs[14]

    @pl.when(pl.program_id(0) == 0)
    def _():
        state[...] = jnp.zeros_like(state)

    chains = [(j, z, h) for j in range(sub) for z in range(2) for h in range(N_HEADS_D)]
    masks = [_tri_masks(z) for z in range(2)]
    nt = (((1,), (1,)), ((), ()))
    gcs, decays, betas, g_lasts = [], [], [], []
    for j in range(sub):
        for z in range(2):
            rows = _sub_rows(z, j, sub)
            _, _, _, bcol_ref, gcol_ref, grow_ref = ins[z]
            incl = masks[z][0]
            tri = jnp.where(incl, 1.0, 0.0).astype(BF16)
            c_hi, c_mid, c_lo = _split3(gcol_ref[rows, :])
            gc_cols = (jnp.dot(tri, c_hi, preferred_element_type=F32) + jnp.dot(tri, c_mid, preferred_element_type=F32)
                       + jnp.dot(tri, c_lo, preferred_element_type=F32))
            r_hi, r_mid, r_lo = _split3(grow_ref[rows.start // CHUNK])
            gc_rows = (lax.dot_general(r_hi, tri, nt, preferred_element_type=F32)
                       + lax.dot_general(r_mid, tri, nt, preferred_element_type=F32)
                       + lax.dot_general(r_lo, tri, nt, preferred_element_type=F32))
            last = CHUNK - 1 if z == 0 else 0
            bcol = bcol_ref[rows, :]
            for h in range(N_HEADS_D):
                idx = z * N_HEADS_D + h
                gc = gc_cols[:, idx:idx + 1]
                diff = gc - gc_rows[idx:idx + 1, :]
                gcs.append(gc)
                decays.append(jnp.where(incl, jnp.exp(jnp.where(incl, diff, 0.0)), 0.0))
                betas.append(bcol[:, idx:idx + 1])
                g_lasts.append(gc[last:last + 1, :])

    def part(i, c):
        j, z, h = c
        return ins[z][i][_sub_rows(z, j, sub), h * HEAD_DIM_D:(h + 1) * HEAD_DIM_D]

    qs = [part(0, c) for c in chains]
    ks = [part(1, c) for c in chains]
    vs = [part(2, c) for c in chains]
    kbs = [k * beta for k, beta in zip(ks, betas)]
    e_gcs = [jnp.exp(gc) for gc in gcs]
    kqs = [_bdot_nt(jnp.concatenate([kb, q], axis=0), k) for kb, q, k in zip(kbs, qs, ks)]
    n_mats = [jnp.where(masks[c[1]][1], -(kq[:CHUNK] * dc), 0.0) for c, kq, dc in zip(chains, kqs, decays)]
    attns = [kq[CHUNK:] * dc for kq, dc in zip(kqs, decays)]
    uks = _neumann_solve(n_mats, [jnp.concatenate([v * beta, kb * e], axis=1)
                                  for v, beta, kb, e in zip(vs, betas, kbs, e_gcs)])
    per = 2 * N_HEADS_D
    s_cur = [state[z, h] for _, z, h in chains[:per]]
    for j in range(sub):
        sel = slice(j * per, (j + 1) * per)
        us = [uk[:, :HEAD_DIM_D] - _bdot(uk[:, HEAD_DIM_D:], s) for uk, s in zip(uks[sel], s_cur)]
        os_ = [_bdot(q * e, s) for q, e, s in zip(qs[sel], e_gcs[sel], s_cur)]
        os_ = [o + _bdot(attn, u) for o, attn, u in zip(os_, attns[sel], us)]
        s_cur = [s * jnp.exp(gl) + _bdot_tn(k * jnp.exp(gl - gc), u)
                 for s, gl, k, gc, u in zip(s_cur, g_lasts[sel], ks[sel], gcs[sel], us)]
        for (_, z, h), o in zip(chains[sel], os_):
            o_refs[z][_sub_rows(z, j, sub), h * HEAD_DIM_D:(h + 1) * HEAD_DIM_D] = o
    for (_, z, h), s in zip(chains[:per], s_cur):
        state[z, h] = s


def deltanet_scan(q, k, v, beta, g):
    S, C = q.shape
    step_rows = DN_SUB_CHUNKS * CHUNK
    n = S // step_rows
    nz = 2 * N_HEADS_D
    g_rows = g.reshape(S // CHUNK, CHUNK, nz).transpose(0, 2, 1)

    def specs(idx):
        wide = pl.BlockSpec((step_rows, C), lambda c: (idx(c), 0))
        col = pl.BlockSpec((step_rows, nz), lambda c: (idx(c), 0))
        row = pl.BlockSpec((DN_SUB_CHUNKS, nz, CHUNK), lambda c: (idx(c), 0, 0))
        return [wide, wide, wide, col, col, row], wide

    in_f, out_f = specs(lambda c: c)
    in_b, out_b = specs(lambda c: n - 1 - c)
    o0, o1 = pl.pallas_call(
        _dn_chunk_kernel,
        grid=(n,),
        in_specs=in_f + in_b,
        out_specs=[out_f, out_b],
        out_shape=[jax.ShapeDtypeStruct((S, C), F32)] * 2,
        scratch_shapes=[pltpu.VMEM((2, N_HEADS_D, HEAD_DIM_D, HEAD_DIM_D), F32)],
        compiler_params=_params("arbitrary"),
        name="deltanet_scan",
    )(q, k, v, beta, g, g_rows, q, k, v, beta, g, g_rows)
    return o0, o1


def _rope_tables(positions):
    half = ROPE_DIM // 2
    dim = jnp.arange(2 * HEAD_DIM) % HEAD_DIM
    freq = jnp.where(dim < ROPE_DIM, jnp.power(ROPE_THETA, -(dim % half).astype(F32) / half), 0.0)
    sign = jnp.where(dim < half, -1.0, 1.0)
    ang = positions[:, None].astype(F32) * freq
    return jnp.cos(ang), sign * jnp.sin(ang)


def _ones_block_diag(width, group):
    idx = jnp.arange(width) // group
    return (idx[:, None] == idx[None, :]).astype(BF16)


def _block_diag(blocks):
    rows = sum(b.shape[0] for b in blocks)
    cols = sum(b.shape[1] for b in blocks)
    out = jnp.zeros((rows, cols), blocks[0].dtype)
    r = c = 0
    for b in blocks:
        out = lax.dynamic_update_slice(out, b, (r, c))
        r += b.shape[0]
        c += b.shape[1]
    return out


def _even_layer(x, positions, mix_norm, w_in, q_norm, k_norm, shift_mu, lora_mu, w0, w1, w2, a0, a1, a2, g1, g2,
                k_k, k_a, r_k, ln_w, ln_b, w_out, ffn_norm, ffn_gate, ffn_up, ffn_down, cast_through=()):
    row = lambda t: t.reshape(1, -1)
    cos_t, sin_t = _rope_tables(positions)
    ones_bd = _ones_block_diag(WIDTH_B, HEAD_DIM)
    lora_in = jnp.concatenate([w1[0], w1[1], a1[0], a1[1], g1], axis=1).astype(BF16)
    lora_out = _block_diag([w2[0], w2[1], a2[0], a2[1], g2]).astype(BF16)
    (q, k, v, r, vb, a_vec, lw0, lw1, k0, k1, b0, b1, gate, bonus) = even_prep(
        x, mix_norm, w_in.astype(BF16), row(jnp.tile(q_norm, N_HEADS_A)), row(jnp.tile(k_norm, N_HEADS_A)), cos_t, sin_t,
        shift_mu, lora_mu, lora_in, lora_out, w0, a0, row(k_k), row(k_a), row(r_k), ones_bd)
    y_a = dilated_attention(q, k, v)
    y0, y1, cast = rwkv7_scan(r, vb, a_vec, (lw0, lw1), (k0, k1), (b0, b1),
                              [w.reshape(-1, w.shape[-1]) for w in cast_through])
    cast = [c.reshape(w.shape) for c, w in zip(cast, cast_through)]
    x = even_post_ffn(x, y_a, y0, y1, gate, bonus, row(ln_w), row(ln_b), ones_bd, w_out.astype(BF16), ffn_norm,
                      ffn_gate.astype(BF16), ffn_up.astype(BF16), ffn_down.astype(BF16))
    return x, cast


def _odd_layer(x, mix_norm, w_in, conv_c, conv_dn, A_log, dt_bias, dn_norm, w_out, ffn_norm, router, moe_gate, moe_up, moe_down):
    n_in = w_in.shape[1]
    n_pad = -(-n_in // LANES) * LANES
    w_in_p = jnp.pad(w_in, ((0, 0), (0, n_pad - n_in))).astype(BF16)
    nz = 2 * N_HEADS_D
    neg_a = jnp.zeros((1, LANES), F32).at[0, nz:2 * nz].set(-jnp.exp(A_log.reshape(-1)))
    dt_b = jnp.zeros((1, LANES), F32).at[0, nz:2 * nz].set(dt_bias.reshape(-1))
    ones_bd = _ones_block_diag(WIDTH_D, HEAD_DIM_D)
    y_c, q, k, v, zs, bg = odd_prep(x, mix_norm, w_in_p, conv_c, conv_dn, neg_a, dt_b, ones_bd)
    o0, o1 = deltanet_scan(q, k, v, bg[:, :nz], bg[:, nz:2 * nz])
    wr_pad = jnp.pad(router, ((0, 0), (0, LANES - N_EXPERTS)))
    x, route = odd_post(x, y_c, o0, o1, zs, jnp.tile(dn_norm, N_HEADS_D).reshape(1, -1), ones_bd, w_out.astype(BF16),
                        ffn_norm, wr_pad)
    return moe_top2(x, ffn_norm, route, moe_gate.astype(BF16), moe_up.astype(BF16), moe_down.astype(BF16))


def kernel(x, positions, ev_mix_norm, ev_w_in, ev_q_norm, ev_k_norm, ev_shift_mu, ev_lora_mu, ev_w0, ev_w1, ev_w2, ev_a0, ev_a1, ev_a2, ev_g1, ev_g2, ev_k_k, ev_k_a, ev_r_k, ev_ln_w, ev_ln_b, ev_w_out, ev_ffn_norm, ev_ffn_gate, ev_ffn_up, ev_ffn_down, od_mix_norm, od_w_in, od_conv_c, od_conv_dn, od_A_log, od_dt_bias, od_dn_norm, od_w_out, od_ffn_norm, od_router, od_moe_gate, od_moe_up, od_moe_down):
    B, S, D = x.shape
    assert B == 1
    xs = x.reshape(S, D)
    pos = positions.reshape(S)
    n_layers = ev_mix_norm.shape[0] + od_mix_norm.shape[0]
    for layer in range(n_layers):
        i = layer // 2
        if layer % 2 == 0:
            nxt = (od_moe_gate[i], od_moe_up[i], od_moe_down[i]) if layer + 1 < n_layers else ()
            xs, moe_w = _even_layer(xs, pos, ev_mix_norm[i], ev_w_in[i], ev_q_norm[i], ev_k_norm[i], ev_shift_mu[i],
                                    ev_lora_mu[i], ev_w0[i], ev_w1[i], ev_w2[i], ev_a0[i], ev_a1[i], ev_a2[i], ev_g1[i],
                                    ev_g2[i], ev_k_k[i], ev_k_a[i], ev_r_k[i], ev_ln_w[i], ev_ln_b[i], ev_w_out[i],
                                    ev_ffn_norm[i], ev_ffn_gate[i], ev_ffn_up[i], ev_ffn_down[i], cast_through=nxt)
        else:
            xs = _odd_layer(xs, od_mix_norm[i], od_w_in[i], od_conv_c[i], od_conv_dn[i], od_A_log[i], od_dt_bias[i],
                            od_dn_norm[i], od_w_out[i], od_ffn_norm[i], od_router[i], *moe_w)
    return xs.reshape(B, S, D)
```

```python
import functools

import jax
import jax.numpy as jnp
from jax import lax
from jax.experimental import pallas as pl
from jax.experimental.pallas import tpu as pltpu

F32 = jnp.float32
BF16 = jnp.bfloat16

HEAD_DIM = 64
N_HEADS_A = 8
WIDTH_A = N_HEADS_A * HEAD_DIM
DILATION_PATTERNS = ((128, 1), (512, 4), (2048, 16))
ROPE_DIM = HEAD_DIM // 4
ROPE_THETA = 500000.0
N_HEADS_B = 8
WIDTH_B = N_HEADS_B * HEAD_DIM
DECAY_LORA, ICLR_LORA, GATE_LORA = 64, 64, 128
RWKV_LN_EPS = 64e-5
WIDTH_C = 512
N_HEADS_D = 4
HEAD_DIM_D = 128
WIDTH_D = N_HEADS_D * HEAD_DIM_D
CHUNK = 64
N_EXPERTS = 8
TOP_K = 2
ROUTE_LANE_I1, ROUTE_LANE_I2, ROUTE_LANE_G1, ROUTE_LANE_G2 = 8, 9, 10, 11
MOE_ROW_SPLITS = 4
NORM_EPS = 1e-6
NEG_INF = -1e30

V7X_VMEM_LIMIT_BYTES = 56 * 1024 * 1024
LANES = 128


def _params(*sem):
    return pltpu.CompilerParams(dimension_semantics=sem, vmem_limit_bytes=V7X_VMEM_LIMIT_BYTES)


def _bdot(a, b):
    return jnp.dot(a.astype(BF16), b.astype(BF16), preferred_element_type=F32)


def _bdot_nt(a, b):
    return lax.dot_general(a.astype(BF16), b.astype(BF16), (((1,), (1,)), ((), ())), preferred_element_type=F32)


def _bdot_tn(a, b):
    return lax.dot_general(a.astype(BF16), b.astype(BF16), (((0,), (0,)), ((), ())), preferred_element_type=F32)


def _split3(x):
    hi = x.astype(BF16)
    r1 = x - hi.astype(F32)
    mid = r1.astype(BF16)
    lo = (r1 - mid.astype(F32)).astype(BF16)
    return hi, mid, lo


def _rms(x, w):
    return x * lax.rsqrt(jnp.mean(x * x, axis=-1, keepdims=True) + NORM_EPS) * w


HALO = 8


def _group_sum(x, ones_bd):
    hi = x.astype(BF16)
    lo = (x - hi.astype(F32)).astype(BF16)
    return jnp.dot(hi, ones_bd, preferred_element_type=F32) + jnp.dot(lo, ones_bd, preferred_element_type=F32)


def _silu(x):
    return x * jax.nn.sigmoid(x)


def _halo_specs(tm, n_rows, width):
    per = tm // HALO
    cur = pl.BlockSpec((tm, width), lambda i: (i, 0))
    prv = pl.BlockSpec((HALO, width), lambda i: (jnp.maximum(i * per - 1, 0), 0))
    nxt = pl.BlockSpec((HALO, width), lambda i: (jnp.minimum((i + 1) * per, n_rows // HALO - 1), 0))
    return cur, prv, nxt


def _make_shifts(tm, n_rows):
    row = pl.program_id(0) * tm + lax.broadcasted_iota(jnp.int32, (tm, 1), 0)
    first, last = row == 0, row == n_rows - 1
    n_ext = tm + 2 * HALO

    def shifts(t):
        prev = jnp.where(first, 0.0, pltpu.roll(t, 1, 0)[HALO:HALO + tm])
        nxt = jnp.where(last, 0.0, pltpu.roll(t, n_ext - 1, 0)[HALO:HALO + tm])
        return prev, t[HALO:HALO + tm], nxt

    return shifts


def _even_prep_kernel(xc_ref, xp_ref, xn_ref, nw_ref, win_ref, qn_ref, kn_ref, cos_ref, sin_ref, smu_ref, lmu_ref,
                      lin_ref, lout_ref, w0_ref, a0_ref, kk_ref, ka_ref, rk_ref, ones_ref,
                      q_out, k_out, v_out, r_out, vb_out, a_out, lw0_out, lw1_out, k0_out, k1_out, b0_out, b1_out,
                      gate_out, bonus_out, *, tm, n_rows):
    shifts = _make_shifts(tm, n_rows)
    he = _rms(jnp.concatenate([xp_ref[...], xc_ref[...], xn_ref[...]], axis=0), nw_ref[...])
    proj = jnp.dot(he.astype(BF16), win_ref[...], preferred_element_type=F32)
    ones = ones_ref[...]
    lane = lax.broadcasted_iota(jnp.int32, (tm, WIDTH_A), 1) % HEAD_DIM

    n_pairs = WIDTH_A // cos_ref.shape[1]
    cos_t = jnp.concatenate([cos_ref[...]] * n_pairs, axis=1)
    sin_t = jnp.concatenate([sin_ref[...]] * n_pairs, axis=1)

    def head_rms_rope(t, w):
        t = t * lax.rsqrt(_group_sum(t * t, ones) * (1.0 / HEAD_DIM) + NORM_EPS) * w
        half = ROPE_DIM // 2
        swapped = jnp.where(lane < half, pltpu.roll(t, WIDTH_A - half, 1), pltpu.roll(t, half, 1))
        return t * cos_t + swapped * sin_t

    cur = proj[HALO:HALO + tm]
    q = head_rms_rope(cur[:, :WIDTH_A], qn_ref[...]) * HEAD_DIM ** -0.5
    k = head_rms_rope(cur[:, WIDTH_A:2 * WIDTH_A], kn_ref[...])
    v = cur[:, 2 * WIDTH_A:3 * WIDTH_A]
    q_out[...] = q
    k_out[...] = k
    v_out[...] = v

    p_prev, p_cur, p_next = shifts(proj[:, 3 * WIDTH_A:])
    smu = smu_ref[...]
    rkv = p_cur + smu[0:1] * (p_prev - p_cur) + smu[1:2] * (p_next - p_cur)
    r, kin, vb = rkv[:, :WIDTH_B], rkv[:, WIDTH_B:2 * WIDTH_B], rkv[:, 2 * WIDTH_B:]
    h_prev, h_cur, h_next = shifts(he)
    lmu = lmu_ref[...]
    hx = h_cur + lmu[0:1] * (h_prev - h_cur) + lmu[1:2] * (h_next - h_cur)
    l1 = jnp.dot(hx.astype(BF16), lin_ref[...], preferred_element_type=F32)
    n_w, n_a = 2 * DECAY_LORA, 2 * ICLR_LORA
    l1 = jnp.concatenate([jnp.tanh(l1[:, :n_w]), l1[:, n_w:n_w + n_a], jax.nn.sigmoid(l1[:, n_w + n_a:])], axis=1)
    l2 = jnp.dot(l1.astype(BF16), lout_ref[...], preferred_element_type=F32)
    w0, a0 = w0_ref[...], a0_ref[...]
    kk = kin * kk_ref[...]
    kk = kk * lax.rsqrt(_group_sum(kk * kk, ones) + 1e-6)
    kdirs = []
    for z, (lw_out, k_out_z, b_out_z) in enumerate(((lw0_out, k0_out, b0_out), (lw1_out, k1_out, b1_out))):
        w_pre = l2[:, z * WIDTH_B:(z + 1) * WIDTH_B] + w0[z:z + 1]
        lw_out[...] = -jnp.exp(-0.5) * jax.nn.sigmoid(w_pre)
        iclr = jax.nn.sigmoid(l2[:, (2 + z) * WIDTH_B:(3 + z) * WIDTH_B] + a0[z:z + 1])
        kdir = kin * (1.0 + (iclr - 1.0) * ka_ref[...])
        k_out_z[...] = kdir
        b_out_z[...] = kk * iclr
        kdirs.append(kdir)
    r_out[...] = r
    vb_out[...] = vb
    a_out[...] = -kk
    gate_out[...] = l2[:, 4 * WIDTH_B:]
    bonus_out[...] = _group_sum(r * (kdirs[0] + kdirs[1]) * rk_ref[...], ones) * vb


def even_prep(x, mix_norm, w_in, q_norm, k_norm, cos_t, sin_t, shift_mu, lora_mu, lora_in, lora_out, w0, a0, k_k, k_a,
              r_k, ones_bd, *, tm=256):
    S, D = x.shape
    cur, prv, nxt = _halo_specs(tm, S, D)
    full = lambda a: pl.BlockSpec(a.shape, lambda i: (0,) * a.ndim)
    rows = pl.BlockSpec((tm, WIDTH_B), lambda i: (i, 0))
    consts = [mix_norm.reshape(1, D), w_in, q_norm, k_norm]
    consts2 = [shift_mu, lora_mu, lora_in, lora_out, w0, a0, k_k, k_a, r_k, ones_bd]
    return pl.pallas_call(
        functools.partial(_even_prep_kernel, tm=tm, n_rows=S),
        grid=(S // tm,),
        in_specs=[cur, prv, nxt] + [full(a) for a in consts]
        + [pl.BlockSpec((tm, cos_t.shape[1]), lambda i: (i, 0))] * 2 + [full(a) for a in consts2],
        out_specs=[rows] * 14,
        out_shape=[jax.ShapeDtypeStruct((S, WIDTH_B), F32)] * 14,
        compiler_params=_params("parallel"),
        name="even_prep",
    )(x, x, x, *consts, cos_t, sin_t, *consts2)


def _odd_prep_kernel(xc_ref, xp_ref, xn_ref, nw_ref, win_ref, cc_ref, cdn_ref, nega_ref, dtb_ref, ones_ref,
                     yc_out, q_out, k_out, v_out, zs_out, bg_out, *, tm, n_rows):
    shifts = _make_shifts(tm, n_rows)
    he = _rms(jnp.concatenate([xp_ref[...], xc_ref[...], xn_ref[...]], axis=0), nw_ref[...])
    proj = jnp.dot(he.astype(BF16), win_ref[...], preferred_element_type=F32)
    cur = proj[HALO:HALO + tm]

    def conv3(t, w):
        prev, mid, nxt = shifts(t)
        return w[0:1] * prev + w[1:2] * mid + w[2:3] * nxt

    o_dn = 3 * WIDTH_C
    yc_out[...] = cur[:, :WIDTH_C] * conv3(proj[:, WIDTH_C:2 * WIDTH_C] * proj[:, 2 * WIDTH_C:o_dn], cc_ref[...])
    qkv = _silu(conv3(proj[:, o_dn:o_dn + 3 * WIDTH_D], cdn_ref[...]))
    ones = ones_ref[...]
    l2n = lambda t: t * lax.rsqrt(_group_sum(t * t, ones) + 1e-6)
    q_out[...] = l2n(qkv[:, :WIDTH_D]) * HEAD_DIM_D ** -0.5
    k_out[...] = l2n(qkv[:, WIDTH_D:2 * WIDTH_D])
    v_out[...] = qkv[:, 2 * WIDTH_D:]
    zs_out[...] = _silu(cur[:, o_dn + 3 * WIDTH_D:o_dn + 4 * WIDTH_D])
    tail = cur[:, o_dn + 4 * WIDTH_D:]
    t = tail + dtb_ref[...]
    softplus = jnp.maximum(t, 0.0) + jnp.log(1.0 + jnp.exp(-jnp.abs(t)))
    lane = lax.broadcasted_iota(jnp.int32, tail.shape, 1)
    bg_out[...] = jnp.where(lane < 2 * N_HEADS_D, jax.nn.sigmoid(tail), nega_ref[...] * softplus)


def odd_prep(x, mix_norm, w_in_pad, conv_c, conv_dn, neg_a, dt_b, ones_bd, *, tm=512):
    S, D = x.shape
    cur, prv, nxt = _halo_specs(tm, S, D)
    full = lambda a: pl.BlockSpec(a.shape, lambda i: (0,) * a.ndim)
    rows = pl.BlockSpec((tm, WIDTH_D), lambda i: (i, 0))
    consts = [mix_norm.reshape(1, D), w_in_pad, conv_c, conv_dn, neg_a, dt_b, ones_bd]
    return pl.pallas_call(
        functools.partial(_odd_prep_kernel, tm=tm, n_rows=S),
        grid=(S // tm,),
        in_specs=[cur, prv, nxt] + [full(a) for a in consts],
        out_specs=[rows] * 5 + [pl.BlockSpec((tm, LANES), lambda i: (i, 0))],
        out_shape=[jax.ShapeDtypeStruct((S, WIDTH_D), F32)] * 5 + [jax.ShapeDtypeStruct((S, LANES), F32)],
        compiler_params=_params("parallel"),
        name="odd_prep",
    )(x, x, x, *consts)


def _even_post_ffn_kernel(x_ref, ya_ref, y0_ref, y1_ref, gate_ref, bonus_ref, lnw_ref, lnb_ref, ones_ref, wo_ref,
                          fnw_ref, wg_ref, wu_ref, wd_ref, o_ref, h_scr):
    @pl.when(pl.program_id(1) == 0)
    def _():
        ones = ones_ref[...]
        yf = y0_ref[...] + y1_ref[...]
        dev = yf - _group_sum(yf, ones) * (1.0 / HEAD_DIM)
        var = _group_sum(dev * dev, ones) * (1.0 / HEAD_DIM)
        yn = dev * lax.rsqrt(var + RWKV_LN_EPS) * lnw_ref[...] + lnb_ref[...]
        y_b = (yn + bonus_ref[...]) * gate_ref[...]
        y = jnp.concatenate([ya_ref[...], y_b], axis=1)
        x = x_ref[...] + jnp.dot(y.astype(BF16), wo_ref[...], preferred_element_type=F32)
        h_scr[...] = _rms(x, fnw_ref[...]).astype(BF16)
        o_ref[...] = x

    h = h_scr[...]
    g = jnp.dot(h, wg_ref[...], preferred_element_type=F32)
    u = jnp.dot(h, wu_ref[...], preferred_element_type=F32)
    o_ref[...] += jnp.dot((_silu(g) * u).astype(BF16), wd_ref[...], preferred_element_type=F32)


def even_post_ffn(x, y_a, y0, y1, gate, bonus, ln_w, ln_b, ones_bd, w_out, ffn_norm, wg, wu, wd, *, tm=512, tf=1408):
    S, D = x.shape
    F = wg.shape[1]
    rows = pl.BlockSpec((tm, WIDTH_B), lambda i, f: (i, 0))
    full = lambda a: pl.BlockSpec(a.shape, lambda i, f: (0,) * a.ndim)
    consts = [ln_w, ln_b, ones_bd, w_out, ffn_norm.reshape(1, D)]
    return pl.pallas_call(
        _even_post_ffn_kernel,
        grid=(S // tm, F // tf),
        in_specs=[pl.BlockSpec((tm, D), lambda i, f: (i, 0)),
                  rows, rows, rows, rows, rows] + [full(a) for a in consts] + [
            pl.BlockSpec((D, tf), lambda i, f: (0, f)),
            pl.BlockSpec((D, tf), lambda i, f: (0, f)),
            pl.BlockSpec((tf, D), lambda i, f: (f, 0)),
        ],
        out_specs=pl.BlockSpec((tm, D), lambda i, f: (i, 0)),
        out_shape=jax.ShapeDtypeStruct((S, D), F32),
        scratch_shapes=[pltpu.VMEM((tm, D), BF16)],
        compiler_params=_params("parallel", "arbitrary"),
        name="even_post_ffn",
    )(x, y_a, y0, y1, gate, bonus, *consts, wg, wu, wd)


def _route_record(h, wr):
    h_hi, w_hi = h.astype(BF16), wr.astype(BF16)
    h_lo, w_lo = (h - h_hi.astype(F32)).astype(BF16), (wr - w_hi.astype(F32)).astype(BF16)
    logits = (jnp.dot(h_hi, w_hi, preferred_element_type=F32) + jnp.dot(h_lo, w_hi, preferred_element_type=F32)
              + jnp.dot(h_hi, w_lo, preferred_element_type=F32))
    lane = lax.broadcasted_iota(jnp.int32, logits.shape, 1)
    valid = lane < N_EXPERTS
    lg = jnp.where(valid, logits, NEG_INF)
    e = jnp.exp(lg - jnp.max(lg, axis=-1, keepdims=True))
    p = e / jnp.sum(e, axis=-1, keepdims=True)
    pm = jnp.where(valid, p, -1.0)
    m1 = jnp.max(pm, axis=-1, keepdims=True)
    i1 = jnp.min(jnp.where(pm == m1, lane, LANES), axis=-1, keepdims=True)
    pm2 = jnp.where(lane == i1, -1.0, pm)
    m2 = jnp.max(pm2, axis=-1, keepdims=True)
    i2 = jnp.min(jnp.where(pm2 == m2, lane, LANES), axis=-1, keepdims=True)
    tot = m1 + m2
    g1, g2 = m1 / tot, m2 / tot
    out = jnp.where(lane == i1, g1, 0.0) + jnp.where(lane == i2, g2, 0.0)
    out = jnp.where(lane == ROUTE_LANE_I1, i1.astype(F32), out)
    out = jnp.where(lane == ROUTE_LANE_I2, i2.astype(F32), out)
    out = jnp.where(lane == ROUTE_LANE_G1, g1, out)
    return jnp.where(lane == ROUTE_LANE_G2, g2, out)


def _odd_post_kernel(x_ref, yc_ref, o0_ref, o1_ref, zs_ref, dnw_ref, ones_ref, wo_ref, fnw_ref, wr_ref, x_out, route_out):
    o = o0_ref[...] + o1_ref[...]
    ms = _group_sum(o * o, ones_ref[...]) * (1.0 / HEAD_DIM_D)
    y_d = o * lax.rsqrt(ms + NORM_EPS) * dnw_ref[...] * zs_ref[...]
    y = jnp.concatenate([yc_ref[...], y_d], axis=1)
    x = x_ref[...] + jnp.dot(y.astype(BF16), wo_ref[...], preferred_element_type=F32)
    x_out[...] = x
    route_out[...] = _route_record(_rms(x, fnw_ref[...]), wr_ref[...])


def odd_post(x, y_c, o0, o1, zs, dn_norm, ones_bd, w_out, ffn_norm, wr_pad, *, tm=512):
    S, D = x.shape
    rows = pl.BlockSpec((tm, WIDTH_D), lambda i: (i, 0))
    full = lambda a: pl.BlockSpec(a.shape, lambda i: (0,) * a.ndim)
    consts = [dn_norm, ones_bd, w_out, ffn_norm.reshape(1, D), wr_pad]
    return pl.pallas_call(
        _odd_post_kernel,
        grid=(S // tm,),
        in_specs=[pl.BlockSpec((tm, D), lambda i: (i, 0)), rows, rows, rows, rows] + [full(a) for a in consts],
        out_specs=[pl.BlockSpec((tm, D), lambda i: (i, 0)), pl.BlockSpec((tm, LANES), lambda i: (i, 0))],
        out_shape=[jax.ShapeDtypeStruct((S, D), F32), jax.ShapeDtypeStruct((S, LANES), F32)],
        compiler_params=_params("parallel"),
        name="odd_post",
    )(x, y_c, o0, o1, zs, *consts)


def _row_copy(src_hbm, src_row, dst_vmem, dst_row, sem):
    return pltpu.make_async_copy(src_hbm.at[pl.ds(src_row, 1)], dst_vmem.at[pl.ds(dst_row, 1)], sem)


def _moe_group_kernel(te_ref, tv_ref, tok_ref, x_hbm, nw_ref, wg_ref, wu_ref, wd_ref, o_ref, xbuf, h_scr, sem, *, tm, n_f):
    m = pl.program_id(0)
    f = pl.program_id(1)
    valid = tv_ref[m] > 0
    slot = m % 2
    part = tm // n_f

    @pl.when((f == 0) & (m == 0))
    def _():
        def start(j, carry):
            _row_copy(x_hbm, tok_ref[j], xbuf.at[0], j, sem.at[0]).start()
            return carry

        lax.fori_loop(0, tm, start, 0, unroll=8)

    fed = valid | ((m > 0) & (tv_ref[jnp.maximum(m - 1, 0)] > 0))

    @pl.when(fed & (f == 0))
    def _():
        pltpu.make_async_copy(x_hbm.at[pl.ds(0, tm)], xbuf.at[slot], sem.at[slot]).wait()

    @pl.when(valid & (f == 0))
    def _():
        h_scr[...] = _rms(xbuf[slot], nw_ref[...]).astype(BF16)

    for into in range(2):
        for ff in range(n_f):
            @pl.when(valid & (slot == 1 - into) & (f == ff))
            def _(into=into, ff=ff):
                nbase = (m + 1) * tm + ff * part
                for j in range(part):
                    _row_copy(x_hbm, tok_ref[nbase + j], xbuf.at[into], ff * part + j, sem.at[into]).start()

    quarter = tm // MOE_ROW_SPLITS
    need = (tv_ref[m] + quarter - 1) // quarter
    for n_q in range(1, MOE_ROW_SPLITS + 1):
        rows = n_q * quarter

        @pl.when(need == n_q)
        def _(rows=rows):
            h = h_scr[0:rows, :]
            g = jnp.dot(h, wg_ref[...], preferred_element_type=F32)
            u = jnp.dot(h, wu_ref[...], preferred_element_type=F32)
            y = jnp.dot((_silu(g) * u).astype(BF16), wd_ref[...], preferred_element_type=F32)

            @pl.when(f == 0)
            def _():
                o_ref[0:rows, :] = y
                if rows < tm:
                    o_ref[rows:, :] = jnp.zeros((tm - rows, o_ref.shape[1]), o_ref.dtype)

            @pl.when(f != 0)
            def _():
                o_ref[0:rows, :] += y

    @pl.when(jnp.logical_not(valid) & (f == 0))
    def _():
        o_ref[...] = jnp.zeros_like(o_ref)


def moe_grouped_ffn(x, nw, tile_expert, tile_rows, src_tok, wg, wu, wd, *, tm, tf=1792):
    S, D = x.shape
    E, _, F = wg.shape
    n_tiles = tile_expert.shape[0]
    nf = F // tf

    def w_in(m, f, te, tv, tok):
        return (te[m], 0, jnp.where(tv[m] > 0, f, nf - 1))

    def w_out(m, f, te, tv, tok):
        return (te[m], jnp.where(tv[m] > 0, f, nf - 1), 0)

    grid_spec = pltpu.PrefetchScalarGridSpec(
        num_scalar_prefetch=3,
        grid=(n_tiles, nf),
        in_specs=[
            pl.BlockSpec(memory_space=pl.ANY),
            pl.BlockSpec((1, D), lambda m, f, te, tv, tok: (0, 0)),
            pl.BlockSpec((None, D, tf), w_in),
            pl.BlockSpec((None, D, tf), w_in),
            pl.BlockSpec((None, tf, D), w_out),
        ],
        out_specs=pl.BlockSpec((tm, D), lambda m, f, te, tv, tok: (m, 0)),
        scratch_shapes=[pltpu.VMEM((2, tm, D), F32), pltpu.VMEM((tm, D), BF16), pltpu.SemaphoreType.DMA((2,))],
    )
    return pl.pallas_call(
        functools.partial(_moe_group_kernel, tm=tm, n_f=nf),
        grid_spec=grid_spec,
        out_shape=jax.ShapeDtypeStruct((n_tiles * tm, D), F32),
        compiler_params=_params("arbitrary", "arbitrary"),
        name="moe_grouped_ffn",
    )(tile_expert, tile_rows, src_tok, x, nw.reshape(1, D), wg, wu, wd)


def _moe_combine_kernel(pos_ref, x_ref, rt_ref, ys_hbm, o_ref, buf, sem, *, tc, n_tok):
    i = pl.program_id(0)
    slot = i % 2

    def gather_tile(tile, into):
        base = tile * tc

        def start(j, carry):
            for k in range(TOP_K):
                _row_copy(ys_hbm, pos_ref[k * n_tok + base + j], buf.at[into, k], j, sem.at[into]).start()
            return carry

        lax.fori_loop(0, tc, start, 0, unroll=8)

    @pl.when(i == 0)
    def _():
        gather_tile(0, 0)

    for into in range(2):
        @pl.when((i + 1 < pl.num_programs(0)) & (slot == 1 - into))
        def _(into=into):
            base = (i + 1) * tc
            for j in range(tc):
                for k in range(TOP_K):
                    _row_copy(ys_hbm, pos_ref[k * n_tok + base + j], buf.at[into, k], j, sem.at[into]).start()

    for k in range(TOP_K):
        pltpu.make_async_copy(ys_hbm.at[pl.ds(0, tc)], buf.at[slot, k], sem.at[slot]).wait()
    rt = rt_ref[...]
    o_ref[...] = (x_ref[...] + rt[:, ROUTE_LANE_G1:ROUTE_LANE_G1 + 1] * buf[slot, 0]
                  + rt[:, ROUTE_LANE_G2:ROUTE_LANE_G2 + 1] * buf[slot, 1])


def moe_combine(x, route, ys, pos, *, tc=512):
    S, D = x.shape
    grid_spec = pltpu.PrefetchScalarGridSpec(
        num_scalar_prefetch=1,
        grid=(S // tc,),
        in_specs=[
            pl.BlockSpec((tc, D), lambda i, pos: (i, 0)),
            pl.BlockSpec((tc, LANES), lambda i, pos: (i, 0)),
            pl.BlockSpec(memory_space=pl.ANY),
        ],
        out_specs=pl.BlockSpec((tc, D), lambda i, pos: (i, 0)),
        scratch_shapes=[pltpu.VMEM((2, TOP_K, tc, D), F32), pltpu.SemaphoreType.DMA((2,))],
    )
    return pl.pallas_call(
        functools.partial(_moe_combine_kernel, tc=tc, n_tok=S),
        grid_spec=grid_spec,
        out_shape=jax.ShapeDtypeStruct((S, D), F32),
        compiler_params=_params("arbitrary"),
        name="moe_combine",
    )(pos, x, route, ys)


def moe_top2(x, nw, route, wg, wu, wd, *, tm=512):
    S, D = x.shape
    E = wg.shape[0]
    experts = jnp.concatenate([route[:, ROUTE_LANE_I1], route[:, ROUTE_LANE_I2]]).astype(jnp.int32)
    onehot = (experts[:, None] == jnp.arange(E, dtype=jnp.int32)).astype(jnp.int32)
    csum = jnp.cumsum(onehot, axis=0)
    rank = jnp.sum(onehot * csum, axis=1) - 1
    padded = (csum[-1] + tm - 1) // tm * tm
    ends = jnp.cumsum(padded)
    pos = (jnp.sum(onehot * (ends - padded), axis=1) + rank).astype(jnp.int32)
    n_tiles = TOP_K * S // tm + E + 1
    tokens = jnp.tile(jnp.arange(S, dtype=jnp.int32), TOP_K)
    src_tok = jnp.zeros((n_tiles * tm,), jnp.int32).at[pos].set(tokens)
    tile_start = jnp.arange(n_tiles, dtype=jnp.int32) * tm
    tile_expert = jnp.minimum(jnp.sum(tile_start[:, None] >= ends[None, :], axis=1), E - 1).astype(jnp.int32)
    seg_end = (ends - padded + csum[-1])[tile_expert]
    tile_rows = jnp.clip(seg_end - tile_start, 0, tm).astype(jnp.int32)
    ys = moe_grouped_ffn(x, nw, tile_expert, tile_rows, src_tok, wg, wu, wd, tm=tm)
    return moe_combine(x, route, ys, pos)


ATTN_RADIUS = 64
ATTN_BQ = 128
ATTN_TILE = ATTN_BQ * max(d for _, d in DILATION_PATTERNS)
ATTN_HALO = ATTN_RADIUS * max(d for _, d in DILATION_PATTERNS)
ATTN_BLOCKS_IN_FLIGHT = 8
assert all(w // (2 * d) == ATTN_RADIUS for w, d in DILATION_PATTERNS)


def _attn_kernel(q_ref, kp_ref, kc_ref, kn_ref, vp_ref, vc_ref, vn_ref, o_ref, kbuf, vbuf, m_s, l_s, a_s, *, n_tok):
    tile, bq, halo, rad = ATTN_TILE, ATTN_BQ, ATTN_HALO, ATTN_RADIUS
    kbuf[0:halo, :] = kp_ref[...]
    kbuf[halo:halo + tile, :] = kc_ref[...]
    kbuf[halo + tile:, :] = kn_ref[...]
    vbuf[0:halo, :] = vp_ref[...]
    vbuf[halo:halo + tile, :] = vc_ref[...]
    vbuf[halo + tile:, :] = vn_ref[...]

    t0 = pl.program_id(1) * tile
    qi = lax.broadcasted_iota(jnp.int32, (bq, bq + 2 * rad), 0)
    kj = lax.broadcasted_iota(jnp.int32, (bq, bq + 2 * rad), 1)
    band = (kj >= qi) & (kj <= qi + 2 * rad)
    lo_half = lax.broadcasted_iota(jnp.int32, (bq, 2 * HEAD_DIM), 1) < HEAD_DIM

    for n_branch, (_, d) in enumerate(sorted(DILATION_PATTERNS, key=lambda wd: -wd[1])):
        span = bq * d
        reps = max(1, ATTN_BLOCKS_IN_FLIGHT // d)

        def block(blk, carry, d=d, span=span, reps=reps, first=n_branch == 0):
            stride = None if d == 1 else d
            problems = [(rep, r) for rep in range(reps) for r in range(d)]

            def scores(rep, r):
                base = pl.multiple_of(blk * (reps * span), span) + rep * span
                q_rows = pl.ds(base + r, bq, stride=stride)
                k_rows = pl.ds(halo + base + (r - rad * d), bq + 2 * rad, stride=stride)
                q = q_ref[q_rows, :]
                kw = kbuf[k_rows, :].astype(BF16)
                tok = t0 + base + (r - rad * d) + d * kj
                mask = band & (tok >= 0) & (tok < n_tok)
                sc = [jnp.where(mask, _bdot_nt(jnp.where(own, q, 0.0), kw), NEG_INF)
                      for own in (lo_half, jnp.logical_not(lo_half))]
                return q_rows, k_rows, sc

            pending = scores(*problems[0])
            for i in range(len(problems)):
                q_rows, k_rows, sc_pair = pending
                if i + 1 < len(problems):
                    pending = scores(*problems[i + 1])
                vw = vbuf[k_rows, :].astype(BF16)
                halves = []
                for sc in sc_pair:
                    m_h = jnp.max(sc, axis=-1, keepdims=True)
                    p = jnp.exp(sc - m_h)
                    halves.append((m_h, jnp.sum(p, axis=-1, keepdims=True),
                                   jnp.dot(p.astype(BF16), vw, preferred_element_type=F32)))
                m_b, l_b, a_b = (jnp.where(lo_half, x0, x1) for x0, x1 in zip(*halves))
                if first:
                    m_n, l_n, a_n = m_b, l_b, a_b
                else:
                    m_o = m_s[q_rows, :]
                    m_n = jnp.maximum(m_o, m_b)
                    w_o = jnp.exp(m_o - m_n)
                    w_b = jnp.exp(m_b - m_n)
                    l_n = l_s[q_rows, :] * w_o + l_b * w_b
                    a_n = a_s[q_rows, :] * w_o + a_b * w_b
                m_s[q_rows, :] = m_n
                l_s[q_rows, :] = l_n
                a_s[q_rows, :] = a_n
            return carry

        lax.fori_loop(0, tile // (reps * span), block, 0)

    o_ref[...] = a_s[...] / l_s[...]


def dilated_attention(q, k, v):
    S, W = q.shape
    pair = 2 * HEAD_DIM
    per = ATTN_TILE // ATTN_HALO
    cur = pl.BlockSpec((ATTN_TILE, pair), lambda p, i: (i, p))
    prv = pl.BlockSpec((ATTN_HALO, pair), lambda p, i: (jnp.maximum(i * per - 1, 0), p))
    nxt = pl.BlockSpec((ATTN_HALO, pair), lambda p, i: (jnp.minimum((i + 1) * per, S // ATTN_HALO - 1), p))
    ext = ATTN_TILE + 2 * ATTN_HALO
    return pl.pallas_call(
        functools.partial(_attn_kernel, n_tok=S),
        grid=(W // pair, S // ATTN_TILE),
        in_specs=[cur, prv, cur, nxt, prv, cur, nxt],
        out_specs=cur,
        out_shape=jax.ShapeDtypeStruct((S, W), F32),
        scratch_shapes=[pltpu.VMEM((ext, pair), F32), pltpu.VMEM((ext, pair), F32)]
        + [pltpu.VMEM((ATTN_TILE, pair), F32)] * 3,
        compiler_params=_params("parallel", "arbitrary"),
        name="dilated_attention",
    )(q, k, k, k, v, v, v)


def _tri_masks(z):
    row = lax.broadcasted_iota(jnp.int32, (CHUNK, CHUNK), 0)
    col = lax.broadcasted_iota(jnp.int32, (CHUNK, CHUNK), 1)
    if z == 0:
        return col <= row, col < row
    return col >= row, col > row


RWKV_SUB_CHUNKS = 2
DN_SUB_CHUNKS = 4


def _sub_rows(z, j, sub):
    i = j if z == 0 else sub - 1 - j
    return slice(i * CHUNK, (i + 1) * CHUNK)


def _neumann_solve(ns, xs):
    steps = CHUNK.bit_length() - 1
    for i in range(steps):
        xs = [x + _bdot(n, x) for n, x in zip(ns, xs)]
        if i + 1 < steps:
            ns = [_bdot(n, n) for n in ns]
    return xs


def _rwkv_chunk_kernel(*refs, n_cast):
    sub = RWKV_SUB_CHUNKS
    ins = (refs[0:6], refs[6:12])
    cast_in = refs[12:12 + n_cast]
    y_refs = refs[12 + n_cast:14 + n_cast]
    cast_out = refs[14 + n_cast:14 + 2 * n_cast]
    state = refs[14 + 2 * n_cast]
    for src, dst in zip(cast_in, cast_out):
        dst[...] = src[...].astype(BF16)

    @pl.when(pl.program_id(0) == 0)
    def _():
        state[...] = jnp.zeros_like(state)

    chains = [(j, z, h) for j in range(sub) for z in range(2) for h in range(N_HEADS_B)]
    masks = [_tri_masks(z) for z in range(2)]
    prep = {}
    for j in range(sub):
        for z in range(2):
            rows = _sub_rows(z, j, sub)
            r_ref, v_ref, a_ref, lw_ref, k_ref, b_ref = ins[z]
            tri = jnp.where(masks[z][0], 1.0, 0.0).astype(BF16)
            lw = lw_ref[rows, :]
            hi, mid, lo = _split3(lw)
            cum = (jnp.dot(tri, hi, preferred_element_type=F32) + jnp.dot(tri, mid, preferred_element_type=F32)
                   + jnp.dot(tri, lo, preferred_element_type=F32))
            tot = jnp.sum(lw, axis=0, keepdims=True)
            e_neg = jnp.exp(-cum)
            e_end = jnp.exp(tot - cum)
            k = k_ref[rows, :]
            b = b_ref[rows, :]
            prep[j, z] = dict(rt=r_ref[rows, :] * jnp.exp(cum), at=a_ref[rows, :] * jnp.exp(cum - lw), kt=k * e_neg,
                              bt=b * e_neg, kh=k * e_end, bh=b * e_end, dw=jnp.exp(tot), v=v_ref[rows, :])

    def part(name, c):
        j, z, h = c
        return prep[j, z][name][:, h * HEAD_DIM:(h + 1) * HEAD_DIM]

    xs = [_bdot_nt(jnp.concatenate([part("at", c), part("rt", c)], axis=0),
                   jnp.concatenate([part("bt", c), part("kt", c)], axis=0)) for c in chains]
    a_ab = [jnp.where(masks[c[1]][1], x[:CHUNK, :CHUNK], 0.0) for c, x in zip(chains, xs)]
    a_rb = [jnp.where(masks[c[1]][0], x[CHUNK:, :CHUNK], 0.0) for c, x in zip(chains, xs)]
    cys = [_bdot(jnp.concatenate([jnp.where(masks[c[1]][1], x[:CHUNK, CHUNK:], 0.0),
                                  jnp.where(masks[c[1]][0], x[CHUNK:, CHUNK:], 0.0)], axis=0), part("v", c))
           for c, x in zip(chains, xs)]
    pqs = _neumann_solve(a_ab, [jnp.concatenate([part("at", c), cy[:CHUNK]], axis=1) for c, cy in zip(chains, cys)])
    kvs = [_bdot_tn(part("v", c), part("kh", c)) for c in chains]
    per = 2 * N_HEADS_B
    s_cur = [state[z, h] for _, z, h in chains[:per]]
    for j in range(sub):
        sel = slice(j * per, (j + 1) * per)
        cs = chains[sel]
        us = [_bdot_nt(pq[:, :HEAD_DIM], s) + pq[:, HEAD_DIM:] for pq, s in zip(pqs[sel], s_cur)]
        ys = [_bdot_nt(part("rt", c), s) for c, s in zip(cs, s_cur)]
        ys = [y + _bdot(arb, u) + cy[CHUNK:] for y, arb, u, cy in zip(ys, a_rb[sel], us, cys[sel])]
        s_cur = [s * part("dw", c) + _bdot_tn(u, part("bh", c)) + kv for c, s, u, kv in zip(cs, s_cur, us, kvs[sel])]
        for (_, z, h), y in zip(cs, ys):
            y_refs[z][_sub_rows(z, j, sub), h * HEAD_DIM:(h + 1) * HEAD_DIM] = y
    for (_, z, h), s in zip(chains[:per], s_cur):
        state[z, h] = s


def rwkv7_scan(r, v, a, lw, k, b, cast_through=()):
    S, C = r.shape
    step_rows = RWKV_SUB_CHUNKS * CHUNK
    n = S // step_rows
    fwd = pl.BlockSpec((step_rows, C), lambda c: (c, 0))
    bwd = pl.BlockSpec((step_rows, C), lambda c: (n - 1 - c, 0))
    cast_specs = [pl.BlockSpec((w.shape[0] // n, w.shape[1]), lambda c: (c, 0)) for w in cast_through]
    y0, y1, *cast = pl.pallas_call(
        functools.partial(_rwkv_chunk_kernel, n_cast=len(cast_through)),
        grid=(n,),
        in_specs=[fwd] * 6 + [bwd] * 6 + cast_specs,
        out_specs=[fwd, bwd] + cast_specs,
        out_shape=[jax.ShapeDtypeStruct((S, C), F32)] * 2 + [jax.ShapeDtypeStruct(w.shape, BF16) for w in cast_through],
        scratch_shapes=[pltpu.VMEM((2, N_HEADS_B, HEAD_DIM, HEAD_DIM), F32)],
        compiler_params=_params("arbitrary"),
        name="rwkv7_scan",
    )(r, v, a, lw[0], k[0], b[0], r, v, a, lw[1], k[1], b[1], *cast_through)
    return y0, y1, cast


def _dn_chunk_kernel(*refs):
    sub = DN_SUB_CHUNKS
    ins = (refs[0:6], refs[6:12])
    o_refs = refs[12:14]
    state = refs[14]

    @pl.when(pl.program_id(0) == 0)
    def _():
        state[...] = jnp.zeros_like(state)

    chains = [(j, z, h) for j in range(sub) for z in range(2) for h in range(N_HEADS_D)]
    masks = [_tri_masks(z) for z in range(2)]
    nt = (((1,), (1,)), ((), ()))
    gcs, decays, betas, g_lasts = [], [], [], []
    for j in range(sub):
        for z in range(2):
            rows = _sub_rows(z, j, sub)
            _, _, _, bcol_ref, gcol_ref, grow_ref = ins[z]
            incl = masks[z][0]
            tri = jnp.where(incl, 1.0, 0.0).astype(BF16)
            c_hi, c_mid, c_lo = _split3(gcol_ref[rows, :])
            gc_cols = (jnp.dot(tri, c_hi, preferred_element_type=F32) + jnp.dot(tri, c_mid, preferred_element_type=F32)
                       + jnp.dot(tri, c_lo, preferred_element_type=F32))
            r_hi, r_mid, r_lo = _split3(grow_ref[rows.start // CHUNK])
            gc_rows = (lax.dot_general(r_hi, tri, nt, preferred_element_type=F32)
                       + lax.dot_general(r_mid, tri, nt, preferred_element_type=F32)
                       + lax.dot_general(r_lo, tri, nt, preferred_element_type=F32))
            last = CHUNK - 1 if z == 0 else 0
            bcol = bcol_ref[rows, :]
            for h in range(N_HEADS_D):
                idx = z * N_HEADS_D + h
                gc = gc_cols[:, idx:idx + 1]
                diff = gc - gc_rows[idx:idx + 1, :]
                gcs.append(gc)
                decays.append(jnp.where(incl, jnp.exp(jnp.where(incl, diff, 0.0)), 0.0))
                betas.append(bcol[:, idx:idx + 1])
                g_lasts.append(gc[last:last + 1, :])

    def part(i, c):
        j, z, h = c
        return ins[z][i][_sub_rows(z, j, sub), h * HEAD_DIM_D:(h + 1) * HEAD_DIM_D]

    qs = [part(0, c) for c in chains]
    ks = [part(1, c) for c in chains]
    vs = [part(2, c) for c in chains]
    kbs = [k * beta for k, beta in zip(ks, betas)]
    e_gcs = [jnp.exp(gc) for gc in gcs]
    kqs = [_bdot_nt(jnp.concatenate([kb, q], axis=0), k) for kb, q, k in zip(kbs, qs, ks)]
    n_mats = [jnp.where(masks[c[1]][1], -(kq[:CHUNK] * dc), 0.0) for c, kq, dc in zip(chains, kqs, decays)]
    attns = [kq[CHUNK:] * dc for kq, dc in zip(kqs, decays)]
    uks = _neumann_solve(n_mats, [jnp.concatenate([v * beta, kb * e], axis=1)
                                  for v, beta, kb, e in zip(vs, betas, kbs, e_gcs)])
    per = 2 * N_HEADS_D
    s_cur = [state[z, h] for _, z, h in chains[:per]]
    for j in range(sub):
        sel = slice(j * per, (j + 1) * per)
        us = [uk[:, :HEAD_DIM_D] - _bdot(uk[:, HEAD_DIM_D:], s) for uk, s in zip(uks[sel], s_cur)]
        os_ = [_bdot(q * e, s) for q, e, s in zip(qs[sel], e_gcs[sel], s_cur)]
        os_ = [o + _bdot(attn, u) for o, attn, u in zip(os_, attns[sel], us)]
        s_cur = [s * jnp.exp(gl) + _bdot_tn(k * jnp.exp(gl - gc), u)
                 for s, gl, k, gc, u in zip(s_cur, g_lasts[sel], ks[sel], gcs[sel], us)]
        for (_, z, h), o in zip(chains[sel], os_):
            o_refs[z][_sub_rows(z, j, sub), h * HEAD_DIM_D:(h + 1) * HEAD_DIM_D] = o
    for (_, z, h), s in zip(chains[:per], s_cur):
        state[z, h] = s


def deltanet_scan(q, k, v, beta, g):
    S, C = q.shape
    step_rows = DN_SUB_CHUNKS * CHUNK
    n = S // step_rows
    nz = 2 * N_HEADS_D
    g_rows = g.reshape(S // CHUNK, CHUNK, nz).transpose(0, 2, 1)

    def specs(idx):
        wide = pl.BlockSpec((step_rows, C), lambda c: (idx(c), 0))
        col = pl.BlockSpec((step_rows, nz), lambda c: (idx(c), 0))
        row = pl.BlockSpec((DN_SUB_CHUNKS, nz, CHUNK), lambda c: (idx(c), 0, 0))
        return [wide, wide, wide, col, col, row], wide

    in_f, out_f = specs(lambda c: c)
    in_b, out_b = specs(lambda c: n - 1 - c)
    o0, o1 = pl.pallas_call(
        _dn_chunk_kernel,
        grid=(n,),
        in_specs=in_f + in_b,
        out_specs=[out_f, out_b],
        out_shape=[jax.ShapeDtypeStruct((S, C), F32)] * 2,
        scratch_shapes=[pltpu.VMEM((2, N_HEADS_D, HEAD_DIM_D, HEAD_DIM_D), F32)],
        compiler_params=_params("arbitrary"),
        name="deltanet_scan",
    )(q, k, v, beta, g, g_rows, q, k, v, beta, g, g_rows)
    return o0, o1


def _rope_tables(positions):
    half = ROPE_DIM // 2
    dim = jnp.arange(2 * HEAD_DIM) % HEAD_DIM
    freq = jnp.where(dim < ROPE_DIM, jnp.power(ROPE_THETA, -(dim % half).astype(F32) / half), 0.0)
    sign = jnp.where(dim < half, -1.0, 1.0)
    ang = positions[:, None].astype(F32) * freq
    return jnp.cos(ang), sign * jnp.sin(ang)


def _ones_block_diag(width, group):
    idx = jnp.arange(width) // group
    return (idx[:, None] == idx[None, :]).astype(BF16)


def _block_diag(blocks):
    rows = sum(b.shape[0] for b in blocks)
    cols = sum(b.shape[1] for b in blocks)
    out = jnp.zeros((rows, cols), blocks[0].dtype)
    r = c = 0
    for b in blocks:
        out = lax.dynamic_update_slice(out, b, (r, c))
        r += b.shape[0]
        c += b.shape[1]
    return out


def _even_layer(x, positions, mix_norm, w_in, q_norm, k_norm, shift_mu, lora_mu, w0, w1, w2, a0, a1, a2, g1, g2,
                k_k, k_a, r_k, ln_w, ln_b, w_out, ffn_norm, ffn_gate, ffn_up, ffn_down, cast_through=()):
    row = lambda t: t.reshape(1, -1)
    cos_t, sin_t = _rope_tables(positions)
    ones_bd = _ones_block_diag(WIDTH_B, HEAD_DIM)
    lora_in = jnp.concatenate([w1[0], w1[1], a1[0], a1[1], g1], axis=1).astype(BF16)
    lora_out = _block_diag([w2[0], w2[1], a2[0], a2[1], g2]).astype(BF16)
    (q, k, v, r, vb, a_vec, lw0, lw1, k0, k1, b0, b1, gate, bonus) = even_prep(
        x, mix_norm, w_in.astype(BF16), row(jnp.tile(q_norm, N_HEADS_A)), row(jnp.tile(k_norm, N_HEADS_A)), cos_t, sin_t,
        shift_mu, lora_mu, lora_in, lora_out, w0, a0, row(k_k), row(k_a), row(r_k), ones_bd)
    y_a = dilated_attention(q, k, v)
    y0, y1, cast = rwkv7_scan(r, vb, a_vec, (lw0, lw1), (k0, k1), (b0, b1),
                              [w.reshape(-1, w.shape[-1]) for w in cast_through])
    cast = [c.reshape(w.shape) for c, w in zip(cast, cast_through)]
    x = even_post_ffn(x, y_a, y0, y1, gate, bonus, row(ln_w), row(ln_b), ones_bd, w_out.astype(BF16), ffn_norm,
                      ffn_gate.astype(BF16), ffn_up.astype(BF16), ffn_down.astype(BF16))
    return x, cast


def _odd_layer(x, mix_norm, w_in, conv_c, conv_dn, A_log, dt_bias, dn_norm, w_out, ffn_norm, router, moe_gate, moe_up, moe_down):
    n_in = w_in.shape[1]
    n_pad = -(-n_in // LANES) * LANES
    w_in_p = jnp.pad(w_in, ((0, 0), (0, n_pad - n_in))).astype(BF16)
    nz = 2 * N_HEADS_D
    neg_a = jnp.zeros((1, LANES), F32).at[0, nz:2 * nz].set(-jnp.exp(A_log.reshape(-1)))
    dt_b = jnp.zeros((1, LANES), F32).at[0, nz:2 * nz].set(dt_bias.reshape(-1))
    ones_bd = _ones_block_diag(WIDTH_D, HEAD_DIM_D)
    y_c, q, k, v, zs, bg = odd_prep(x, mix_norm, w_in_p, conv_c, conv_dn, neg_a, dt_b, ones_bd)
    o0, o1 = deltanet_scan(q, k, v, bg[:, :nz], bg[:, nz:2 * nz])
    wr_pad = jnp.pad(router, ((0, 0), (0, LANES - N_EXPERTS)))
    x, route = odd_post(x, y_c, o0, o1, zs, jnp.tile(dn_norm, N_HEADS_D).reshape(1, -1), ones_bd, w_out.astype(BF16),
                        ffn_norm, wr_pad)
    return moe_top2(x, ffn_norm, route, moe_gate.astype(BF16), moe_up.astype(BF16), moe_down.astype(BF16))


def kernel(x, positions, ev_mix_norm, ev_w_in, ev_q_norm, ev_k_norm, ev_shift_mu, ev_lora_mu, ev_w0, ev_w1, ev_w2, ev_a0, ev_a1, ev_a2, ev_g1, ev_g2, ev_k_k, ev_k_a, ev_r_k, ev_ln_w, ev_ln_b, ev_w_out, ev_ffn_norm, ev_ffn_gate, ev_ffn_up, ev_ffn_down, od_mix_norm, od_w_in, od_conv_c, od_conv_dn, od_A_log, od_dt_bias, od_dn_norm, od_w_out, od_ffn_norm, od_router, od_moe_gate, od_moe_up, od_moe_down):
    B, S, D = x.shape
    assert B == 1
    xs = x.reshape(S, D)
    pos = positions.reshape(S)
    n_layers = ev_mix_norm.shape[0] + od_mix_norm.shape[0]
    for layer in range(n_layers):
        i = layer // 2
        if layer % 2 == 0:
            nxt = (od_moe_gate[i], od_moe_up[i], od_moe_down[i]) if layer + 1 < n_layers else ()
            xs, moe_w = _even_layer(xs, pos, ev_mix_norm[i], ev_w_in[i], ev_q_norm[i], ev_k_norm[i], ev_shift_mu[i],
                                    ev_lora_mu[i], ev_w0[i], ev_w1[i], ev_w2[i], ev_a0[i], ev_a1[i], ev_a2[i], ev_g1[i],
                                    ev_g2[i], ev_k_k[i], ev_k_a[i], ev_r_k[i], ev_ln_w[i], ev_ln_b[i], ev_w_out[i],
                                    ev_ffn_norm[i], ev_ffn_gate[i], ev_ffn_up[i], ev_ffn_down[i], cast_through=nxt)
        else:
            xs = _odd_layer(xs, od_mix_norm[i], od_w_in[i], od_conv_c[i], od_conv_dn[i], od_A_log[i], od_dt_bias[i],
                            od_dn_norm[i], od_w_out[i], od_ffn_norm[i], od_router[i], *moe_w)
    return xs.reshape(B, S, D)
```

```python
import functools

import jax
import jax.numpy as jnp
from jax import lax
from jax.experimental import pallas as pl
from jax.experimental.pallas import tpu as pltpu

F32 = jnp.float32
BF16 = jnp.bfloat16

HEAD_DIM = 64
N_HEADS_A = 8
WIDTH_A = N_HEADS_A * HEAD_DIM
DILATION_PATTERNS = ((128, 1), (512, 4), (2048, 16))
ROPE_DIM = HEAD_DIM // 4
ROPE_THETA = 500000.0
N_HEADS_B = 8
WIDTH_B = N_HEADS_B * HEAD_DIM
DECAY_LORA, ICLR_LORA, GATE_LORA = 64, 64, 128
RWKV_LN_EPS = 64e-5
WIDTH_C = 512
N_HEADS_D = 4
HEAD_DIM_D = 128
WIDTH_D = N_HEADS_D * HEAD_DIM_D
CHUNK = 64
N_EXPERTS = 8
TOP_K = 2
ROUTE_LANE_I1, ROUTE_LANE_I2, ROUTE_LANE_G1, ROUTE_LANE_G2 = 8, 9, 10, 11
MOE_ROW_SPLITS = 4
NORM_EPS = 1e-6
NEG_INF = -1e30

V7X_VMEM_LIMIT_BYTES = 56 * 1024 * 1024
LANES = 128


def _params(*sem):
    return pltpu.CompilerParams(dimension_semantics=sem, vmem_limit_bytes=V7X_VMEM_LIMIT_BYTES)


def _bdot(a, b):
    return jnp.dot(a.astype(BF16), b.astype(BF16), preferred_element_type=F32)


def _bdot_nt(a, b):
    return lax.dot_general(a.astype(BF16), b.astype(BF16), (((1,), (1,)), ((), ())), preferred_element_type=F32)


def _bdot_tn(a, b):
    return lax.dot_general(a.astype(BF16), b.astype(BF16), (((0,), (0,)), ((), ())), preferred_element_type=F32)


def _split3(x):
    hi = x.astype(BF16)
    r1 = x - hi.astype(F32)
    mid = r1.astype(BF16)
    lo = (r1 - mid.astype(F32)).astype(BF16)
    return hi, mid, lo


def _rms(x, w):
    return x * lax.rsqrt(jnp.mean(x * x, axis=-1, keepdims=True) + NORM_EPS) * w


HALO = 8


def _group_sum(x, ones_bd):
    hi = x.astype(BF16)
    lo = (x - hi.astype(F32)).astype(BF16)
    return jnp.dot(hi, ones_bd, preferred_element_type=F32) + jnp.dot(lo, ones_bd, preferred_element_type=F32)


def _silu(x):
    return x * jax.nn.sigmoid(x)


def _halo_specs(tm, n_rows, width):
    per = tm // HALO
    cur = pl.BlockSpec((tm, width), lambda i: (i, 0))
    prv = pl.BlockSpec((HALO, width), lambda i: (jnp.maximum(i * per - 1, 0), 0))
    nxt = pl.BlockSpec((HALO, width), lambda i: (jnp.minimum((i + 1) * per, n_rows // HALO - 1), 0))
    return cur, prv, nxt


def _make_shifts(tm, n_rows):
    row = pl.program_id(0) * tm + lax.broadcasted_iota(jnp.int32, (tm, 1), 0)
    first, last = row == 0, row == n_rows - 1
    n_ext = tm + 2 * HALO

    def shifts(t):
        prev = jnp.where(first, 0.0, pltpu.roll(t, 1, 0)[HALO:HALO + tm])
        nxt = jnp.where(last, 0.0, pltpu.roll(t, n_ext - 1, 0)[HALO:HALO + tm])
        return prev, t[HALO:HALO + tm], nxt

    return shifts


def _even_prep_kernel(xc_ref, xp_ref, xn_ref, nw_ref, win_ref, qn_ref, kn_ref, cos_ref, sin_ref, smu_ref, lmu_ref,
                      lin_ref, lout_ref, w0_ref, a0_ref, kk_ref, ka_ref, rk_ref, ones_ref,
                      q_out, k_out, v_out, r_out, vb_out, a_out, lw0_out, lw1_out, k0_out, k1_out, b0_out, b1_out,
                      gate_out, bonus_out, *, tm, n_rows):
    shifts = _make_shifts(tm, n_rows)
    he = _rms(jnp.concatenate([xp_ref[...], xc_ref[...], xn_ref[...]], axis=0), nw_ref[...])
    proj = jnp.dot(he.astype(BF16), win_ref[...], preferred_element_type=F32)
    ones = ones_ref[...]
    lane = lax.broadcasted_iota(jnp.int32, (tm, WIDTH_A), 1) % HEAD_DIM

    n_pairs = WIDTH_A // cos_ref.shape[1]
    cos_t = jnp.concatenate([cos_ref[...]] * n_pairs, axis=1)
    sin_t = jnp.concatenate([sin_ref[...]] * n_pairs, axis=1)

    def head_rms_rope(t, w):
        t = t * lax.rsqrt(_group_sum(t * t, ones) * (1.0 / HEAD_DIM) + NORM_EPS) * w
        half = ROPE_DIM // 2
        swapped = jnp.where(lane < half, pltpu.roll(t, WIDTH_A - half, 1), pltpu.roll(t, half, 1))
        return t * cos_t + swapped * sin_t

    cur = proj[HALO:HALO + tm]
    q = head_rms_rope(cur[:, :WIDTH_A], qn_ref[...]) * HEAD_DIM ** -0.5
    k = head_rms_rope(cur[:, WIDTH_A:2 * WIDTH_A], kn_ref[...])
    v = cur[:, 2 * WIDTH_A:3 * WIDTH_A]
    q_out[...] = q
    k_out[...] = k
    v_out[...] = v

    p_prev, p_cur, p_next = shifts(proj[:, 3 * WIDTH_A:])
    smu = smu_ref[...]
    rkv = p_cur + smu[0:1] * (p_prev - p_cur) + smu[1:2] * (p_next - p_cur)
    r, kin, vb = rkv[:, :WIDTH_B], rkv[:, WIDTH_B:2 * WIDTH_B], rkv[:, 2 * WIDTH_B:]
    h_prev, h_cur, h_next = shifts(he)
    lmu = lmu_ref[...]
    hx = h_cur + lmu[0:1] * (h_prev - h_cur) + lmu[1:2] * (h_next - h_cur)
    l1 = jnp.dot(hx.astype(BF16), lin_ref[...], preferred_element_type=F32)
    n_w, n_a = 2 * DECAY_LORA, 2 * ICLR_LORA
    l1 = jnp.concatenate([jnp.tanh(l1[:, :n_w]), l1[:, n_w:n_w + n_a], jax.nn.sigmoid(l1[:, n_w + n_a:])], axis=1)
    l2 = jnp.dot(l1.astype(BF16), lout_ref[...], preferred_element_type=F32)
    w0, a0 = w0_ref[...], a0_ref[...]
    kk = kin * kk_ref[...]
    kk = kk * lax.rsqrt(_group_sum(kk * kk, ones) + 1e-6)
    kdirs = []
    for z, (lw_out, k_out_z, b_out_z) in enumerate(((lw0_out, k0_out, b0_out), (lw1_out, k1_out, b1_out))):
        w_pre = l2[:, z * WIDTH_B:(z + 1) * WIDTH_B] + w0[z:z + 1]
        lw_out[...] = -jnp.exp(-0.5) * jax.nn.sigmoid(w_pre)
        iclr = jax.nn.sigmoid(l2[:, (2 + z) * WIDTH_B:(3 + z) * WIDTH_B] + a0[z:z + 1])
        kdir = kin * (1.0 + (iclr - 1.0) * ka_ref[...])
        k_out_z[...] = kdir
        b_out_z[...] = kk * iclr
        kdirs.append(kdir)
    r_out[...] = r
    vb_out[...] = vb
    a_out[...] = -kk
    gate_out[...] = l2[:, 4 * WIDTH_B:]
    bonus_out[...] = _group_sum(r * (kdirs[0] + kdirs[1]) * rk_ref[...], ones) * vb


def even_prep(x, mix_norm, w_in, q_norm, k_norm, cos_t, sin_t, shift_mu, lora_mu, lora_in, lora_out, w0, a0, k_k, k_a,
              r_k, ones_bd, *, tm=256):
    S, D = x.shape
    cur, prv, nxt = _halo_specs(tm, S, D)
    full = lambda a: pl.BlockSpec(a.shape, lambda i: (0,) * a.ndim)
    rows = pl.BlockSpec((tm, WIDTH_B), lambda i: (i, 0))
    consts = [mix_norm.reshape(1, D), w_in, q_norm, k_norm]
    consts2 = [shift_mu, lora_mu, lora_in, lora_out, w0, a0, k_k, k_a, r_k, ones_bd]
    return pl.pallas_call(
        functools.partial(_even_prep_kernel, tm=tm, n_rows=S),
        grid=(S // tm,),
        in_specs=[cur, prv, nxt] + [full(a) for a in consts]
        + [pl.BlockSpec((tm, cos_t.shape[1]), lambda i: (i, 0))] * 2 + [full(a) for a in consts2],
        out_specs=[rows] * 14,
        out_shape=[jax.ShapeDtypeStruct((S, WIDTH_B), F32)] * 14,
        compiler_params=_params("parallel"),
        name="even_prep",
    )(x, x, x, *consts, cos_t, sin_t, *consts2)


def _odd_prep_kernel(xc_ref, xp_ref, xn_ref, nw_ref, win_ref, cc_ref, cdn_ref, nega_ref, dtb_ref, ones_ref,
                     yc_out, q_out, k_out, v_out, zs_out, bg_out, *, tm, n_rows):
    shifts = _make_shifts(tm, n_rows)
    he = _rms(jnp.concatenate([xp_ref[...], xc_ref[...], xn_ref[...]], axis=0), nw_ref[...])
    proj = jnp.dot(he.astype(BF16), win_ref[...], preferred_element_type=F32)
    cur = proj[HALO:HALO + tm]

    def conv3(t, w):
        prev, mid, nxt = shifts(t)
        return w[0:1] * prev + w[1:2] * mid + w[2:3] * nxt

    o_dn = 3 * WIDTH_C
    yc_out[...] = cur[:, :WIDTH_C] * conv3(proj[:, WIDTH_C:2 * WIDTH_C] * proj[:, 2 * WIDTH_C:o_dn], cc_ref[...])
    qkv = _silu(conv3(proj[:, o_dn:o_dn + 3 * WIDTH_D], cdn_ref[...]))
    ones = ones_ref[...]
    l2n = lambda t: t * lax.rsqrt(_group_sum(t * t, ones) + 1e-6)
    q_out[...] = l2n(qkv[:, :WIDTH_D]) * HEAD_DIM_D ** -0.5
    k_out[...] = l2n(qkv[:, WIDTH_D:2 * WIDTH_D])
    v_out[...] = qkv[:, 2 * WIDTH_D:]
    zs_out[...] = _silu(cur[:, o_dn + 3 * WIDTH_D:o_dn + 4 * WIDTH_D])
    tail = cur[:, o_dn + 4 * WIDTH_D:]
    t = tail + dtb_ref[...]
    softplus = jnp.maximum(t, 0.0) + jnp.log(1.0 + jnp.exp(-jnp.abs(t)))
    lane = lax.broadcasted_iota(jnp.int32, tail.shape, 1)
    bg_out[...] = jnp.where(lane < 2 * N_HEADS_D, jax.nn.sigmoid(tail), nega_ref[...] * softplus)


def odd_prep(x, mix_norm, w_in_pad, conv_c, conv_dn, neg_a, dt_b, ones_bd, *, tm=512):
    S, D = x.shape
    cur, prv, nxt = _halo_specs(tm, S, D)
    full = lambda a: pl.BlockSpec(a.shape, lambda i: (0,) * a.ndim)
    rows = pl.BlockSpec((tm, WIDTH_D), lambda i: (i, 0))
    consts = [mix_norm.reshape(1, D), w_in_pad, conv_c, conv_dn, neg_a, dt_b, ones_bd]
    return pl.pallas_call(
        functools.partial(_odd_prep_kernel, tm=tm, n_rows=S),
        grid=(S // tm,),
        in_specs=[cur, prv, nxt] + [full(a) for a in consts],
        out_specs=[rows] * 5 + [pl.BlockSpec((tm, LANES), lambda i: (i, 0))],
        out_shape=[jax.ShapeDtypeStruct((S, WIDTH_D), F32)] * 5 + [jax.ShapeDtypeStruct((S, LANES), F32)],
        compiler_params=_params("parallel"),
        name="odd_prep",
    )(x, x, x, *consts)


def _even_post_ffn_kernel(x_ref, ya_ref, y0_ref, y1_ref, gate_ref, bonus_ref, lnw_ref, lnb_ref, ones_ref, wo_ref,
                          fnw_ref, wg_ref, wu_ref, wd_ref, o_ref, h_scr):
    @pl.when(pl.program_id(1) == 0)
    def _():
        ones = ones_ref[...]
        yf = y0_ref[...] + y1_ref[...]
        dev = yf - _group_sum(yf, ones) * (1.0 / HEAD_DIM)
        var = _group_sum(dev * dev, ones) * (1.0 / HEAD_DIM)
        yn = dev * lax.rsqrt(var + RWKV_LN_EPS) * lnw_ref[...] + lnb_ref[...]
        y_b = (yn + bonus_ref[...]) * gate_ref[...]
        y = jnp.concatenate([ya_ref[...], y_b], axis=1)
        x = x_ref[...] + jnp.dot(y.astype(BF16), wo_ref[...], preferred_element_type=F32)
        h_scr[...] = _rms(x, fnw_ref[...]).astype(BF16)
        o_ref[...] = x

    h = h_scr[...]
    g = jnp.dot(h, wg_ref[...], preferred_element_type=F32)
    u = jnp.dot(h, wu_ref[...], preferred_element_type=F32)
    o_ref[...] += jnp.dot((_silu(g) * u).astype(BF16), wd_ref[...], preferred_element_type=F32)


def even_post_ffn(x, y_a, y0, y1, gate, bonus, ln_w, ln_b, ones_bd, w_out, ffn_norm, wg, wu, wd, *, tm=512, tf=1408):
    S, D = x.shape
    F = wg.shape[1]
    rows = pl.BlockSpec((tm, WIDTH_B), lambda i, f: (i, 0))
    full = lambda a: pl.BlockSpec(a.shape, lambda i, f: (0,) * a.ndim)
    consts = [ln_w, ln_b, ones_bd, w_out, ffn_norm.reshape(1, D)]
    return pl.pallas_call(
        _even_post_ffn_kernel,
        grid=(S // tm, F // tf),
        in_specs=[pl.BlockSpec((tm, D), lambda i, f: (i, 0)),
                  rows, rows, rows, rows, rows] + [full(a) for a in consts] + [
            pl.BlockSpec((D, tf), lambda i, f: (0, f)),
            pl.BlockSpec((D, tf), lambda i, f: (0, f)),
            pl.BlockSpec((tf, D), lambda i, f: (f, 0)),
        ],
        out_specs=pl.BlockSpec((tm, D), lambda i, f: (i, 0)),
        out_shape=jax.ShapeDtypeStruct((S, D), F32),
        scratch_shapes=[pltpu.VMEM((tm, D), BF16)],
        compiler_params=_params("parallel", "arbitrary"),
        name="even_post_ffn",
    )(x, y_a, y0, y1, gate, bonus, *consts, wg, wu, wd)


def _route_record(h, wr):
    h_hi, w_hi = h.astype(BF16), wr.astype(BF16)
    h_lo, w_lo = (h - h_hi.astype(F32)).astype(BF16), (wr - w_hi.astype(F32)).astype(BF16)
    logits = (jnp.dot(h_hi, w_hi, preferred_element_type=F32) + jnp.dot(h_lo, w_hi, preferred_element_type=F32)
              + jnp.dot(h_hi, w_lo, preferred_element_type=F32))
    lane = lax.broadcasted_iota(jnp.int32, logits.shape, 1)
    valid = lane < N_EXPERTS
    lg = jnp.where(valid, logits, NEG_INF)
    e = jnp.exp(lg - jnp.max(lg, axis=-1, keepdims=True))
    p = e / jnp.sum(e, axis=-1, keepdims=True)
    pm = jnp.where(valid, p, -1.0)
    m1 = jnp.max(pm, axis=-1, keepdims=True)
    i1 = jnp.min(jnp.where(pm == m1, lane, LANES), axis=-1, keepdims=True)
    pm2 = jnp.where(lane == i1, -1.0, pm)
    m2 = jnp.max(pm2, axis=-1, keepdims=True)
    i2 = jnp.min(jnp.where(pm2 == m2, lane, LANES), axis=-1, keepdims=True)
    tot = m1 + m2
    g1, g2 = m1 / tot, m2 / tot
    out = jnp.where(lane == i1, g1, 0.0) + jnp.where(lane == i2, g2, 0.0)
    out = jnp.where(lane == ROUTE_LANE_I1, i1.astype(F32), out)
    out = jnp.where(lane == ROUTE_LANE_I2, i2.astype(F32), out)
    out = jnp.where(lane == ROUTE_LANE_G1, g1, out)
    return jnp.where(lane == ROUTE_LANE_G2, g2, out)


def _odd_post_kernel(x_ref, yc_ref, o0_ref, o1_ref, zs_ref, dnw_ref, ones_ref, wo_ref, fnw_ref, wr_ref, x_out, route_out):
    o = o0_ref[...] + o1_ref[...]
    ms = _group_sum(o * o, ones_ref[...]) * (1.0 / HEAD_DIM_D)
    y_d = o * lax.rsqrt(ms + NORM_EPS) * dnw_ref[...] * zs_ref[...]
    y = jnp.concatenate([yc_ref[...], y_d], axis=1)
    x = x_ref[...] + jnp.dot(y.astype(BF16), wo_ref[...], preferred_element_type=F32)
    x_out[...] = x
    route_out[...] = _route_record(_rms(x, fnw_ref[...]), wr_ref[...])


def odd_post(x, y_c, o0, o1, zs, dn_norm, ones_bd, w_out, ffn_norm, wr_pad, *, tm=512):
    S, D = x.shape
    rows = pl.BlockSpec((tm, WIDTH_D), lambda i: (i, 0))
    full = lambda a: pl.BlockSpec(a.shape, lambda i: (0,) * a.ndim)
    consts = [dn_norm, ones_bd, w_out, ffn_norm.reshape(1, D), wr_pad]
    return pl.pallas_call(
        _odd_post_kernel,
        grid=(S // tm,),
        in_specs=[pl.BlockSpec((tm, D), lambda i: (i, 0)), rows, rows, rows, rows] + [full(a) for a in consts],
        out_specs=[pl.BlockSpec((tm, D), lambda i: (i, 0)), pl.BlockSpec((tm, LANES), lambda i: (i, 0))],
        out_shape=[jax.ShapeDtypeStruct((S, D), F32), jax.ShapeDtypeStruct((S, LANES), F32)],
        compiler_params=_params("parallel"),
        name="odd_post",
    )(x, y_c, o0, o1, zs, *consts)


def _row_copy(src_hbm, src_row, dst_vmem, dst_row, sem):
    return pltpu.make_async_copy(src_hbm.at[pl.ds(src_row, 1)], dst_vmem.at[pl.ds(dst_row, 1)], sem)


def _moe_group_kernel(te_ref, tv_ref, tok_ref, x_hbm, nw_ref, wg_ref, wu_ref, wd_ref, o_ref, xbuf, h_scr, sem, *, tm, n_f):
    m = pl.program_id(0)
    f = pl.program_id(1)
    valid = tv_ref[m] > 0
    slot = m % 2
    part = tm // n_f

    @pl.when((f == 0) & (m == 0))
    def _():
        def start(j, carry):
            _row_copy(x_hbm, tok_ref[j], xbuf.at[0], j, sem.at[0]).start()
            return carry

        lax.fori_loop(0, tm, start, 0, unroll=8)

    fed = valid | ((m > 0) & (tv_ref[jnp.maximum(m - 1, 0)] > 0))

    @pl.when(fed & (f == 0))
    def _():
        pltpu.make_async_copy(x_hbm.at[pl.ds(0, tm)], xbuf.at[slot], sem.at[slot]).wait()

    @pl.when(valid & (f == 0))
    def _():
        h_scr[...] = _rms(xbuf[slot], nw_ref[...]).astype(BF16)

    for into in range(2):
        for ff in range(n_f):
            @pl.when(valid & (slot == 1 - into) & (f == ff))
            def _(into=into, ff=ff):
                nbase = (m + 1) * tm + ff * part
                for j in range(part):
                    _row_copy(x_hbm, tok_ref[nbase + j], xbuf.at[into], ff * part + j, sem.at[into]).start()

    quarter = tm // MOE_ROW_SPLITS
    need = (tv_ref[m] + quarter - 1) // quarter
    for n_q in range(1, MOE_ROW_SPLITS + 1):
        rows = n_q * quarter

        @pl.when(need == n_q)
        def _(rows=rows):
            h = h_scr[0:rows, :]
            g = jnp.dot(h, wg_ref[...], preferred_element_type=F32)
            u = jnp.dot(h, wu_ref[...], preferred_element_type=F32)
            y = jnp.dot((_silu(g) * u).astype(BF16), wd_ref[...], preferred_element_type=F32)

            @pl.when(f == 0)
            def _():
                o_ref[0:rows, :] = y
                if rows < tm:
                    o_ref[rows:, :] = jnp.zeros((tm - rows, o_ref.shape[1]), o_ref.dtype)

            @pl.when(f != 0)
            def _():
                o_ref[0:rows, :] += y

    @pl.when(jnp.logical_not(valid) & (f == 0))
    def _():
        o_ref[...] = jnp.zeros_like(o_ref)


def moe_grouped_ffn(x, nw, tile_expert, tile_rows, src_tok, wg, wu, wd, *, tm, tf=1792):
    S, D = x.shape
    E, _, F = wg.shape
    n_tiles = tile_expert.shape[0]
    nf = F // tf

    def w_in(m, f, te, tv, tok):
        return (te[m], 0, jnp.where(tv[m] > 0, f, nf - 1))

    def w_out(m, f, te, tv, tok):
        return (te[m], jnp.where(tv[m] > 0, f, nf - 1), 0)

    grid_spec = pltpu.PrefetchScalarGridSpec(
        num_scalar_prefetch=3,
        grid=(n_tiles, nf),
        in_specs=[
            pl.BlockSpec(memory_space=pl.ANY),
            pl.BlockSpec((1, D), lambda m, f, te, tv, tok: (0, 0)),
            pl.BlockSpec((None, D, tf), w_in),
            pl.BlockSpec((None, D, tf), w_in),
            pl.BlockSpec((None, tf, D), w_out),
        ],
        out_specs=pl.BlockSpec((tm, D), lambda m, f, te, tv, tok: (m, 0)),
        scratch_shapes=[pltpu.VMEM((2, tm, D), F32), pltpu.VMEM((tm, D), BF16), pltpu.SemaphoreType.DMA((2,))],
    )
    return pl.pallas_call(
        functools.partial(_moe_group_kernel, tm=tm, n_f=nf),
        grid_spec=grid_spec,
        out_shape=jax.ShapeDtypeStruct((n_tiles * tm, D), F32),
        compiler_params=_params("arbitrary", "arbitrary"),
        name="moe_grouped_ffn",
    )(tile_expert, tile_rows, src_tok, x, nw.reshape(1, D), wg, wu, wd)


def _moe_combine_kernel(pos_ref, x_ref, rt_ref, ys_hbm, o_ref, buf, sem, *, tc, n_tok):
    i = pl.program_id(0)
    slot = i % 2

    def gather_tile(tile, into):
        base = tile * tc

        def start(j, carry):
            for k in range(TOP_K):
                _row_copy(ys_hbm, pos_ref[k * n_tok + base + j], buf.at[into, k], j, sem.at[into]).start(priority=k)
            return carry

        lax.fori_loop(0, tc, start, 0, unroll=8)

    @pl.when(i == 0)
    def _():
        gather_tile(0, 0)

    for into in range(2):
        @pl.when((i + 1 < pl.num_programs(0)) & (slot == 1 - into))
        def _(into=into):
            base = (i + 1) * tc
            for j in range(tc):
                for k in range(TOP_K):
                    _row_copy(ys_hbm, pos_ref[k * n_tok + base + j], buf.at[into, k], j, sem.at[into]).start(priority=k)

    for k in range(TOP_K):
        pltpu.make_async_copy(ys_hbm.at[pl.ds(0, tc)], buf.at[slot, k], sem.at[slot]).wait()
    rt = rt_ref[...]
    o_ref[...] = (x_ref[...] + rt[:, ROUTE_LANE_G1:ROUTE_LANE_G1 + 1] * buf[slot, 0]
                  + rt[:, ROUTE_LANE_G2:ROUTE_LANE_G2 + 1] * buf[slot, 1])


def moe_combine(x, route, ys, pos, *, tc=512):
    S, D = x.shape
    grid_spec = pltpu.PrefetchScalarGridSpec(
        num_scalar_prefetch=1,
        grid=(S // tc,),
        in_specs=[
            pl.BlockSpec((tc, D), lambda i, pos: (i, 0)),
            pl.BlockSpec((tc, LANES), lambda i, pos: (i, 0)),
            pl.BlockSpec(memory_space=pl.ANY),
        ],
        out_specs=pl.BlockSpec((tc, D), lambda i, pos: (i, 0)),
        scratch_shapes=[pltpu.VMEM((2, TOP_K, tc, D), F32), pltpu.SemaphoreType.DMA((2,))],
    )
    return pl.pallas_call(
        functools.partial(_moe_combine_kernel, tc=tc, n_tok=S),
        grid_spec=grid_spec,
        out_shape=jax.ShapeDtypeStruct((S, D), F32),
        compiler_params=_params("arbitrary"),
        name="moe_combine",
    )(pos, x, route, ys)


def moe_top2(x, nw, route, wg, wu, wd, *, tm=512):
    S, D = x.shape
    E = wg.shape[0]
    experts = jnp.concatenate([route[:, ROUTE_LANE_I1], route[:, ROUTE_LANE_I2]]).astype(jnp.int32)
    onehot = (experts[:, None] == jnp.arange(E, dtype=jnp.int32)).astype(jnp.int32)
    csum = jnp.cumsum(onehot, axis=0)
    rank = jnp.sum(onehot * csum, axis=1) - 1
    padded = (csum[-1] + tm - 1) // tm * tm
    ends = jnp.cumsum(padded)
    pos = (jnp.sum(onehot * (ends - padded), axis=1) + rank).astype(jnp.int32)
    n_tiles = TOP_K * S // tm + E + 1
    tokens = jnp.tile(jnp.arange(S, dtype=jnp.int32), TOP_K)
    src_tok = jnp.zeros((n_tiles * tm,), jnp.int32).at[pos].set(tokens)
    tile_start = jnp.arange(n_tiles, dtype=jnp.int32) * tm
    tile_expert = jnp.minimum(jnp.sum(tile_start[:, None] >= ends[None, :], axis=1), E - 1).astype(jnp.int32)
    seg_end = (ends - padded + csum[-1])[tile_expert]
    tile_rows = jnp.clip(seg_end - tile_start, 0, tm).astype(jnp.int32)
    ys = moe_grouped_ffn(x, nw, tile_expert, tile_rows, src_tok, wg, wu, wd, tm=tm)
    return moe_combine(x, route, ys, pos)


ATTN_RADIUS = 64
ATTN_BQ = 128
ATTN_TILE = ATTN_BQ * max(d for _, d in DILATION_PATTERNS)
ATTN_HALO = ATTN_RADIUS * max(d for _, d in DILATION_PATTERNS)
ATTN_BLOCKS_IN_FLIGHT = 8
assert all(w // (2 * d) == ATTN_RADIUS for w, d in DILATION_PATTERNS)


def _attn_kernel(q_ref, kp_ref, kc_ref, kn_ref, vp_ref, vc_ref, vn_ref, o_ref, kbuf, vbuf, m_s, l_s, a_s, *, n_tok):
    tile, bq, halo, rad = ATTN_TILE, ATTN_BQ, ATTN_HALO, ATTN_RADIUS
    kbuf[0:halo, :] = kp_ref[...]
    kbuf[halo:halo + tile, :] = kc_ref[...]
    kbuf[halo + tile:, :] = kn_ref[...]
    vbuf[0:halo, :] = vp_ref[...]
    vbuf[halo:halo + tile, :] = vc_ref[...]
    vbuf[halo + tile:, :] = vn_ref[...]

    t0 = pl.program_id(1) * tile
    qi = lax.broadcasted_iota(jnp.int32, (bq, bq + 2 * rad), 0)
    kj = lax.broadcasted_iota(jnp.int32, (bq, bq + 2 * rad), 1)
    band = (kj >= qi) & (kj <= qi + 2 * rad)
    lo_half = lax.broadcasted_iota(jnp.int32, (bq, 2 * HEAD_DIM), 1) < HEAD_DIM

    for n_branch, (_, d) in enumerate(sorted(DILATION_PATTERNS, key=lambda wd: -wd[1])):
        span = bq * d
        reps = max(1, ATTN_BLOCKS_IN_FLIGHT // d)

        def block(blk, carry, d=d, span=span, reps=reps, first=n_branch == 0):
            stride = None if d == 1 else d
            problems = [(rep, r) for rep in range(reps) for r in range(d)]

            def scores(rep, r):
                base = pl.multiple_of(blk * (reps * span), span) + rep * span
                q_rows = pl.ds(base + r, bq, stride=stride)
                k_rows = pl.ds(halo + base + (r - rad * d), bq + 2 * rad, stride=stride)
                q = q_ref[q_rows, :]
                kw = kbuf[k_rows, :].astype(BF16)
                tok = t0 + base + (r - rad * d) + d * kj
                mask = band & (tok >= 0) & (tok < n_tok)
                sc = [jnp.where(mask, _bdot_nt(jnp.where(own, q, 0.0), kw), NEG_INF)
                      for own in (lo_half, jnp.logical_not(lo_half))]
                return q_rows, k_rows, sc

            pending = scores(*problems[0])
            for i in range(len(problems)):
                q_rows, k_rows, sc_pair = pending
                if i + 1 < len(problems):
                    pending = scores(*problems[i + 1])
                vw = vbuf[k_rows, :].astype(BF16)
                halves = []
                for sc in sc_pair:
                    m_h = jnp.max(sc, axis=-1, keepdims=True)
                    p = jnp.exp(sc - m_h)
                    halves.append((m_h, jnp.sum(p, axis=-1, keepdims=True),
                                   jnp.dot(p.astype(BF16), vw, preferred_element_type=F32)))
                m_b, l_b, a_b = (jnp.where(lo_half, x0, x1) for x0, x1 in zip(*halves))
                if first:
                    m_n, l_n, a_n = m_b, l_b, a_b
                else:
                    m_o = m_s[q_rows, :]
                    m_n = jnp.maximum(m_o, m_b)
                    w_o = jnp.exp(m_o - m_n)
                    w_b = jnp.exp(m_b - m_n)
                    l_n = l_s[q_rows, :] * w_o + l_b * w_b
                    a_n = a_s[q_rows, :] * w_o + a_b * w_b
                m_s[q_rows, :] = m_n
                l_s[q_rows, :] = l_n
                a_s[q_rows, :] = a_n
            return carry

        lax.fori_loop(0, tile // (reps * span), block, 0)

    o_ref[...] = a_s[...] / l_s[...]


def dilated_attention(q, k, v):
    S, W = q.shape
    pair = 2 * HEAD_DIM
    per = ATTN_TILE // ATTN_HALO
    cur = pl.BlockSpec((ATTN_TILE, pair), lambda p, i: (i, p))
    prv = pl.BlockSpec((ATTN_HALO, pair), lambda p, i: (jnp.maximum(i * per - 1, 0), p))
    nxt = pl.BlockSpec((ATTN_HALO, pair), lambda p, i: (jnp.minimum((i + 1) * per, S // ATTN_HALO - 1), p))
    ext = ATTN_TILE + 2 * ATTN_HALO
    return pl.pallas_call(
        functools.partial(_attn_kernel, n_tok=S),
        grid=(W // pair, S // ATTN_TILE),
        in_specs=[cur, prv, cur, nxt, prv, cur, nxt],
        out_specs=cur,
        out_shape=jax.ShapeDtypeStruct((S, W), F32),
        scratch_shapes=[pltpu.VMEM((ext, pair), F32), pltpu.VMEM((ext, pair), F32)]
        + [pltpu.VMEM((ATTN_TILE, pair), F32)] * 3,
        compiler_params=_params("parallel", "arbitrary"),
        name="dilated_attention",
    )(q, k, k, k, v, v, v)


def _tri_masks(z):
    row = lax.broadcasted_iota(jnp.int32, (CHUNK, CHUNK), 0)
    col = lax.broadcasted_iota(jnp.int32, (CHUNK, CHUNK), 1)
    if z == 0:
        return col <= row, col < row
    return col >= row, col > row


RWKV_SUB_CHUNKS = 2
DN_SUB_CHUNKS = 4


def _sub_rows(z, j, sub):
    i = j if z == 0 else sub - 1 - j
    return slice(i * CHUNK, (i + 1) * CHUNK)


def _neumann_solve(ns, xs):
    steps = CHUNK.bit_length() - 1
    for i in range(steps):
        xs = [x + _bdot(n, x) for n, x in zip(ns, xs)]
        if i + 1 < steps:
            ns = [_bdot(n, n) for n in ns]
    return xs


def _rwkv_chunk_kernel(*refs, n_cast):
    sub = RWKV_SUB_CHUNKS
    ins = (refs[0:6], refs[6:12])
    cast_in = refs[12:12 + n_cast]
    y_refs = refs[12 + n_cast:14 + n_cast]
    cast_out = refs[14 + n_cast:14 + 2 * n_cast]
    state = refs[14 + 2 * n_cast]
    for src, dst in zip(cast_in, cast_out):
        dst[...] = src[...].astype(BF16)

    @pl.when(pl.program_id(0) == 0)
    def _():
        state[...] = jnp.zeros_like(state)

    chains = [(j, z, h) for j in range(sub) for z in range(2) for h in range(N_HEADS_B)]
    masks = [_tri_masks(z) for z in range(2)]
    prep = {}
    for j in range(sub):
        for z in range(2):
            rows = _sub_rows(z, j, sub)
            r_ref, v_ref, a_ref, lw_ref, k_ref, b_ref = ins[z]
            tri = jnp.where(masks[z][0], 1.0, 0.0).astype(BF16)
            lw = lw_ref[rows, :]
            hi, mid, lo = _split3(lw)
            cum = (jnp.dot(tri, hi, preferred_element_type=F32) + jnp.dot(tri, mid, preferred_element_type=F32)
                   + jnp.dot(tri, lo, preferred_element_type=F32))
            tot = jnp.sum(lw, axis=0, keepdims=True)
            e_neg = jnp.exp(-cum)
            e_end = jnp.exp(tot - cum)
            k = k_ref[rows, :]
            b = b_ref[rows, :]
            prep[j, z] = dict(rt=r_ref[rows, :] * jnp.exp(cum), at=a_ref[rows, :] * jnp.exp(cum - lw), kt=k * e_neg,
                              bt=b * e_neg, kh=k * e_end, bh=b * e_end, dw=jnp.exp(tot), v=v_ref[rows, :])

    def part(name, c):
        j, z, h = c
        return prep[j, z][name][:, h * HEAD_DIM:(h + 1) * HEAD_DIM]

    xs = [_bdot_nt(jnp.concatenate([part("at", c), part("rt", c)], axis=0),
                   jnp.concatenate([part("bt", c), part("kt", c)], axis=0)) for c in chains]
    a_ab = [jnp.where(masks[c[1]][1], x[:CHUNK, :CHUNK], 0.0) for c, x in zip(chains, xs)]
    a_rb = [jnp.where(masks[c[1]][0], x[CHUNK:, :CHUNK], 0.0) for c, x in zip(chains, xs)]
    cys = [_bdot(jnp.concatenate([jnp.where(masks[c[1]][1], x[:CHUNK, CHUNK:], 0.0),
                                  jnp.where(masks[c[1]][0], x[CHUNK:, CHUNK:], 0.0)], axis=0), part("v", c))
           for c, x in zip(chains, xs)]
    pqs = _neumann_solve(a_ab, [jnp.concatenate([part("at", c), cy[:CHUNK]], axis=1) for c, cy in zip(chains, cys)])
    kvs = [_bdot_tn(part("v", c), part("kh", c)) for c in chains]
    per = 2 * N_HEADS_B
    s_cur = [state[z, h] for _, z, h in chains[:per]]
    for j in range(sub):
        sel = slice(j * per, (j + 1) * per)
        cs = chains[sel]
        us = [_bdot_nt(pq[:, :HEAD_DIM], s) + pq[:, HEAD_DIM:] for pq, s in zip(pqs[sel], s_cur)]
        ys = [_bdot_nt(part("rt", c), s) for c, s in zip(cs, s_cur)]
        ys = [y + _bdot(arb, u) + cy[CHUNK:] for y, arb, u, cy in zip(ys, a_rb[sel], us, cys[sel])]
        s_cur = [s * part("dw", c) + _bdot_tn(u, part("bh", c)) + kv for c, s, u, kv in zip(cs, s_cur, us, kvs[sel])]
        for (_, z, h), y in zip(cs, ys):
            y_refs[z][_sub_rows(z, j, sub), h * HEAD_DIM:(h + 1) * HEAD_DIM] = y
    for (_, z, h), s in zip(chains[:per], s_cur):
        state[z, h] = s


def rwkv7_scan(r, v, a, lw, k, b, cast_through=()):
    S, C = r.shape
    step_rows = RWKV_SUB_CHUNKS * CHUNK
    n = S // step_rows
    fwd = pl.BlockSpec((step_rows, C), lambda c: (c, 0))
    bwd = pl.BlockSpec((step_rows, C), lambda c: (n - 1 - c, 0))
    cast_specs = [pl.BlockSpec((w.shape[0] // n, w.shape[1]), lambda c: (c, 0)) for w in cast_through]
    y0, y1, *cast = pl.pallas_call(
        functools.partial(_rwkv_chunk_kernel, n_cast=len(cast_through)),
        grid=(n,),
        in_specs=[fwd] * 6 + [bwd] * 6 + cast_specs,
        out_specs=[fwd, bwd] + cast_specs,
        out_shape=[jax.ShapeDtypeStruct((S, C), F32)] * 2 + [jax.ShapeDtypeStruct(w.shape, BF16) for w in cast_through],
        scratch_shapes=[pltpu.VMEM((2, N_HEADS_B, HEAD_DIM, HEAD_DIM), F32)],
        compiler_params=_params("arbitrary"),
        name="rwkv7_scan",
    )(r, v, a, lw[0], k[0], b[0], r, v, a, lw[1], k[1], b[1], *cast_through)
    return y0, y1, cast


def _dn_chunk_kernel(*refs):
    sub = DN_SUB_CHUNKS
    ins = (refs[0:6], refs[6:12])
    o_refs = refs[12:14]
    state = refs[14]

    @pl.when(pl.program_id(0) == 0)
    def _():
        state[...] = jnp.zeros_like(state)

    chains = [(j, z, h) for j in range(sub) for z in range(2) for h in range(N_HEADS_D)]
    masks = [_tri_masks(z) for z in range(2)]
    nt = (((1,), (1,)), ((), ()))
    gcs, decays, betas, g_lasts = [], [], [], []
    for j in range(sub):
        for z in range(2):
            rows = _sub_rows(z, j, sub)
            _, _, _, bcol_ref, gcol_ref, grow_ref = ins[z]
            incl = masks[z][0]
            tri = jnp.where(incl, 1.0, 0.0).astype(BF16)
            c_hi, c_mid, c_lo = _split3(gcol_ref[rows, :])
            gc_cols = (jnp.dot(tri, c_hi, preferred_element_type=F32) + jnp.dot(tri, c_mid, preferred_element_type=F32)
                       + jnp.dot(tri, c_lo, preferred_element_type=F32))
            r_hi, r_mid, r_lo = _split3(grow_ref[rows.start // CHUNK])
            gc_rows = (lax.dot_general(r_hi, tri, nt, preferred_element_type=F32)
                       + lax.dot_general(r_mid, tri, nt, preferred_element_type=F32)
                       + lax.dot_general(r_lo, tri, nt, preferred_element_type=F32))
            last = CHUNK - 1 if z == 0 else 0
            bcol = bcol_ref[rows, :]
            for h in range(N_HEADS_D):
                idx = z * N_HEADS_D + h
                gc = gc_cols[:, idx:idx + 1]
                diff = gc - gc_rows[idx:idx + 1, :]
                gcs.append(gc)
                decays.append(jnp.where(incl, jnp.exp(jnp.where(incl, diff, 0.0)), 0.0))
                betas.append(bcol[:, idx:idx + 1])
                g_lasts.append(gc[last:last + 1, :])

    def part(i, c):
        j, z, h = c
        return ins[z][i][_sub_rows(z, j, sub), h * HEAD_DIM_D:(h + 1) * HEAD_DIM_D]

    qs = [part(0, c) for c in chains]
    ks = [part(1, c) for c in chains]
    vs = [part(2, c) for c in chains]
    kbs = [k * beta for k, beta in zip(ks, betas)]
    e_gcs = [jnp.exp(gc) for gc in gcs]
    kqs = [_bdot_nt(jnp.concatenate([kb, q], axis=0), k) for kb, q, k in zip(kbs, qs, ks)]
    n_mats = [jnp.where(masks[c[1]][1], -(kq[:CHUNK] * dc), 0.0) for c, kq, dc in zip(chains, kqs, decays)]
    attns = [kq[CHUNK:] * dc for kq, dc in zip(kqs, decays)]
    uks = _neumann_solve(n_mats, [jnp.concatenate([v * beta, kb * e], axis=1)
                                  for v, beta, kb, e in zip(vs, betas, kbs, e_gcs)])
    per = 2 * N_HEADS_D
    s_cur = [state[z, h] for _, z, h in chains[:per]]
    for j in range(sub):
        sel = slice(j * per, (j + 1) * per)
        us = [uk[:, :HEAD_DIM_D] - _bdot(uk[:, HEAD_DIM_D:], s) for uk, s in zip(uks[sel], s_cur)]
        os_ = [_bdot(q * e, s) for q, e, s in zip(qs[sel], e_gcs[sel], s_cur)]
        os_ = [o + _bdot(attn, u) for o, attn, u in zip(os_, attns[sel], us)]
        s_cur = [s * jnp.exp(gl) + _bdot_tn(k * jnp.exp(gl - gc), u)
                 for s, gl, k, gc, u in zip(s_cur, g_lasts[sel], ks[sel], gcs[sel], us)]
        for (_, z, h), o in zip(chains[sel], os_):
            o_refs[z][_sub_rows(z, j, sub), h * HEAD_DIM_D:(h + 1) * HEAD_DIM_D] = o
    for (_, z, h), s in zip(chains[:per], s_cur):
        state[z, h] = s


def deltanet_scan(q, k, v, beta, g):
    S, C = q.shape
    step_rows = DN_SUB_CHUNKS * CHUNK
    n = S // step_rows
    nz = 2 * N_HEADS_D
    g_rows = g.reshape(S // CHUNK, CHUNK, nz).transpose(0, 2, 1)

    def specs(idx):
        wide = pl.BlockSpec((step_rows, C), lambda c: (idx(c), 0))
        col = pl.BlockSpec((step_rows, nz), lambda c: (idx(c), 0))
        row = pl.BlockSpec((DN_SUB_CHUNKS, nz, CHUNK), lambda c: (idx(c), 0, 0))
        return [wide, wide, wide, col, col, row], wide

    in_f, out_f = specs(lambda c: c)
    in_b, out_b = specs(lambda c: n - 1 - c)
    o0, o1 = pl.pallas_call(
        _dn_chunk_kernel,
        grid=(n,),
        in_specs=in_f + in_b,
        out_specs=[out_f, out_b],
        out_shape=[jax.ShapeDtypeStruct((S, C), F32)] * 2,
        scratch_shapes=[pltpu.VMEM((2, N_HEADS_D, HEAD_DIM_D, HEAD_DIM_D), F32)],
        compiler_params=_params("arbitrary"),
        name="deltanet_scan",
    )(q, k, v, beta, g, g_rows, q, k, v, beta, g, g_rows)
    return o0, o1


def _rope_tables(positions):
    half = ROPE_DIM // 2
    dim = jnp.arange(2 * HEAD_DIM) % HEAD_DIM
    freq = jnp.where(dim < ROPE_DIM, jnp.power(ROPE_THETA, -(dim % half).astype(F32) / half), 0.0)
    sign = jnp.where(dim < half, -1.0, 1.0)
    ang = positions[:, None].astype(F32) * freq
    return jnp.cos(ang), sign * jnp.sin(ang)


def _ones_block_diag(width, group):
    idx = jnp.arange(width) // group
    return (idx[:, None] == idx[None, :]).astype(BF16)


def _block_diag(blocks):
    rows = sum(b.shape[0] for b in blocks)
    cols = sum(b.shape[1] for b in blocks)
    out = jnp.zeros((rows, cols), blocks[0].dtype)
    r = c = 0
    for b in blocks:
        out = lax.dynamic_update_slice(out, b, (r, c))
        r += b.shape[0]
        c += b.shape[1]
    return out


def _even_layer(x, positions, mix_norm, w_in, q_norm, k_norm, shift_mu, lora_mu, w0, w1, w2, a0, a1, a2, g1, g2,
                k_k, k_a, r_k, ln_w, ln_b, w_out, ffn_norm, ffn_gate, ffn_up, ffn_down, cast_through=()):
    row = lambda t: t.reshape(1, -1)
    cos_t, sin_t = _rope_tables(positions)
    ones_bd = _ones_block_diag(WIDTH_B, HEAD_DIM)
    lora_in = jnp.concatenate([w1[0], w1[1], a1[0], a1[1], g1], axis=1).astype(BF16)
    lora_out = _block_diag([w2[0], w2[1], a2[0], a2[1], g2]).astype(BF16)
    (q, k, v, r, vb, a_vec, lw0, lw1, k0, k1, b0, b1, gate, bonus) = even_prep(
        x, mix_norm, w_in.astype(BF16), row(jnp.tile(q_norm, N_HEADS_A)), row(jnp.tile(k_norm, N_HEADS_A)), cos_t, sin_t,
        shift_mu, lora_mu, lora_in, lora_out, w0, a0, row(k_k), row(k_a), row(r_k), ones_bd)
    y_a = dilated_attention(q, k, v)
    y0, y1, cast = rwkv7_scan(r, vb, a_vec, (lw0, lw1), (k0, k1), (b0, b1),
                              [w.reshape(-1, w.shape[-1]) for w in cast_through])
    cast = [c.reshape(w.shape) for c, w in zip(cast, cast_through)]
    x = even_post_ffn(x, y_a, y0, y1, gate, bonus, row(ln_w), row(ln_b), ones_bd, w_out.astype(BF16), ffn_norm,
                      ffn_gate.astype(BF16), ffn_up.astype(BF16), ffn_down.astype(BF16))
    return x, cast


def _odd_layer(x, mix_norm, w_in, conv_c, conv_dn, A_log, dt_bias, dn_norm, w_out, ffn_norm, router, moe_gate, moe_up, moe_down):
    n_in = w_in.shape[1]
    n_pad = -(-n_in // LANES) * LANES
    w_in_p = jnp.pad(w_in, ((0, 0), (0, n_pad - n_in))).astype(BF16)
    nz = 2 * N_HEADS_D
    neg_a = jnp.zeros((1, LANES), F32).at[0, nz:2 * nz].set(-jnp.exp(A_log.reshape(-1)))
    dt_b = jnp.zeros((1, LANES), F32).at[0, nz:2 * nz].set(dt_bias.reshape(-1))
    ones_bd = _ones_block_diag(WIDTH_D, HEAD_DIM_D)
    y_c, q, k, v, zs, bg = odd_prep(x, mix_norm, w_in_p, conv_c, conv_dn, neg_a, dt_b, ones_bd)
    o0, o1 = deltanet_scan(q, k, v, bg[:, :nz], bg[:, nz:2 * nz])
    wr_pad = jnp.pad(router, ((0, 0), (0, LANES - N_EXPERTS)))
    x, route = odd_post(x, y_c, o0, o1, zs, jnp.tile(dn_norm, N_HEADS_D).reshape(1, -1), ones_bd, w_out.astype(BF16),
                        ffn_norm, wr_pad)
    return moe_top2(x, ffn_norm, route, moe_gate.astype(BF16), moe_up.astype(BF16), moe_down.astype(BF16))


def kernel(x, positions, ev_mix_norm, ev_w_in, ev_q_norm, ev_k_norm, ev_shift_mu, ev_lora_mu, ev_w0, ev_w1, ev_w2, ev_a0, ev_a1, ev_a2, ev_g1, ev_g2, ev_k_k, ev_k_a, ev_r_k, ev_ln_w, ev_ln_b, ev_w_out, ev_ffn_norm, ev_ffn_gate, ev_ffn_up, ev_ffn_down, od_mix_norm, od_w_in, od_conv_c, od_conv_dn, od_A_log, od_dt_bias, od_dn_norm, od_w_out, od_ffn_norm, od_router, od_moe_gate, od_moe_up, od_moe_down):
    B, S, D = x.shape
    assert B == 1
    xs = x.reshape(S, D)
    pos = positions.reshape(S)
    n_layers = ev_mix_norm.shape[0] + od_mix_norm.shape[0]
    for layer in range(n_layers):
        i = layer // 2
        if layer % 2 == 0:
            nxt = (od_moe_gate[i], od_moe_up[i], od_moe_down[i]) if layer + 1 < n_layers else ()
            xs, moe_w = _even_layer(xs, pos, ev_mix_norm[i], ev_w_in[i], ev_q_norm[i], ev_k_norm[i], ev_shift_mu[i],
                                    ev_lora_mu[i], ev_w0[i], ev_w1[i], ev_w2[i], ev_a0[i], ev_a1[i], ev_a2[i], ev_g1[i],
                                    ev_g2[i], ev_k_k[i], ev_k_a[i], ev_r_k[i], ev_ln_w[i], ev_ln_b[i], ev_w_out[i],
                                    ev_ffn_norm[i], ev_ffn_gate[i], ev_ffn_up[i], ev_ffn_down[i], cast_through=nxt)
        else:
            xs = _odd_layer(xs, od_mix_norm[i], od_w_in[i], od_conv_c[i], od_conv_dn[i], od_A_log[i], od_dt_bias[i],
                            od_dn_norm[i], od_w_out[i], od_ffn_norm[i], od_router[i], *moe_w)
    return xs.reshape(B, S, D)
```
